```python
import jax
import jax.numpy as jnp
from jax import lax
import numpy as np


D_MODEL = 1024
BATCH = 4
SEQ = 4096
DEPTH = 2

EPS = 1e-6
NEG_INF = -1e30
Q_BLOCK = 128

MLA_HEADS = 8
MLA_NOPE = 64
MLA_ROPE = 32
MLA_V = 64
MLA_Q_RANK = 384
MLA_KV_RANK = 256
ROPE_THETA = 10000.0

NSA_HEADS = 8
NSA_KV_HEADS = 2
NSA_REP = NSA_HEADS // NSA_KV_HEADS
NSA_DH = 64
L_CMP = 32
D_CMP = 16
CMP_HID = 128
L_SEL = 64
N_SEL = 16
W_WIN = 512

D_MIX = MLA_HEADS * MLA_V + NSA_HEADS * NSA_DH
NSA_KV = NSA_KV_HEADS * NSA_DH
D_IN = MLA_Q_RANK + MLA_KV_RANK + MLA_ROPE + NSA_HEADS * NSA_DH + 6 * NSA_KV + 3 * NSA_HEADS

D_FF = 2816
CONV_W = 3

kernel_name = 'mla_nsa_hybrid_convffn'


def rmsnorm(x, g):
    x32 = x.astype(jnp.float32)
    y = x32 * lax.rsqrt(jnp.mean(x32 * x32, axis=-1, keepdims=True) + EPS)
    return (y * g.astype(jnp.float32)).astype(x.dtype)


def alibi_slopes(n):
    return jnp.exp2(-8.0 * jnp.arange(1, n + 1, dtype=jnp.float32) / n)


def rope_tables(S, dim):
    inv = 1.0 / (ROPE_THETA ** (jnp.arange(0, dim, 2, dtype=jnp.float32) / dim))
    ang = jnp.arange(S, dtype=jnp.float32)[:, None] * inv[None, :]
    return jnp.cos(ang), jnp.sin(ang)


def apply_rope(x, cos, sin):
    x1, x2 = jnp.split(x, 2, axis=-1)
    c = cos.astype(x.dtype)
    s = sin.astype(x.dtype)
    return jnp.concatenate([x1 * c - x2 * s, x2 * c + x1 * s], axis=-1)


def mla_attention(c_q, c_kv, k_rope, q_norm, kv_norm, w_uq, w_ukv):
    B, S, _ = c_q.shape
    H = MLA_HEADS
    q = (rmsnorm(c_q, q_norm) @ w_uq).reshape(B, S, H, MLA_NOPE + MLA_ROPE)
    kv = (rmsnorm(c_kv, kv_norm) @ w_ukv).reshape(B, S, H, MLA_NOPE + MLA_V)
    q_nope, q_rope = q[..., :MLA_NOPE], q[..., MLA_NOPE:]
    k_nope, v = kv[..., :MLA_NOPE], kv[..., MLA_NOPE:]
    cos, sin = rope_tables(S, MLA_ROPE)
    q_rope = apply_rope(q_rope, cos[:, None, :], sin[:, None, :])
    k_rope = apply_rope(k_rope, cos, sin)
    scale = (MLA_NOPE + MLA_ROPE) ** -0.5
    nq = S // Q_BLOCK
    qn = q_nope.reshape(B, nq, Q_BLOCK, H, MLA_NOPE).transpose(1, 0, 2, 3, 4)
    qr = q_rope.reshape(B, nq, Q_BLOCK, H, MLA_ROPE).transpose(1, 0, 2, 3, 4)
    kpos = jnp.arange(S)

    def block(args):
        qn_b, qr_b, qb = args
        s = (jnp.einsum('bqhd,bkhd->bhqk', qn_b, k_nope)
             + jnp.einsum('bqhd,bkd->bhqk', qr_b, k_rope)).astype(jnp.float32) * scale
        qpos = qb * Q_BLOCK + jnp.arange(Q_BLOCK)
        s = jnp.where(kpos[None, :] <= qpos[:, None], s, NEG_INF)
        p = jax.nn.softmax(s, axis=-1).astype(v.dtype)
        return jnp.einsum('bhqk,bkhd->bqhd', p, v)

    o = lax.map(block, (qn, qr, jnp.arange(nq)))
    return o.transpose(1, 0, 2, 3, 4).reshape(B, S, H * MLA_V)


def nsa_attention(q, k_c, v_c, k_s, v_s, k_w, v_w, gates,
                  pos_k, pos_v, ck_w1, ck_w2, cv_w1, cv_w2):
    B, S, G, R, dh = q.shape
    dt = q.dtype
    slopes = alibi_slopes(NSA_HEADS).reshape(G, R)
    scale = dh ** -0.5
    t = jnp.arange(S)

    n_cmp = (S - L_CMP) // D_CMP + 1
    tok = jnp.arange(n_cmp)[:, None] * D_CMP + jnp.arange(L_CMP)[None, :]

    def compress(a, pos, w1, w2):
        blocks = a[:, tok] + pos[None, None, :, None, :]
        hid = jax.nn.gelu(jnp.einsum('bnlgd,lde->bnge', blocks, w1))
        return jnp.einsum('bnge,ed->bngd', hid, w2)

    kc = compress(k_c, pos_k, ck_w1, ck_w2)
    vc = compress(v_c, pos_v, cv_w1, cv_w2)
    blk_end = jnp.arange(n_cmp) * D_CMP + L_CMP - 1
    dist_c = t[:, None] - blk_end[None, :]
    mask_c = dist_c >= 0
    s_c = (jnp.einsum('bsgrd,bngd->bgrsn', q, kc).astype(jnp.float32) * scale
           - slopes[:, :, None, None] * dist_c.astype(jnp.float32))
    p_cmp = jax.nn.softmax(jnp.where(mask_c, s_c, NEG_INF), axis=-1) * mask_c
    o_cmp = jnp.einsum('bgrsn,bngd->bsgrd', p_cmp.astype(dt), vc)

    n_blk = S // L_SEL
    cs = jnp.arange(n_cmp) * D_CMP
    ss = jnp.arange(n_blk) * L_SEL
    overlap = ((cs[:, None] < ss[None, :] + L_SEL)
               & (cs[:, None] + L_CMP > ss[None, :])).astype(jnp.float32)
    imp = jnp.einsum('bgrsn,nj->bgsj', p_cmp, overlap)
    cur = (t // L_SEL)[:, None]
    j = jnp.arange(n_blk)[None, :]
    imp = jnp.where(j > cur, -jnp.inf, imp)
    imp = jnp.where((j == 0) | (j == cur) | (j == cur - 1), jnp.inf, imp)
    n_top = min(N_SEL, n_blk)
    _, idx = lax.top_k(imp, n_top)

    kb = k_s.reshape(B, n_blk, L_SEL, G, dh).transpose(0, 3, 1, 2, 4)
    vb = v_s.reshape(B, n_blk, L_SEL, G, dh).transpose(0, 3, 1, 2, 4)
    nq = S // Q_BLOCK
    q_ch = q.reshape(B, nq, Q_BLOCK, G, R, dh).transpose(1, 0, 2, 3, 4, 5)
    idx_ch = idx.reshape(B, G, nq, Q_BLOCK, n_top).transpose(2, 0, 1, 3, 4)
    bi = jnp.arange(B)[:, None, None, None]
    gi = jnp.arange(G)[None, :, None, None]

    def sel_block(args):
        q_b, idx_b, qb = args
        kg = kb[bi, gi, idx_b]
        vg = vb[bi, gi, idx_b]
        tq = qb * Q_BLOCK + jnp.arange(Q_BLOCK)
        kpos = idx_b[..., None] * L_SEL + jnp.arange(L_SEL)
        dist = (tq[None, None, :, None, None] - kpos)[:, :, None]
        s = (jnp.einsum('bqgrd,bgqnld->bgrqnl', q_b, kg).astype(jnp.float32) * scale
             - slopes[None, :, :, None, None, None] * dist.astype(jnp.float32))
        s = jnp.where(dist >= 0, s, NEG_INF).reshape(B, G, R, Q_BLOCK, n_top * L_SEL)
        p = jax.nn.softmax(s, axis=-1).reshape(B, G, R, Q_BLOCK, n_top, L_SEL).astype(dt)
        return jnp.einsum('bgrqnl,bgqnld->bqgrd', p, vg)

    o_sel = lax.map(sel_block, (q_ch, idx_ch, jnp.arange(nq)))
    o_sel = o_sel.transpose(1, 0, 2, 3, 4, 5).reshape(B, S, G, R, dh)

    n_prev = W_WIN // Q_BLOCK

    def bands(a):
        ap = jnp.pad(a, ((0, 0), (W_WIN, 0), (0, 0), (0, 0)))
        ap = ap.reshape(B, nq + n_prev, Q_BLOCK, G, dh)
        return jnp.concatenate([ap[:, i:i + nq] for i in range(n_prev + 1)], axis=2)

    kw = bands(k_w)
    vw = bands(v_w)
    qw = q.reshape(B, nq, Q_BLOCK, G, R, dh)
    tq = t.reshape(nq, Q_BLOCK)
    kpos = jnp.arange(nq)[:, None] * Q_BLOCK - W_WIN + jnp.arange((n_prev + 1) * Q_BLOCK)[None, :]
    dist_w = tq[:, :, None] - kpos[:, None, :]
    mask_w = (dist_w >= 0) & (dist_w < W_WIN) & (kpos[:, None, :] >= 0)
    s_w = (jnp.einsum('bnqgrd,bnkgd->bgrnqk', qw, kw).astype(jnp.float32) * scale
           - slopes[:, :, None, None, None] * dist_w.astype(jnp.float32))
    p_w = jax.nn.softmax(jnp.where(mask_w, s_w, NEG_INF), axis=-1).astype(dt)
    o_win = jnp.einsum('bgrnqk,bnkgd->bnqgrd', p_w, vw).reshape(B, S, G, R, dh)

    g = jax.nn.sigmoid(gates.astype(jnp.float32)).astype(dt)
    o = g[..., 0:1] * o_cmp + g[..., 1:2] * o_sel + g[..., 2:3] * o_win
    return o.reshape(B, S, G * R * dh)


def token_mixer(n, w_in, q_norm, kv_norm, w_uq, w_ukv, pos_k, pos_v,
                ck_w1, ck_w2, cv_w1, cv_w2, w_o):
    B, S, _ = n.shape
    z = n @ w_in
    sizes = [MLA_Q_RANK, MLA_KV_RANK, MLA_ROPE, NSA_HEADS * NSA_DH] + [NSA_KV] * 6 + [3 * NSA_HEADS]
    offs = [int(o) for o in np.cumsum(sizes)[:-1]]
    (c_q, c_kv, k_rope, q_n, kc, vc, ks, vs, kwin, vwin, gt) = jnp.split(z, offs, axis=-1)
    o_mla = mla_attention(c_q, c_kv, k_rope, q_norm, kv_norm, w_uq, w_ukv)
    G, R, dh = NSA_KV_HEADS, NSA_REP, NSA_DH
    kvs = (B, S, G, dh)
    o_nsa = nsa_attention(q_n.reshape(B, S, G, R, dh),
                          kc.reshape(kvs), vc.reshape(kvs), ks.reshape(kvs), vs.reshape(kvs),
                          kwin.reshape(kvs), vwin.reshape(kvs), gt.reshape(B, S, G, R, 3),
                          pos_k, pos_v, ck_w1, ck_w2, cv_w1, cv_w2)
    return jnp.concatenate([o_mla, o_nsa], axis=-1) @ w_o


def conv_ffn(x, norm_g, w_up, conv_w, conv_b, w_down):
    S = x.shape[1]
    h = rmsnorm(x, norm_g) @ w_up
    hp = jnp.pad(h, ((0, 0), (CONV_W - 1, 0), (0, 0)))
    hc = conv_b
    for i in range(CONV_W):
        hc = hc + hp[:, i:i + S] * conv_w[i]
    gate, up = jnp.split(hc, 2, axis=-1)
    return (jax.nn.silu(gate) * up) @ w_down


def setup_inputs(seed: int = 0) -> dict:
    key = jax.random.key(seed)
    ks = jax.random.split(key, 20)
    f32 = jnp.float32
    L = DEPTH

    def nrm(k, shape, fan_in):
        return jax.random.normal(k, shape, f32) * (fan_in ** -0.5)

    def gain(k, shape):
        return 1.0 + 0.01 * jax.random.normal(k, shape, f32)

    return {
        'x': jax.random.normal(ks[0], (BATCH, SEQ, D_MODEL), f32),
        'attn_norm': gain(ks[1], (L, D_MODEL)),
        'w_in': nrm(ks[2], (L, D_MODEL, D_IN), D_MODEL),
        'q_norm': gain(ks[3], (L, MLA_Q_RANK)),
        'kv_norm': gain(ks[4], (L, MLA_KV_RANK)),
        'w_uq': nrm(ks[5], (L, MLA_Q_RANK, MLA_HEADS * (MLA_NOPE + MLA_ROPE)), MLA_Q_RANK),
        'w_ukv': nrm(ks[6], (L, MLA_KV_RANK, MLA_HEADS * (MLA_NOPE + MLA_V)), MLA_KV_RANK),
        'cmp_pos_k': 0.1 * jax.random.normal(ks[7], (L, L_CMP, NSA_DH), f32),
        'cmp_pos_v': 0.1 * jax.random.normal(ks[8], (L, L_CMP, NSA_DH), f32),
        'cmp_k_w1': nrm(ks[9], (L, L_CMP, NSA_DH, CMP_HID), L_CMP * NSA_DH),
        'cmp_k_w2': nrm(ks[10], (L, CMP_HID, NSA_DH), CMP_HID),
        'cmp_v_w1': nrm(ks[11], (L, L_CMP, NSA_DH, CMP_HID), L_CMP * NSA_DH),
        'cmp_v_w2': nrm(ks[12], (L, CMP_HID, NSA_DH), CMP_HID),
        'w_o': nrm(ks[13], (L, D_MIX, D_MODEL), D_MIX),
        'ffn_norm': gain(ks[14], (L, D_MODEL)),
        'w_up': nrm(ks[15], (L, D_MODEL, 2 * D_FF), D_MODEL),
        'conv_w': nrm(ks[16], (L, CONV_W, 2 * D_FF), CONV_W),
        'conv_b': 0.01 * jax.random.normal(ks[17], (L, 2 * D_FF), f32),
        'w_down': nrm(ks[18], (L, D_FF, D_MODEL), D_FF),
        'final_norm': gain(ks[19], (D_MODEL,)),
    }


def reference(x, attn_norm, w_in, q_norm, kv_norm, w_uq, w_ukv, cmp_pos_k, cmp_pos_v,
              cmp_k_w1, cmp_k_w2, cmp_v_w1, cmp_v_w2, w_o, ffn_norm, w_up, conv_w,
              conv_b, w_down, final_norm):
    h = x
    for l in range(DEPTH):
        h = h + token_mixer(rmsnorm(h, attn_norm[l]), w_in[l], q_norm[l], kv_norm[l],
                            w_uq[l], w_ukv[l], cmp_pos_k[l], cmp_pos_v[l],
                            cmp_k_w1[l], cmp_k_w2[l], cmp_v_w1[l], cmp_v_w2[l], w_o[l])
        h = h + conv_ffn(h, ffn_norm[l], w_up[l], conv_w[l], conv_b[l], w_down[l])
    return rmsnorm(h, final_norm)
```

```python
import functools

import numpy as np
import jax
import jax.numpy as jnp
from jax import lax
from jax.experimental import pallas as pl
from jax.experimental.pallas import tpu as pltpu

F32 = jnp.float32
BF16 = jnp.bfloat16

D_MODEL = 1024
DEPTH = 2
EPS = 1e-6
NEG_INF = -1e30
LANES = 128

MLA_HEADS = 8
MLA_NOPE = 64
MLA_ROPE = 32
MLA_V = 64
MLA_Q_RANK = 384
MLA_KV_RANK = 256
MLA_QK = MLA_NOPE + MLA_ROPE
ROPE_THETA = 10000.0

NSA_HEADS = 8
NSA_KV_HEADS = 2
NSA_REP = NSA_HEADS // NSA_KV_HEADS
NSA_DH = 64
L_CMP = 32
D_CMP = 16
CMP_HID = 128
L_SEL = 64
N_SEL = 16
W_WIN = 512

D_FF = 2816
CONV_W = 3

_SIZES = [MLA_Q_RANK, MLA_KV_RANK, MLA_ROPE, NSA_HEADS * NSA_DH] + [NSA_KV_HEADS * NSA_DH] * 6 + [3 * NSA_HEADS]
_OFFS = [0] + [int(o) for o in np.cumsum(_SIZES)]
(OFF_CQ, OFF_CKV, OFF_KROPE, OFF_QN, OFF_KC, OFF_VC, OFF_KS, OFF_VS, OFF_KW, OFF_VW, OFF_GT, D_IN) = _OFFS

TM = 512
TKV = 256
TQ_MLA = 256
TQ_NSA = 128
VMEM_LIMIT = 56 * 1024 * 1024


def _cparams(sem):
    return pltpu.CompilerParams(dimension_semantics=sem, vmem_limit_bytes=VMEM_LIMIT)


def _dot(a, b):
    return jnp.dot(a, b, preferred_element_type=F32)


def _dot_nt(a, b):
    return lax.dot_general(a, b, (((1,), (1,)), ((), ())), preferred_element_type=F32)


def _rms(x, g):
    return x * lax.rsqrt(jnp.mean(x * x, axis=-1, keepdims=True) + EPS) * g


def _inproj_kernel(x_ref, g_ref, wa_ref, wb_ref, wq_ref, wk_ref, wvt_ref, qaug_ref, posf_ref,
                   za_ref, gt_ref, kcr_ref, vcr_ref, qn_ref, ksw_ref, vt_ref):
    n = _rms(x_ref[...], g_ref[...]).astype(BF16)
    za_ref[...] = _dot(n, wa_ref[...])
    zb = _dot(n, wb_ref[...])
    gt_ref[...] = zb[:, 0:LANES]
    kcr_ref[...] = zb[:, LANES:2 * LANES]
    vcr_ref[...] = zb[:, 2 * LANES:3 * LANES]
    qn_ref[...] = (_dot(n, wq_ref[...]) + qaug_ref[...]).astype(BF16)
    ksw_ref[...] = (_dot(n, wk_ref[...]) + posf_ref[...]).astype(BF16)
    vt = _dot_nt(wvt_ref[...], n).astype(BF16)
    for c in range(TM // TKV):
        vt_ref[0, c] = vt[:, c * TKV:(c + 1) * TKV]


def _inproj(h, g, wa, wb, wq, wk, wvt, qaug, posf, B, S):
    M = B * S
    spt = S // TM
    full = lambda a: pl.BlockSpec(a.shape, lambda i: (0,) * a.ndim)
    row = lambda w: pl.BlockSpec((TM, w), lambda i: (i, 0))
    return pl.pallas_call(
        _inproj_kernel,
        grid=(M // TM,),
        in_specs=[row(D_MODEL), full(g), full(wa), full(wb), full(wq), full(wk), full(wvt), full(qaug),
                  pl.BlockSpec((TM, 4 * LANES), lambda i: (i % spt, 0))],
        out_specs=[row(6 * LANES), row(LANES), row(LANES), row(LANES), row(8 * LANES), row(4 * LANES),
                   pl.BlockSpec((1, TM // TKV, 4 * NSA_DH, TKV), lambda i: (i // spt, i % spt, 0, 0))],
        out_shape=[jax.ShapeDtypeStruct((M, 6 * LANES), F32),
                   jax.ShapeDtypeStruct((M, LANES), F32),
                   jax.ShapeDtypeStruct((M, LANES), F32),
                   jax.ShapeDtypeStruct((M, LANES), F32),
                   jax.ShapeDtypeStruct((M, 8 * LANES), BF16),
                   jax.ShapeDtypeStruct((M, 4 * LANES), BF16),
                   jax.ShapeDtypeStruct((B, S // TKV, 4 * NSA_DH, TKV), BF16)],
        compiler_params=_cparams(("parallel",)),
        name="attn_inproj",
    )(h, g, wa, wb, wq, wk, wvt, qaug, posf)


def _rope_slab(x, ct, sa, sb):
    return x * ct + pltpu.roll(x, LANES - MLA_ROPE // 2, 1) * sa + pltpu.roll(x, MLA_ROPE // 2, 1) * sb


def _mla_prep_kernel(za_ref, qg_ref, kg_ref, wq_ref, wk_ref, wvt_ref, ct_ref, sa_ref, sb_ref,
                     q_ref, k_ref, vt_ref):
    za = za_ref[...]
    cq = _rms(za[:, 0:MLA_Q_RANK], qg_ref[...]).astype(BF16)
    ckv = _rms(za[:, MLA_Q_RANK:MLA_Q_RANK + MLA_KV_RANK], kg_ref[...]).astype(BF16)
    ct, sa, sb = ct_ref[...], sa_ref[...], sb_ref[...]
    kr = _rope_slab(za[:, 5 * LANES:6 * LANES], ct, sa, sb)
    q = _dot(cq, wq_ref[...])
    k = _dot(ckv, wk_ref[...])
    for h in range(MLA_HEADS):
        sl = slice(h * LANES, (h + 1) * LANES)
        q_ref[:, sl] = _rope_slab(q[:, sl], ct, sa, sb).astype(BF16)
        k_ref[:, sl] = (k[:, sl] + kr).astype(BF16)
    vt = _dot_nt(wvt_ref[...], ckv).astype(BF16)
    for c in range(TM // TKV):
        vt_ref[0, c] = vt[:, c * TKV:(c + 1) * TKV]


def _mla_prep(za, qg, kg, wq, wk, wvt, ct, sa, sb, B, S):
    M = B * S
    spt = S // TM
    full = lambda a: pl.BlockSpec(a.shape, lambda i: (0,) * a.ndim)
    row = lambda w: pl.BlockSpec((TM, w), lambda i: (i, 0))
    tab = pl.BlockSpec((TM, LANES), lambda i: (i % spt, 0))
    return pl.pallas_call(
        _mla_prep_kernel,
        grid=(M // TM,),
        in_specs=[row(6 * LANES), full(qg), full(kg), full(wq), full(wk), full(wvt), tab, tab, tab],
        out_specs=[row(8 * LANES), row(8 * LANES),
                   pl.BlockSpec((1, TM // TKV, MLA_HEADS * MLA_V, TKV), lambda i: (i // spt, i % spt, 0, 0))],
        out_shape=[jax.ShapeDtypeStruct((M, 8 * LANES), BF16),
                   jax.ShapeDtypeStruct((M, 8 * LANES), BF16),
                   jax.ShapeDtypeStruct((B, S // TKV, MLA_HEADS * MLA_V, TKV), BF16)],
        compiler_params=_cparams(("parallel",)),
        name="mla_prep",
    )(za, qg, kg, wq, wk, wvt, ct, sa, sb)


def _online_step(s, vt, m, l, acc):
    m_new = jnp.maximum(m, jnp.max(s, axis=0, keepdims=True))
    alpha = jnp.exp(m - m_new)
    p = jnp.exp(s - m_new)
    l = alpha * l + jnp.sum(p, axis=0, keepdims=True)
    acc = alpha * acc + _dot(vt, p.astype(BF16))
    return m_new, l, acc


def _mla_attn_kernel(q_ref, k_ref, vt_ref, o_ref):
    i = pl.program_id(2)
    tq = TQ_MLA
    outs = []
    for hh in range(2):
        sl = slice(hh * LANES, (hh + 1) * LANES)
        vsl = slice(hh * MLA_V, (hh + 1) * MLA_V)
        q = q_ref[0, :, sl]

        def tile(kt, carry, masked, q=q, sl=sl, vsl=vsl):
            k = k_ref[0, pl.ds(pl.multiple_of(kt * TKV, TKV), TKV), sl]
            s = _dot_nt(k, q)
            if masked:
                kpos = kt * TKV + lax.broadcasted_iota(jnp.int32, (TKV, tq), 0)
                qpos = i * tq + lax.broadcasted_iota(jnp.int32, (TKV, tq), 1)
                s = jnp.where(kpos <= qpos, s, NEG_INF)
            return _online_step(s, vt_ref[0, kt, vsl, :], *carry)

        init = (jnp.full((1, tq), -jnp.inf, F32), jnp.zeros((1, tq), F32), jnp.zeros((MLA_V, tq), F32))
        n_full = (i * tq) // TKV
        carry = lax.fori_loop(0, n_full, lambda kt, c: tile(kt, c, False), init)
        for d in range(tq // TKV):
            carry = tile(n_full + d, carry, True)
        m, l, acc = carry
        outs.append(acc / l)
    o_ref[0] = jnp.concatenate(outs, axis=0).T.astype(BF16)


def _mla_attn(q, k, vt, B, S):
    nkt = S // TKV
    return pl.pallas_call(
        _mla_attn_kernel,
        grid=(B, MLA_HEADS // 2, S // TQ_MLA),
        in_specs=[pl.BlockSpec((1, TQ_MLA, 2 * LANES), lambda b, hp, i: (b, i, hp)),
                  pl.BlockSpec((1, S, 2 * LANES), lambda b, hp, i: (b, 0, hp)),
                  pl.BlockSpec((1, nkt, 2 * MLA_V, TKV), lambda b, hp, i: (b, 0, hp, 0))],
        out_specs=pl.BlockSpec((1, TQ_MLA, 2 * MLA_V), lambda b, hp, i: (b, i, hp)),
        out_shape=jax.ShapeDtypeStruct((B, S, MLA_HEADS * MLA_V), BF16),
        compiler_params=_cparams(("parallel", "parallel", "arbitrary")),
        name="mla_attn",
    )(q, k, vt)


def _gelu_tanh(x):
    return 0.5 * x * (1.0 + jnp.tanh(np.sqrt(2.0 / np.pi).astype(np.float32) * (x + 0.044715 * (x * x * x))))


def _compress_kernel(kx_ref, vx_ref, pak_ref, pbk_ref, pav_ref, pbv_ref, wak_ref, wbk_ref, wav_ref, wbv_ref,
                     w2k_ref, w2vt_ref, feat_ref, kc_ref, vct_ref):
    nc = kx_ref.shape[1]

    def hidden(x, pa, pb, wa, wb):
        ya = _dot((x + pa).astype(BF16), wa)
        yb = _dot((x + pb).astype(BF16), wb)
        return _gelu_tanh(ya + pltpu.roll(yb, nc - 1, 0)).astype(BF16)

    hk = hidden(kx_ref[0], pak_ref[...], pbk_ref[...], wak_ref[...], wbk_ref[...])
    hv = hidden(vx_ref[0], pav_ref[...], pbv_ref[...], wav_ref[...], wbv_ref[...])
    kc_ref[0] = (_dot(hk, w2k_ref[...]) + feat_ref[...]).astype(BF16)
    vct_ref[0] = _dot_nt(w2vt_ref[...], hv).astype(BF16)


def _compress(kx, vx, pak, pbk, pav, pbv, wak, wbk, wav, wbv, w2k, w2vt, feat):
    B, nc, width = kx.shape
    full = lambda a: pl.BlockSpec(a.shape, lambda b: (0,) * a.ndim)
    xs = pl.BlockSpec((1, nc, width), lambda b: (b, 0, 0))
    return pl.pallas_call(
        _compress_kernel,
        grid=(B,),
        in_specs=[xs, xs] + [full(a) for a in (pak, pbk, pav, pbv, wak, wbk, wav, wbv, w2k, w2vt, feat)],
        out_specs=[pl.BlockSpec((1, nc, 2 * LANES), lambda b: (b, 0, 0)),
                   pl.BlockSpec((1, 2 * NSA_DH, nc), lambda b: (b, 0, 0))],
        out_shape=[jax.ShapeDtypeStruct((B, nc, 2 * LANES), BF16),
                   jax.ShapeDtypeStruct((B, 2 * NSA_DH, nc), BF16)],
        compiler_params=_cparams(("parallel",)),
        name="nsa_compress",
    )(kx, vx, pak, pbk, pav, pbv, wak, wbk, wav, wbv, w2k, w2vt, feat)


def _nsa_attn_kernel(qn_ref, kc_ref, vct_ref, ks_ref, kw_ref, vst_ref, vwt_ref, gt_ref, ovt_ref,
                     o_ref, mb_ref):
    g = pl.program_id(1)
    i = pl.program_id(2)
    tq = TQ_NSA
    nq4 = NSA_REP * tq
    nc = kc_ref.shape[1]
    nblk = ovt_ref.shape[0]
    t0 = i * tq
    qst = jnp.concatenate([qn_ref[0, :, r * LANES:(r + 1) * LANES] for r in range(NSA_REP)], axis=0)
    tpos = t0 + lax.broadcasted_iota(jnp.int32, (1, tq), 1)
    tpos4 = jnp.concatenate([tpos] * NSA_REP, axis=1)

    sc = _dot_nt(kc_ref[0], qst)
    nrow = lax.broadcasted_iota(jnp.int32, (nc, 1), 0)
    mask_c = (tpos4 >= nrow * D_CMP + (L_CMP - 1)) & (nrow < nc - 1)
    scm = jnp.where(mask_c, sc, NEG_INF)
    ec = jnp.exp(scm - jnp.max(scm, axis=0, keepdims=True))
    pc = jnp.where(mask_c, ec / jnp.sum(ec, axis=0, keepdims=True), 0.0)
    o_cmp = _dot(vct_ref[0], pc.astype(BF16))

    psum = pc[:, 0:tq]
    for r in range(1, NSA_REP):
        psum = psum + pc[:, r * tq:(r + 1) * tq]
    imp = jnp.dot(ovt_ref[...], psum, preferred_element_type=F32, precision=lax.Precision.HIGHEST)
    cur = lax.shift_right_logical(tpos, 6)
    jrow = lax.broadcasted_iota(jnp.int32, (nblk, 1), 0)
    imp = jnp.where(jrow > cur, -jnp.inf, imp)
    imp = jnp.where((jrow == 0) | (jrow == cur) | (jrow == cur - 1), jnp.inf, imp)
    rank = jnp.zeros((nblk, tq), F32)
    for jj in range(nblk):
        row = imp[jj:jj + 1, :]
        beats = (row > imp) | ((row == imp) & (jrow > jj))
        rank = rank + jnp.where(beats, 1.0, 0.0)
    mb_ref[...] = jnp.where(rank < float(min(N_SEL, nblk)), 0.0, NEG_INF)

    init = (jnp.full((1, nq4), -jnp.inf, F32), jnp.zeros((1, nq4), F32), jnp.zeros((NSA_DH, nq4), F32))
    bpt = TKV // L_SEL

    def sel_tile(kt, carry, masked):
        k = ks_ref[0, pl.ds(pl.multiple_of(kt * TKV, TKV), TKV), :]
        s = _dot_nt(k, qst)
        bias = jnp.concatenate(
            [jnp.broadcast_to(mb_ref[pl.ds(kt * bpt + c, 1), :], (L_SEL, tq)) for c in range(bpt)], axis=0)
        if masked:
            kpos = kt * TKV + lax.broadcasted_iota(jnp.int32, (TKV, tq), 0)
            bias = jnp.where(kpos <= tpos, bias, NEG_INF)
        s = s + jnp.concatenate([bias] * NSA_REP, axis=1)
        return _online_step(s, vst_ref[0, kt], *carry)

    kd = t0 // TKV
    carry = lax.fori_loop(0, kd, lambda kt, c: sel_tile(kt, c, False), init)
    _, l_sel, a_sel = sel_tile(kd, carry, True)
    o_sel = a_sel / l_sel

    def win_tile(kt, carry):
        k = kw_ref[0, pl.ds(pl.multiple_of(kt * TKV, TKV), TKV), :]
        s = _dot_nt(k, qst)
        dist = tpos - (kt * TKV + lax.broadcasted_iota(jnp.int32, (TKV, tq), 0))
        bias = jnp.where((dist >= 0) & (dist < W_WIN), 0.0, NEG_INF)
        s = s + jnp.concatenate([bias] * NSA_REP, axis=1)
        return _online_step(s, vwt_ref[0, kt], *carry)

    k_lo = jnp.maximum(t0 - W_WIN, 0) // TKV
    _, l_win, a_win = lax.fori_loop(k_lo, kd + 1, win_tile, init)
    o_win = a_win / l_win

    gt_t = gt_ref[0].T
    band = jnp.where(g == 0, gt_t[0:16, :], gt_t[16:32, :])
    sig = 1.0 / (1.0 + jnp.exp(-band))
    outs = []
    for r in range(NSA_REP):
        sl = slice(r * tq, (r + 1) * tq)
        outs.append(sig[4 * r:4 * r + 1, :] * o_cmp[:, sl] + sig[4 * r + 1:4 * r + 2, :] * o_sel[:, sl]
                    + sig[4 * r + 2:4 * r + 3, :] * o_win[:, sl])
    o_ref[0] = jnp.concatenate(outs, axis=0).T.astype(BF16)


def _nsa_attn(qn, kc, vct, ksw, vt, gt, ovt, B, S):
    nkt = S // TKV
    nc = kc.shape[1]
    G = NSA_KV_HEADS
    qn3 = qn.reshape(B, S, NSA_HEADS * LANES)
    ksw3 = ksw.reshape(B, S, 4 * LANES)
    gt3 = gt.reshape(B, S, LANES)
    return pl.pallas_call(
        _nsa_attn_kernel,
        grid=(B, G, S // TQ_NSA),
        in_specs=[pl.BlockSpec((1, TQ_NSA, NSA_REP * LANES), lambda b, g, i: (b, i, g)),
                  pl.BlockSpec((1, nc, LANES), lambda b, g, i: (b, 0, g)),
                  pl.BlockSpec((1, NSA_DH, nc), lambda b, g, i: (b, g, 0)),
                  pl.BlockSpec((1, S, LANES), lambda b, g, i: (b, 0, g)),
                  pl.BlockSpec((1, S, LANES), lambda b, g, i: (b, 0, G + g)),
                  pl.BlockSpec((1, nkt, NSA_DH, TKV), lambda b, g, i: (b, 0, g, 0)),
                  pl.BlockSpec((1, nkt, NSA_DH, TKV), lambda b, g, i: (b, 0, G + g, 0)),
                  pl.BlockSpec((1, TQ_NSA, LANES), lambda b, g, i: (b, i, 0)),
                  pl.BlockSpec(ovt.shape, lambda b, g, i: (0, 0))],
        out_specs=pl.BlockSpec((1, TQ_NSA, NSA_REP * NSA_DH), lambda b, g, i: (b, i, g)),
        out_shape=jax.ShapeDtypeStruct((B, S, NSA_HEADS * NSA_DH), BF16),
        scratch_shapes=[pltpu.VMEM((ovt.shape[0], TQ_NSA), F32)],
        compiler_params=_cparams(("parallel", "parallel", "arbitrary")),
        name="nsa_attn",
    )(qn3, kc, vct, ksw3, ksw3, vt, vt, gt3, ovt)


def _outproj_kernel(h_ref, a_ref, b_ref, wa_ref, wb_ref, o_ref):
    o_ref[...] = h_ref[...] + _dot(a_ref[...], wa_ref[...]) + _dot(b_ref[...], wb_ref[...])


def _outproj(h, oa, ob, wa, wb):
    M = h.shape[0]
    full = lambda a: pl.BlockSpec(a.shape, lambda i: (0,) * a.ndim)
    row = lambda w: pl.BlockSpec((TM, w), lambda i: (i, 0))
    return pl.pallas_call(
        _outproj_kernel,
        grid=(M // TM,),
        in_specs=[row(D_MODEL), row(oa.shape[1]), row(ob.shape[1]), full(wa), full(wb)],
        out_specs=row(D_MODEL),
        out_shape=jax.ShapeDtypeStruct((M, D_MODEL), F32),
        compiler_params=_cparams(("parallel",)),
        name="attn_outproj",
    )(h, oa, ob, wa, wb)


FF_CHUNK = 256
HALO = 8


def _ffn_kernel(x_ref, xp_ref, g_ref, wup_ref, cw_ref, cb_ref, wdn_ref, fg_ref, o_ref, *, final, spt):
    i = pl.program_id(0)
    x = x_ref[...]
    xe = jnp.concatenate([xp_ref[...], x], axis=0)
    n = _rms(xe, g_ref[...]).astype(BF16)
    keep = jnp.where(i % spt == 0, 0.0, 1.0)
    rows = lax.broadcasted_iota(jnp.int32, (HALO + TM, 1), 0)
    halo_scale = jnp.where(rows < HALO, keep, 1.0)
    acc = x

    def conv(hx, col):
        hx = hx * halo_scale
        cw = cw_ref[:, col:col + FF_CHUNK]
        out = cb_ref[:, col:col + FF_CHUNK]
        for j in range(CONV_W):
            lo = HALO - (CONV_W - 1) + j
            out = out + hx[lo:lo + TM, :] * cw[j:j + 1, :]
        return out

    for c in range(D_FF // FF_CHUNK):
        col = c * FF_CHUNK
        gate = conv(_dot(n, wup_ref[:, col:col + FF_CHUNK]), col)
        up = conv(_dot(n, wup_ref[:, D_FF + col:D_FF + col + FF_CHUNK]), D_FF + col)
        act = (gate * (1.0 / (1.0 + jnp.exp(-gate))) * up).astype(BF16)
        acc = acc + _dot(act, wdn_ref[col:col + FF_CHUNK, :])
    if final:
        acc = _rms(acc, fg_ref[...])
    o_ref[...] = acc


def _ffn(h, g, wup, cw, cb, wdn, fg, S, final):
    M = h.shape[0]
    spt = S // TM
    full = lambda a: pl.BlockSpec(a.shape, lambda i: (0,) * a.ndim)
    row = pl.BlockSpec((TM, D_MODEL), lambda i: (i, 0))
    prev = pl.BlockSpec((HALO, D_MODEL), lambda i: (jnp.maximum(i * (TM // HALO) - 1, 0), 0))
    return pl.pallas_call(
        functools.partial(_ffn_kernel, final=final, spt=spt),
        grid=(M // TM,),
        in_specs=[row, prev, full(g), full(wup), full(cw), full(cb), full(wdn), full(fg)],
        out_specs=row,
        out_shape=jax.ShapeDtypeStruct((M, D_MODEL), F32),
        compiler_params=_cparams(("parallel",)),
        name="conv_ffn",
    )(h, h, g, wup, cw, cb, wdn, fg)


def _gather_cols(w, idx):
    idx = np.asarray(idx)
    return jnp.where(jnp.asarray(idx >= 0)[None, :], w[:, np.maximum(idx, 0)], 0.0)


def _slab_idx(n_slabs, width, base_fn):
    idx = -np.ones((n_slabs, LANES), np.int64)
    for s in range(n_slabs):
        idx[s, :width] = base_fn(s) + np.arange(width)
    return idx.reshape(-1)


def _inproj_weights(w_in):
    G, dh = NSA_KV_HEADS, NSA_DH
    ia = -np.ones(6 * LANES, np.int64)
    ia[0:MLA_Q_RANK] = OFF_CQ + np.arange(MLA_Q_RANK)
    ia[MLA_Q_RANK:MLA_Q_RANK + MLA_KV_RANK] = OFF_CKV + np.arange(MLA_KV_RANK)
    ia[5 * LANES + MLA_NOPE:5 * LANES + MLA_NOPE + MLA_ROPE] = OFF_KROPE + np.arange(MLA_ROPE)
    igt = -np.ones(LANES, np.int64)
    for g in range(G):
        for r in range(NSA_REP):
            for c in range(3):
                igt[g * 16 + r * 4 + c] = OFF_GT + (g * NSA_REP + r) * 3 + c
    ib = np.concatenate([igt, OFF_KC + np.arange(G * dh), OFF_VC + np.arange(G * dh)])
    iq = _slab_idx(NSA_HEADS, dh, lambda h: OFF_QN + h * dh)
    ik = np.concatenate([_slab_idx(G, dh, lambda g: OFF_KS + g * dh), _slab_idx(G, dh, lambda g: OFF_KW + g * dh)])
    iv = np.concatenate([OFF_VS + np.arange(G * dh), OFF_VW + np.arange(G * dh)])
    wa = _gather_cols(w_in, ia).astype(BF16)
    wb = _gather_cols(w_in, ib).astype(BF16)
    wq = (_gather_cols(w_in, iq) * (NSA_DH ** -0.5)).astype(BF16)
    wk = _gather_cols(w_in, ik).astype(BF16)
    wvt = w_in[:, iv].T.astype(BF16)
    return wa, wb, wq, wk, wvt


def _const_tables(S):
    slopes = 2.0 ** (-8.0 * np.arange(1, NSA_HEADS + 1) / NSA_HEADS)
    qaug = np.zeros((1, NSA_HEADS * LANES), np.float32)
    for h in range(NSA_HEADS):
        qaug[0, h * LANES + NSA_DH] = slopes[h] * L_SEL
        qaug[0, h * LANES + NSA_DH + 1] = slopes[h]
    t = np.arange(S)
    posf = np.zeros((S, 4 * LANES), np.float32)
    for s in range(4):
        posf[:, s * LANES + NSA_DH] = t // L_SEL
        posf[:, s * LANES + NSA_DH + 1] = t % L_SEL
    nc = S // D_CMP
    bend = np.arange(nc) * D_CMP + L_CMP - 1
    feat = np.zeros((nc, 2 * LANES), np.float32)
    for s in range(2):
        feat[:, s * LANES + NSA_DH] = bend // L_SEL
        feat[:, s * LANES + NSA_DH + 1] = bend % L_SEL
    n_cmp = (S - L_CMP) // D_CMP + 1
    nblk = S // L_SEL
    cs = np.arange(n_cmp) * D_CMP
    ss = np.arange(nblk) * L_SEL
    ov = ((cs[:, None] < ss[None, :] + L_SEL) & (cs[:, None] + L_CMP > ss[None, :])).astype(np.float32)
    ovt = np.zeros((nblk, nc), np.float32)
    ovt[:, :n_cmp] = ov.T
    return jnp.asarray(qaug), jnp.asarray(posf), jnp.asarray(feat), jnp.asarray(ovt)


def _rope_consts(S):
    half = MLA_ROPE // 2
    inv = 1.0 / (ROPE_THETA ** (jnp.arange(0, MLA_ROPE, 2, dtype=F32) / MLA_ROPE))
    ang = jnp.arange(S, dtype=F32)[:, None] * inv[None, :]
    cos, sin = jnp.cos(ang), jnp.sin(ang)
    z = lambda w: jnp.zeros((S, w), F32)
    ct = jnp.concatenate([jnp.ones((S, MLA_NOPE), F32), cos, cos, z(LANES - MLA_QK)], axis=1)
    sa = jnp.concatenate([z(MLA_NOPE), -sin, z(LANES - MLA_NOPE - half)], axis=1)
    sb = jnp.concatenate([z(MLA_NOPE + half), sin, z(LANES - MLA_QK)], axis=1)
    return ct, sa, sb


def _mla_weights(w_uq, w_ukv):
    iq = _slab_idx(MLA_HEADS, MLA_QK, lambda h: h * MLA_QK)
    wq = (_gather_cols(w_uq, iq) * (MLA_QK ** -0.5)).astype(BF16)
    ik = _slab_idx(MLA_HEADS, MLA_NOPE, lambda h: h * (MLA_NOPE + MLA_V))
    wk = _gather_cols(w_ukv, ik).astype(BF16)
    iv = np.concatenate([h * (MLA_NOPE + MLA_V) + MLA_NOPE + np.arange(MLA_V) for h in range(MLA_HEADS)])
    wvt = w_ukv[:, iv].T.astype(BF16)
    return wq, wk, wvt


def _compress_weights(pos, w1, w2):
    G, dh, half = NSA_KV_HEADS, NSA_DH, L_CMP // 2
    eye = jnp.eye(G, dtype=F32)

    def first_layer(w):
        return jnp.einsum('lde,gk->lgdke', w, eye).reshape(half * G * dh, G * CMP_HID).astype(BF16)

    def pos_row(p):
        return jnp.broadcast_to(p[:, None, :], (half, G, dh)).reshape(1, half * G * dh)

    w2k = jnp.einsum('ed,gk->gekd', jnp.pad(w2, ((0, 0), (0, LANES - dh))), eye).reshape(G * CMP_HID, G * LANES)
    w2vt = jnp.einsum('ed,gk->gdke', w2, eye).reshape(G * dh, G * CMP_HID)
    return (pos_row(pos[:half]), pos_row(pos[half:]), first_layer(w1[:half]), first_layer(w1[half:]),
            w2k.astype(BF16), w2vt.astype(BF16))


def kernel(x, attn_norm, w_in, q_norm, kv_norm, w_uq, w_ukv, cmp_pos_k, cmp_pos_v, cmp_k_w1, cmp_k_w2, cmp_v_w1,
           cmp_v_w2, w_o, ffn_norm, w_up, conv_w, conv_b, w_down, final_norm):
    B, S, D = x.shape
    M = B * S
    assert D == D_MODEL and S % TM == 0 and S % TQ_MLA == 0 and TQ_MLA % TKV == 0 and TKV % TQ_NSA == 0
    qaug, posf, feat, ovt = _const_tables(S)
    ct, sa, sb = _rope_consts(S)
    nc = S // D_CMP
    h = x.reshape(M, D)
    for l in range(DEPTH):
        wa, wb, wq, wk, wvt = _inproj_weights(w_in[l])
        za, gt, kcr, vcr, qn, ksw, vt = _inproj(h, attn_norm[l][None, :], wa, wb, wq, wk, wvt, qaug, posf, B, S)

        mq, mk, mvt = _mla_weights(w_uq[l], w_ukv[l])
        q, k, vmt = _mla_prep(za, q_norm[l][None, :], kv_norm[l][None, :], mq, mk, mvt, ct, sa, sb, B, S)
        o_mla = _mla_attn(q.reshape(B, S, -1), k.reshape(B, S, -1), vmt, B, S)

        pak, pbk, wak, wbk, w2k, _ = _compress_weights(cmp_pos_k[l], cmp_k_w1[l], cmp_k_w2[l])
        pav, pbv, wav, wbv, _, w2vt = _compress_weights(cmp_pos_v[l], cmp_v_w1[l], cmp_v_w2[l])
        kc, vct = _compress(kcr.reshape(B, nc, -1), vcr.reshape(B, nc, -1), pak, pbk, pav, pbv,
                            wak, wbk, wav, wbv, w2k, w2vt, feat)
        o_nsa = _nsa_attn(qn, kc, vct, ksw, vt, gt, ovt, B, S)

        wo = w_o[l].astype(BF16)
        half = MLA_HEADS * MLA_V
        h = _outproj(h, o_mla.reshape(M, -1), o_nsa.reshape(M, -1), wo[:half], wo[half:])
        h = _ffn(h, ffn_norm[l][None, :], w_up[l].astype(BF16), conv_w[l], conv_b[l][None, :],
                 w_down[l].astype(BF16), final_norm[None, :], S, final=(l == DEPTH - 1))
    return h.reshape(B, S, D)
```

```python
import functools

import numpy as np
import jax
import jax.numpy as jnp
from jax import lax
from jax.experimental import pallas as pl
from jax.experimental.pallas import tpu as pltpu

F32 = jnp.float32
BF16 = jnp.bfloat16

D_MODEL = 1024
DEPTH = 2
EPS = 1e-6
NEG_INF = -1e30
LOG2E = float(np.log2(np.e))
LANES = 128

MLA_HEADS = 8
MLA_NOPE = 64
MLA_ROPE = 32
MLA_V = 64
MLA_Q_RANK = 384
MLA_KV_RANK = 256
MLA_QK = MLA_NOPE + MLA_ROPE
ROPE_THETA = 10000.0

NSA_HEADS = 8
NSA_KV_HEADS = 2
NSA_REP = NSA_HEADS // NSA_KV_HEADS
NSA_DH = 64
L_CMP = 32
D_CMP = 16
CMP_HID = 128
L_SEL = 64
N_SEL = 16
W_WIN = 512

D_FF = 2816
CONV_W = 3

_SIZES = [MLA_Q_RANK, MLA_KV_RANK, MLA_ROPE, NSA_HEADS * NSA_DH] + [NSA_KV_HEADS * NSA_DH] * 6 + [3 * NSA_HEADS]
_OFFS = [0] + [int(o) for o in np.cumsum(_SIZES)]
(OFF_CQ, OFF_CKV, OFF_KROPE, OFF_QN, OFF_KC, OFF_VC, OFF_KS, OFF_VS, OFF_KW, OFF_VW, OFF_GT, D_IN) = _OFFS

TM = 512
TKV = 256
TQ_MLA = 256
TQ_NSA = 256
VMEM_LIMIT = 56 * 1024 * 1024


def _cparams(sem):
    return pltpu.CompilerParams(dimension_semantics=sem, vmem_limit_bytes=VMEM_LIMIT)


def _dot(a, b):
    return jnp.dot(a, b, preferred_element_type=F32)


def _dot_nt(a, b):
    return lax.dot_general(a, b, (((1,), (1,)), ((), ())), preferred_element_type=F32)


def _rms(x, g):
    return x * lax.rsqrt(jnp.mean(x * x, axis=-1, keepdims=True) + EPS) * g


def _inproj_kernel(x_ref, g_ref, wa_ref, wb_ref, wq_ref, wk_ref, wvt_ref, qaug_ref, posf_ref,
                   za_ref, gt_ref, kcr_ref, vcr_ref, qn_ref, ksw_ref, vt_ref):
    n = _rms(x_ref[...], g_ref[...]).astype(BF16)
    za_ref[...] = _dot(n, wa_ref[...])
    zb = _dot(n, wb_ref[...])
    gt_ref[...] = zb[:, 0:LANES]
    kcr_ref[...] = zb[:, LANES:2 * LANES]
    vcr_ref[...] = zb[:, 2 * LANES:3 * LANES]
    qn_ref[...] = (_dot(n, wq_ref[...]) + qaug_ref[...]).astype(BF16)
    ksw_ref[...] = (_dot(n, wk_ref[...]) + posf_ref[...]).astype(BF16)
    vt = _dot_nt(wvt_ref[...], n).astype(BF16)
    for c in range(TM // TKV):
        vt_ref[0, c] = vt[:, c * TKV:(c + 1) * TKV]


def _inproj(h, g, wa, wb, wq, wk, wvt, qaug, posf, B, S):
    M = B * S
    spt = S // TM
    full = lambda a: pl.BlockSpec(a.shape, lambda i: (0,) * a.ndim)
    row = lambda w: pl.BlockSpec((TM, w), lambda i: (i, 0))
    return pl.pallas_call(
        _inproj_kernel,
        grid=(M // TM,),
        in_specs=[row(D_MODEL), full(g), full(wa), full(wb), full(wq), full(wk), full(wvt), full(qaug),
                  pl.BlockSpec((TM, 4 * LANES), lambda i: (i % spt, 0))],
        out_specs=[row(6 * LANES), row(LANES), row(LANES), row(LANES), row(8 * LANES), row(4 * LANES),
                   pl.BlockSpec((1, TM // TKV, 4 * NSA_DH, TKV), lambda i: (i // spt, i % spt, 0, 0))],
        out_shape=[jax.ShapeDtypeStruct((M, 6 * LANES), F32),
                   jax.ShapeDtypeStruct((M, LANES), F32),
                   jax.ShapeDtypeStruct((M, LANES), F32),
                   jax.ShapeDtypeStruct((M, LANES), F32),
                   jax.ShapeDtypeStruct((M, 8 * LANES), BF16),
                   jax.ShapeDtypeStruct((M, 4 * LANES), BF16),
                   jax.ShapeDtypeStruct((B, S // TKV, 4 * NSA_DH, TKV), BF16)],
        compiler_params=_cparams(("parallel",)),
        name="attn_inproj",
    )(h, g, wa, wb, wq, wk, wvt, qaug, posf)


def _rope_slab(x, ct, sa, sb):
    return x * ct + pltpu.roll(x, LANES - MLA_ROPE // 2, 1) * sa + pltpu.roll(x, MLA_ROPE // 2, 1) * sb


def _mla_prep_kernel(za_ref, qg_ref, kg_ref, wq_ref, wk_ref, wvt_ref, ct_ref, sa_ref, sb_ref,
                     q_ref, k_ref, vt_ref):
    za = za_ref[...]
    cq = _rms(za[:, 0:MLA_Q_RANK], qg_ref[...]).astype(BF16)
    ckv = _rms(za[:, MLA_Q_RANK:MLA_Q_RANK + MLA_KV_RANK], kg_ref[...]).astype(BF16)
    ct, sa, sb = ct_ref[...], sa_ref[...], sb_ref[...]
    kr = _rope_slab(za[:, 5 * LANES:6 * LANES], ct, sa, sb)
    q = _dot(cq, wq_ref[...])
    k = _dot(ckv, wk_ref[...])
    for h in range(MLA_HEADS):
        sl = slice(h * LANES, (h + 1) * LANES)
        q_ref[:, sl] = _rope_slab(q[:, sl], ct, sa, sb).astype(BF16)
        k_ref[:, sl] = (k[:, sl] + kr).astype(BF16)
    vt = _dot_nt(wvt_ref[...], ckv).astype(BF16)
    for c in range(TM // TKV):
        vt_ref[0, c] = vt[:, c * TKV:(c + 1) * TKV]


def _mla_prep(za, qg, kg, wq, wk, wvt, ct, sa, sb, B, S):
    M = B * S
    spt = S // TM
    full = lambda a: pl.BlockSpec(a.shape, lambda i: (0,) * a.ndim)
    row = lambda w: pl.BlockSpec((TM, w), lambda i: (i, 0))
    tab = pl.BlockSpec((TM, LANES), lambda i: (i % spt, 0))
    return pl.pallas_call(
        _mla_prep_kernel,
        grid=(M // TM,),
        in_specs=[row(6 * LANES), full(qg), full(kg), full(wq), full(wk), full(wvt), tab, tab, tab],
        out_specs=[row(8 * LANES), row(8 * LANES),
                   pl.BlockSpec((1, TM // TKV, MLA_HEADS * MLA_V, TKV), lambda i: (i // spt, i % spt, 0, 0))],
        out_shape=[jax.ShapeDtypeStruct((M, 8 * LANES), BF16),
                   jax.ShapeDtypeStruct((M, 8 * LANES), BF16),
                   jax.ShapeDtypeStruct((B, S // TKV, MLA_HEADS * MLA_V, TKV), BF16)],
        compiler_params=_cparams(("parallel",)),
        name="mla_prep",
    )(za, qg, kg, wq, wk, wvt, ct, sa, sb)


def _stage_scores(s, s_ref, mx_ref, h):
    mx_ref[h] = jnp.max(s, axis=0, keepdims=True)
    s_ref[h] = s


def _online_update(vt, s_ref, mx_ref, m_ref, l_ref, acc_ref, h, exp_fn):
    m_old = m_ref[h]
    m_new = jnp.maximum(m_old, mx_ref[h])
    alpha = exp_fn(m_old - m_new)
    p = exp_fn(s_ref[h] - m_new)
    l_ref[h] = alpha * l_ref[h] + jnp.sum(p, axis=0, keepdims=True)
    acc_ref[h] = alpha * acc_ref[h] + _dot(vt, p.astype(BF16))
    m_ref[h] = m_new


def _mla_attn_kernel(q_ref, k_ref, vt_ref, o_ref, m_ref, l_ref, acc_ref, s_ref, mx_ref):
    i = pl.program_id(1)
    tq = TQ_MLA
    m_ref[...] = jnp.full(m_ref.shape, -jnp.inf, F32)
    l_ref[...] = jnp.zeros(l_ref.shape, F32)
    acc_ref[...] = jnp.zeros(acc_ref.shape, F32)

    def tile(kt, masked):
        if masked:
            kpos = kt * TKV + lax.broadcasted_iota(jnp.int32, (TKV, tq), 0)
            qpos = i * tq + lax.broadcasted_iota(jnp.int32, (TKV, tq), 1)
            keep = kpos <= qpos
        for h in range(MLA_HEADS):
            sl = slice(h * LANES, (h + 1) * LANES)
            k = k_ref[0, pl.ds(pl.multiple_of(kt * TKV, TKV), TKV), sl]
            s = _dot_nt(k, q_ref[0, :, sl])
            if masked:
                s = jnp.where(keep, s, NEG_INF)
            _stage_scores(s, s_ref, mx_ref, h)
        for h in range(MLA_HEADS):
            _online_update(vt_ref[0, kt, h * MLA_V:(h + 1) * MLA_V, :], s_ref, mx_ref, m_ref, l_ref, acc_ref, h,
                           jnp.exp2)

    n_full = (i * tq) // TKV

    @pl.loop(0, n_full)
    def _(kt):
        tile(kt, False)

    for d in range(tq // TKV):
        tile(n_full + d, True)
    out = jnp.concatenate([acc_ref[h] / l_ref[h] for h in range(MLA_HEADS)], axis=0)
    o_ref[0] = out.T.astype(BF16)


def _mla_attn(q, k, vt, B, S):
    nkt = S // TKV
    hw = MLA_HEADS * LANES
    return pl.pallas_call(
        _mla_attn_kernel,
        grid=(B, S // TQ_MLA),
        in_specs=[pl.BlockSpec((1, TQ_MLA, hw), lambda b, i: (b, i, 0)),
                  pl.BlockSpec((1, S, hw), lambda b, i: (b, 0, 0)),
                  pl.BlockSpec((1, nkt, MLA_HEADS * MLA_V, TKV), lambda b, i: (b, 0, 0, 0))],
        out_specs=pl.BlockSpec((1, TQ_MLA, MLA_HEADS * MLA_V), lambda b, i: (b, i, 0)),
        out_shape=jax.ShapeDtypeStruct((B, S, MLA_HEADS * MLA_V), BF16),
        scratch_shapes=[pltpu.VMEM((MLA_HEADS, 1, TQ_MLA), F32), pltpu.VMEM((MLA_HEADS, 1, TQ_MLA), F32),
                        pltpu.VMEM((MLA_HEADS, MLA_V, TQ_MLA), F32),
                        pltpu.VMEM((MLA_HEADS, TKV, TQ_MLA), F32), pltpu.VMEM((MLA_HEADS, 1, TQ_MLA), F32)],
        compiler_params=_cparams(("parallel", "arbitrary")),
        name="mla_attn",
    )(q, k, vt)


def _gelu_tanh(x):
    return 0.5 * x * (1.0 + jnp.tanh(np.sqrt(2.0 / np.pi).astype(np.float32) * (x + 0.044715 * (x * x * x))))


def _compress_kernel(kx_ref, vx_ref, pak_ref, pbk_ref, pav_ref, pbv_ref, wak_ref, wbk_ref, wav_ref, wbv_ref,
                     w2k_ref, w2vt_ref, feat_ref, kc_ref, vct_ref):
    nc = kx_ref.shape[1]

    def hidden(x, pa, pb, wa, wb):
        ya = _dot((x + pa).astype(BF16), wa)
        yb = _dot((x + pb).astype(BF16), wb)
        return _gelu_tanh(ya + pltpu.roll(yb, nc - 1, 0)).astype(BF16)

    hk = hidden(kx_ref[0], pak_ref[...], pbk_ref[...], wak_ref[...], wbk_ref[...])
    hv = hidden(vx_ref[0], pav_ref[...], pbv_ref[...], wav_ref[...], wbv_ref[...])
    kc_ref[0] = (_dot(hk, w2k_ref[...]) + feat_ref[...]).astype(BF16)
    vct_ref[0] = _dot_nt(w2vt_ref[...], hv).astype(BF16)


def _compress(kx, vx, pak, pbk, pav, pbv, wak, wbk, wav, wbv, w2k, w2vt, feat):
    B, nc, width = kx.shape
    full = lambda a: pl.BlockSpec(a.shape, lambda b: (0,) * a.ndim)
    xs = pl.BlockSpec((1, nc, width), lambda b: (b, 0, 0))
    return pl.pallas_call(
        _compress_kernel,
        grid=(B,),
        in_specs=[xs, xs] + [full(a) for a in (pak, pbk, pav, pbv, wak, wbk, wav, wbv, w2k, w2vt, feat)],
        out_specs=[pl.BlockSpec((1, nc, 2 * LANES), lambda b: (b, 0, 0)),
                   pl.BlockSpec((1, 2 * NSA_DH, nc), lambda b: (b, 0, 0))],
        out_shape=[jax.ShapeDtypeStruct((B, nc, 2 * LANES), BF16),
                   jax.ShapeDtypeStruct((B, 2 * NSA_DH, nc), BF16)],
        compiler_params=_cparams(("parallel",)),
        name="nsa_compress",
    )(kx, vx, pak, pbk, pav, pbv, wak, wbk, wav, wbv, w2k, w2vt, feat)


def _nsa_attn_kernel(qn_ref, kc_ref, vct_ref, ks_ref, kw_ref, vst_ref, vwt_ref, gt_ref, ovt_ref,
                     o_ref, mb_ref, ocmp_ref, m_ref, l_ref, acc_ref, mw_ref, lw_ref, accw_ref, s_ref, mx_ref):
    i = pl.program_id(1)
    tq = TQ_NSA
    G, R, dh = NSA_KV_HEADS, NSA_REP, NSA_DH
    nc = kc_ref.shape[1]
    nblk = ovt_ref.shape[0]
    t0 = i * tq
    tpos = t0 + lax.broadcasted_iota(jnp.int32, (1, tq), 1)
    q_of = lambda h: qn_ref[0, :, h * LANES:(h + 1) * LANES]

    for ref in (m_ref, mw_ref):
        ref[...] = jnp.full(ref.shape, -jnp.inf, F32)
    for ref in (l_ref, lw_ref, acc_ref, accw_ref):
        ref[...] = jnp.zeros(ref.shape, F32)

    nrow = lax.broadcasted_iota(jnp.int32, (nc, 1), 0)
    mask_c = (tpos >= nrow * D_CMP + (L_CMP - 1)) & (nrow < nc - 1)
    cur = lax.shift_right_logical(tpos, 6)
    jrow = lax.broadcasted_iota(jnp.int32, (nblk, 1), 0)
    for g in range(G):
        kc = kc_ref[0, :, g * LANES:(g + 1) * LANES]
        psum = None
        for r in range(R):
            h = g * R + r
            scm = jnp.where(mask_c, _dot_nt(kc, q_of(h)), NEG_INF)
            ec = jnp.exp(scm - jnp.max(scm, axis=0, keepdims=True))
            pc = jnp.where(mask_c, ec / jnp.sum(ec, axis=0, keepdims=True), 0.0)
            ocmp_ref[h] = _dot(vct_ref[0, g * dh:(g + 1) * dh, :], pc.astype(BF16))
            psum = pc if psum is None else psum + pc
        imp = jnp.dot(ovt_ref[...], psum, preferred_element_type=F32, precision=lax.Precision.HIGHEST)
        imp = jnp.where(jrow > cur, -jnp.inf, imp)
        imp = jnp.where((jrow == 0) | (jrow == cur) | (jrow == cur - 1), jnp.inf, imp)
        rank = jnp.zeros((nblk, tq), F32)
        for jj in range(nblk):
            row = imp[jj:jj + 1, :]
            beats = (row > imp) | ((row == imp) & (jrow > jj))
            rank = rank + jnp.where(beats, 1.0, 0.0)
        mb_ref[g] = jnp.where(rank < float(min(N_SEL, nblk)), 0.0, NEG_INF)

    bpt = TKV // L_SEL

    def sel_tile(kt, masked):
        if masked:
            keep = kt * TKV + lax.broadcasted_iota(jnp.int32, (TKV, tq), 0) <= tpos
        for g in range(G):
            k = ks_ref[0, pl.ds(pl.multiple_of(kt * TKV, TKV), TKV), g * LANES:(g + 1) * LANES]
            bias = jnp.concatenate(
                [jnp.broadcast_to(mb_ref[g, pl.ds(kt * bpt + c, 1), :], (L_SEL, tq)) for c in range(bpt)], axis=0)
            if masked:
                bias = jnp.where(keep, bias, NEG_INF)
            for r in range(R):
                _stage_scores(_dot_nt(k, q_of(g * R + r)) + bias, s_ref, mx_ref, g * R + r)
        for h in range(G * R):
            g = h // R
            _online_update(vst_ref[0, kt, g * dh:(g + 1) * dh, :], s_ref, mx_ref, m_ref, l_ref, acc_ref, h, jnp.exp)

    @pl.loop(0, i)
    def _(kt):
        sel_tile(kt, False)

    sel_tile(i, True)

    @pl.loop(jnp.maximum(i - W_WIN // TKV, 0), i + 1)
    def _(kt):
        dist = tpos - (kt * TKV + lax.broadcasted_iota(jnp.int32, (TKV, tq), 0))
        bias = jnp.where((dist >= 0) & (dist < W_WIN), 0.0, NEG_INF)
        for g in range(G):
            k = kw_ref[0, pl.ds(pl.multiple_of(kt * TKV, TKV), TKV), g * LANES:(g + 1) * LANES]
            for r in range(R):
                _stage_scores(_dot_nt(k, q_of(g * R + r)) + bias, s_ref, mx_ref, g * R + r)
        for h in range(G * R):
            g = h // R
            _online_update(vwt_ref[0, kt, g * dh:(g + 1) * dh, :], s_ref, mx_ref, mw_ref, lw_ref, accw_ref, h, jnp.exp)

    sig = 1.0 / (1.0 + jnp.exp(-gt_ref[0].T[0:G * 16, :]))
    outs = []
    for h in range(G * R):
        row = (h // R) * 16 + (h % R) * 4
        outs.append(sig[row:row + 1, :] * ocmp_ref[h] + sig[row + 1:row + 2, :] * (acc_ref[h] / l_ref[h])
                    + sig[row + 2:row + 3, :] * (accw_ref[h] / lw_ref[h]))
    o_ref[0] = jnp.concatenate(outs, axis=0).T.astype(BF16)


def _nsa_attn(qn, kc, vct, ksw, vt, gt, ovt, B, S):
    nkt = S // TKV
    nc = kc.shape[1]
    G = NSA_KV_HEADS
    qn3 = qn.reshape(B, S, NSA_HEADS * LANES)
    ksw3 = ksw.reshape(B, S, 4 * LANES)
    gt3 = gt.reshape(B, S, LANES)
    H, dh, tq = NSA_HEADS, NSA_DH, TQ_NSA
    state = [pltpu.VMEM((H, 1, tq), F32), pltpu.VMEM((H, 1, tq), F32), pltpu.VMEM((H, dh, tq), F32)]
    return pl.pallas_call(
        _nsa_attn_kernel,
        grid=(B, S // tq),
        in_specs=[pl.BlockSpec((1, tq, H * LANES), lambda b, i: (b, i, 0)),
                  pl.BlockSpec((1, nc, G * LANES), lambda b, i: (b, 0, 0)),
                  pl.BlockSpec((1, G * dh, nc), lambda b, i: (b, 0, 0)),
                  pl.BlockSpec((1, S, G * LANES), lambda b, i: (b, 0, 0)),
                  pl.BlockSpec((1, S, G * LANES), lambda b, i: (b, 0, 1)),
                  pl.BlockSpec((1, nkt, G * dh, TKV), lambda b, i: (b, 0, 0, 0)),
                  pl.BlockSpec((1, nkt, G * dh, TKV), lambda b, i: (b, 0, 1, 0)),
                  pl.BlockSpec((1, tq, LANES), lambda b, i: (b, i, 0)),
                  pl.BlockSpec(ovt.shape, lambda b, i: (0, 0))],
        out_specs=pl.BlockSpec((1, tq, H * dh), lambda b, i: (b, i, 0)),
        out_shape=jax.ShapeDtypeStruct((B, S, H * dh), BF16),
        scratch_shapes=([pltpu.VMEM((G, ovt.shape[0], tq), F32), pltpu.VMEM((H, dh, tq), F32)] + state + state
                        + [pltpu.VMEM((H, TKV, tq), F32), pltpu.VMEM((H, 1, tq), F32)]),
        compiler_params=_cparams(("parallel", "arbitrary")),
        name="nsa_attn",
    )(qn3, kc, vct, ksw3, ksw3, vt, vt, gt3, ovt)


def _outproj_kernel(h_ref, a_ref, b_ref, wa_ref, wb_ref, o_ref):
    o_ref[...] = h_ref[...] + _dot(a_ref[...], wa_ref[...]) + _dot(b_ref[...], wb_ref[...])


def _outproj(h, oa, ob, wa, wb):
    M = h.shape[0]
    full = lambda a: pl.BlockSpec(a.shape, lambda i: (0,) * a.ndim)
    row = lambda w: pl.BlockSpec((TM, w), lambda i: (i, 0))
    return pl.pallas_call(
        _outproj_kernel,
        grid=(M // TM,),
        in_specs=[row(D_MODEL), row(oa.shape[1]), row(ob.shape[1]), full(wa), full(wb)],
        out_specs=row(D_MODEL),
        out_shape=jax.ShapeDtypeStruct((M, D_MODEL), F32),
        compiler_params=_cparams(("parallel",)),
        name="attn_outproj",
    )(h, oa, ob, wa, wb)


FF_CHUNK = 256
HALO = 8


def _ffn_kernel(x_ref, xp_ref, g_ref, wup_ref, cw_ref, cb_ref, wdn_ref, fg_ref, o_ref, *, final, spt):
    i = pl.program_id(0)
    x = x_ref[...]
    xe = jnp.concatenate([xp_ref[...], x], axis=0)
    n = _rms(xe, g_ref[...]).astype(BF16)
    keep = jnp.where(i % spt == 0, 0.0, 1.0)
    rows = lax.broadcasted_iota(jnp.int32, (HALO + TM, 1), 0)
    halo_scale = jnp.where(rows < HALO, keep, 1.0)
    acc = x

    def conv(hx, col):
        hx = hx * halo_scale
        cw = cw_ref[:, col:col + FF_CHUNK]
        out = cb_ref[:, col:col + FF_CHUNK]
        for j in range(CONV_W):
            lo = HALO - (CONV_W - 1) + j
            out = out + hx[lo:lo + TM, :] * cw[j:j + 1, :]
        return out

    for c in range(D_FF // FF_CHUNK):
        col = c * FF_CHUNK
        gate = conv(_dot(n, wup_ref[:, col:col + FF_CHUNK]), col)
        up = conv(_dot(n, wup_ref[:, D_FF + col:D_FF + col + FF_CHUNK]), D_FF + col)
        act = (gate * (1.0 / (1.0 + jnp.exp(-gate))) * up).astype(BF16)
        acc = acc + _dot(act, wdn_ref[col:col + FF_CHUNK, :])
    if final:
        acc = _rms(acc, fg_ref[...])
    o_ref[...] = acc


def _ffn(h, g, wup, cw, cb, wdn, fg, S, final):
    M = h.shape[0]
    spt = S // TM
    full = lambda a: pl.BlockSpec(a.shape, lambda i: (0,) * a.ndim)
    row = pl.BlockSpec((TM, D_MODEL), lambda i: (i, 0))
    prev = pl.BlockSpec((HALO, D_MODEL), lambda i: (jnp.maximum(i * (TM // HALO) - 1, 0), 0))
    return pl.pallas_call(
        functools.partial(_ffn_kernel, final=final, spt=spt),
        grid=(M // TM,),
        in_specs=[row, prev, full(g), full(wup), full(cw), full(cb), full(wdn), full(fg)],
        out_specs=row,
        out_shape=jax.ShapeDtypeStruct((M, D_MODEL), F32),
        compiler_params=_cparams(("parallel",)),
        name="conv_ffn",
    )(h, h, g, wup, cw, cb, wdn, fg)


def _gather_cols(w, idx):
    idx = np.asarray(idx)
    return jnp.where(jnp.asarray(idx >= 0)[None, :], w[:, np.maximum(idx, 0)], 0.0)


def _slab_idx(n_slabs, width, base_fn):
    idx = -np.ones((n_slabs, LANES), np.int64)
    for s in range(n_slabs):
        idx[s, :width] = base_fn(s) + np.arange(width)
    return idx.reshape(-1)


def _inproj_weights(w_in):
    G, dh = NSA_KV_HEADS, NSA_DH
    ia = -np.ones(6 * LANES, np.int64)
    ia[0:MLA_Q_RANK] = OFF_CQ + np.arange(MLA_Q_RANK)
    ia[MLA_Q_RANK:MLA_Q_RANK + MLA_KV_RANK] = OFF_CKV + np.arange(MLA_KV_RANK)
    ia[5 * LANES + MLA_NOPE:5 * LANES + MLA_NOPE + MLA_ROPE] = OFF_KROPE + np.arange(MLA_ROPE)
    igt = -np.ones(LANES, np.int64)
    for g in range(G):
        for r in range(NSA_REP):
            for c in range(3):
                igt[g * 16 + r * 4 + c] = OFF_GT + (g * NSA_REP + r) * 3 + c
    ib = np.concatenate([igt, OFF_KC + np.arange(G * dh), OFF_VC + np.arange(G * dh)])
    iq = _slab_idx(NSA_HEADS, dh, lambda h: OFF_QN + h * dh)
    ik = np.concatenate([_slab_idx(G, dh, lambda g: OFF_KS + g * dh), _slab_idx(G, dh, lambda g: OFF_KW + g * dh)])
    iv = np.concatenate([OFF_VS + np.arange(G * dh), OFF_VW + np.arange(G * dh)])
    wa = _gather_cols(w_in, ia).astype(BF16)
    wb = _gather_cols(w_in, ib).astype(BF16)
    wq = (_gather_cols(w_in, iq) * (NSA_DH ** -0.5)).astype(BF16)
    wk = _gather_cols(w_in, ik).astype(BF16)
    wvt = w_in[:, iv].T.astype(BF16)
    return wa, wb, wq, wk, wvt


def _const_tables(S):
    slopes = 2.0 ** (-8.0 * np.arange(1, NSA_HEADS + 1) / NSA_HEADS)
    qaug = np.zeros((1, NSA_HEADS * LANES), np.float32)
    for h in range(NSA_HEADS):
        qaug[0, h * LANES + NSA_DH] = slopes[h] * L_SEL
        qaug[0, h * LANES + NSA_DH + 1] = slopes[h]
    t = np.arange(S)
    posf = np.zeros((S, 4 * LANES), np.float32)
    for s in range(4):
        posf[:, s * LANES + NSA_DH] = t // L_SEL
        posf[:, s * LANES + NSA_DH + 1] = t % L_SEL
    nc = S // D_CMP
    bend = np.arange(nc) * D_CMP + L_CMP - 1
    feat = np.zeros((nc, 2 * LANES), np.float32)
    for s in range(2):
        feat[:, s * LANES + NSA_DH] = bend // L_SEL
        feat[:, s * LANES + NSA_DH + 1] = bend % L_SEL
    n_cmp = (S - L_CMP) // D_CMP + 1
    nblk = S // L_SEL
    cs = np.arange(n_cmp) * D_CMP
    ss = np.arange(nblk) * L_SEL
    ov = ((cs[:, None] < ss[None, :] + L_SEL) & (cs[:, None] + L_CMP > ss[None, :])).astype(np.float32)
    ovt = np.zeros((nblk, nc), np.float32)
    ovt[:, :n_cmp] = ov.T
    return jnp.asarray(qaug), jnp.asarray(posf), jnp.asarray(feat), jnp.asarray(ovt)


def _rope_consts(S):
    half = MLA_ROPE // 2
    inv = 1.0 / (ROPE_THETA ** (jnp.arange(0, MLA_ROPE, 2, dtype=F32) / MLA_ROPE))
    ang = jnp.arange(S, dtype=F32)[:, None] * inv[None, :]
    cos, sin = jnp.cos(ang), jnp.sin(ang)
    z = lambda w: jnp.zeros((S, w), F32)
    ct = jnp.concatenate([jnp.ones((S, MLA_NOPE), F32), cos, cos, z(LANES - MLA_QK)], axis=1)
    sa = jnp.concatenate([z(MLA_NOPE), -sin, z(LANES - MLA_NOPE - half)], axis=1)
    sb = jnp.concatenate([z(MLA_NOPE + half), sin, z(LANES - MLA_QK)], axis=1)
    return ct, sa, sb


def _mla_weights(w_uq, w_ukv):
    iq = _slab_idx(MLA_HEADS, MLA_QK, lambda h: h * MLA_QK)
    wq = (_gather_cols(w_uq, iq) * (MLA_QK ** -0.5 * LOG2E)).astype(BF16)
    ik = _slab_idx(MLA_HEADS, MLA_NOPE, lambda h: h * (MLA_NOPE + MLA_V))
    wk = _gather_cols(w_ukv, ik).astype(BF16)
    iv = np.concatenate([h * (MLA_NOPE + MLA_V) + MLA_NOPE + np.arange(MLA_V) for h in range(MLA_HEADS)])
    wvt = w_ukv[:, iv].T.astype(BF16)
    return wq, wk, wvt


def _compress_weights(pos, w1, w2):
    G, dh, half = NSA_KV_HEADS, NSA_DH, L_CMP // 2
    eye = jnp.eye(G, dtype=F32)

    def first_layer(w):
        return jnp.einsum('lde,gk->lgdke', w, eye).reshape(half * G * dh, G * CMP_HID).astype(BF16)

    def pos_row(p):
        return jnp.broadcast_to(p[:, None, :], (half, G, dh)).reshape(1, half * G * dh)

    w2k = jnp.einsum('ed,gk->gekd', jnp.pad(w2, ((0, 0), (0, LANES - dh))), eye).reshape(G * CMP_HID, G * LANES)
    w2vt = jnp.einsum('ed,gk->gdke', w2, eye).reshape(G * dh, G * CMP_HID)
    return (pos_row(pos[:half]), pos_row(pos[half:]), first_layer(w1[:half]), first_layer(w1[half:]),
            w2k.astype(BF16), w2vt.astype(BF16))


def kernel(x, attn_norm, w_in, q_norm, kv_norm, w_uq, w_ukv, cmp_pos_k, cmp_pos_v, cmp_k_w1, cmp_k_w2, cmp_v_w1,
           cmp_v_w2, w_o, ffn_norm, w_up, conv_w, conv_b, w_down, final_norm):
    B, S, D = x.shape
    M = B * S
    assert D == D_MODEL and S % TM == 0 and S % TQ_MLA == 0 and TQ_MLA % TKV == 0 and TQ_NSA == TKV
    qaug, posf, feat, ovt = _const_tables(S)
    ct, sa, sb = _rope_consts(S)
    nc = S // D_CMP
    h = x.reshape(M, D)
    for l in range(DEPTH):
        wa, wb, wq, wk, wvt = _inproj_weights(w_in[l])
        za, gt, kcr, vcr, qn, ksw, vt = _inproj(h, attn_norm[l][None, :], wa, wb, wq, wk, wvt, qaug, posf, B, S)

        mq, mk, mvt = _mla_weights(w_uq[l], w_ukv[l])
        q, k, vmt = _mla_prep(za, q_norm[l][None, :], kv_norm[l][None, :], mq, mk, mvt, ct, sa, sb, B, S)
        o_mla = _mla_attn(q.reshape(B, S, -1), k.reshape(B, S, -1), vmt, B, S)

        pak, pbk, wak, wbk, w2k, _ = _compress_weights(cmp_pos_k[l], cmp_k_w1[l], cmp_k_w2[l])
        pav, pbv, wav, wbv, _, w2vt = _compress_weights(cmp_pos_v[l], cmp_v_w1[l], cmp_v_w2[l])
        kc, vct = _compress(kcr.reshape(B, nc, -1), vcr.reshape(B, nc, -1), pak, pbk, pav, pbv,
                            wak, wbk, wav, wbv, w2k, w2vt, feat)
        o_nsa = _nsa_attn(qn, kc, vct, ksw, vt, gt, ovt, B, S)

        wo = w_o[l].astype(BF16)
        half = MLA_HEADS * MLA_V
        h = _outproj(h, o_mla.reshape(M, -1), o_nsa.reshape(M, -1), wo[:half], wo[half:])
        h = _ffn(h, ffn_norm[l][None, :], w_up[l].astype(BF16), conv_w[l], conv_b[l][None, :],
                 w_down[l].astype(BF16), final_norm[None, :], S, final=(l == DEPTH - 1))
    return h.reshape(B, S, D)
```

```python
import functools

import numpy as np
import jax
import jax.numpy as jnp
from jax import lax
from jax.experimental import pallas as pl
from jax.experimental.pallas import tpu as pltpu

F32 = jnp.float32
BF16 = jnp.bfloat16

D_MODEL = 1024
DEPTH = 2
EPS = 1e-6
NEG_INF = -1e30
LOG2E = float(np.log2(np.e))
LANES = 128

MLA_HEADS = 8
MLA_NOPE = 64
MLA_ROPE = 32
MLA_V = 64
MLA_Q_RANK = 384
MLA_KV_RANK = 256
MLA_QK = MLA_NOPE + MLA_ROPE
ROPE_THETA = 10000.0

NSA_HEADS = 8
NSA_KV_HEADS = 2
NSA_REP = NSA_HEADS // NSA_KV_HEADS
NSA_DH = 64
L_CMP = 32
D_CMP = 16
CMP_HID = 128
L_SEL = 64
N_SEL = 16
W_WIN = 512

D_FF = 2816
CONV_W = 3

_SIZES = [MLA_Q_RANK, MLA_KV_RANK, MLA_ROPE, NSA_HEADS * NSA_DH] + [NSA_KV_HEADS * NSA_DH] * 6 + [3 * NSA_HEADS]
_OFFS = [0] + [int(o) for o in np.cumsum(_SIZES)]
(OFF_CQ, OFF_CKV, OFF_KROPE, OFF_QN, OFF_KC, OFF_VC, OFF_KS, OFF_VS, OFF_KW, OFF_VW, OFF_GT, D_IN) = _OFFS

TM = 512
TKV = 256
TQ_MLA = 256
TQ_NSA = 256
VMEM_LIMIT = 56 * 1024 * 1024


def _cparams(sem):
    return pltpu.CompilerParams(dimension_semantics=sem, vmem_limit_bytes=VMEM_LIMIT)


def _dot(a, b):
    return jnp.dot(a, b, preferred_element_type=F32)


def _dot_nt(a, b):
    return lax.dot_general(a, b, (((1,), (1,)), ((), ())), preferred_element_type=F32)


def _rms(x, g):
    return x * lax.rsqrt(jnp.mean(x * x, axis=-1, keepdims=True) + EPS) * g


def _inproj_kernel(x_ref, g_ref, wa_ref, wb_ref, wq_ref, wk_ref, wvt_ref, qaug_ref, posf_ref,
                   za_ref, gt_ref, kcr_ref, vcr_ref, qn_ref, ksw_ref, vt_ref):
    n = _rms(x_ref[...], g_ref[...]).astype(BF16)
    za_ref[...] = _dot(n, wa_ref[...])
    zb = _dot(n, wb_ref[...])
    gt_ref[...] = zb[:, 0:LANES]
    kcr_ref[...] = zb[:, LANES:2 * LANES]
    vcr_ref[...] = zb[:, 2 * LANES:3 * LANES]
    qn_ref[...] = (_dot(n, wq_ref[...]) + qaug_ref[...]).astype(BF16)
    ksw_ref[...] = (_dot(n, wk_ref[...]) + posf_ref[...]).astype(BF16)
    vt = _dot_nt(wvt_ref[...], n).astype(BF16)
    for c in range(TM // TKV):
        vt_ref[0, c] = vt[:, c * TKV:(c + 1) * TKV]


def _inproj(h, g, wa, wb, wq, wk, wvt, qaug, posf, B, S):
    M = B * S
    spt = S // TM
    full = lambda a: pl.BlockSpec(a.shape, lambda i: (0,) * a.ndim)
    row = lambda w: pl.BlockSpec((TM, w), lambda i: (i, 0))
    return pl.pallas_call(
        _inproj_kernel,
        grid=(M // TM,),
        in_specs=[row(D_MODEL), full(g), full(wa), full(wb), full(wq), full(wk), full(wvt), full(qaug),
                  pl.BlockSpec((TM, 4 * LANES), lambda i: (i % spt, 0))],
        out_specs=[row(6 * LANES), row(LANES), row(LANES), row(LANES), row(8 * LANES), row(4 * LANES),
                   pl.BlockSpec((1, TM // TKV, 4 * NSA_DH, TKV), lambda i: (i // spt, i % spt, 0, 0))],
        out_shape=[jax.ShapeDtypeStruct((M, 6 * LANES), F32),
                   jax.ShapeDtypeStruct((M, LANES), F32),
                   jax.ShapeDtypeStruct((M, LANES), F32),
                   jax.ShapeDtypeStruct((M, LANES), F32),
                   jax.ShapeDtypeStruct((M, 8 * LANES), BF16),
                   jax.ShapeDtypeStruct((M, 4 * LANES), BF16),
                   jax.ShapeDtypeStruct((B, S // TKV, 4 * NSA_DH, TKV), BF16)],
        compiler_params=_cparams(("parallel",)),
        name="attn_inproj",
    )(h, g, wa, wb, wq, wk, wvt, qaug, posf)


def _rope_slab(x, ct, sa, sb):
    return x * ct + pltpu.roll(x, LANES - MLA_ROPE // 2, 1) * sa + pltpu.roll(x, MLA_ROPE // 2, 1) * sb


def _mla_prep_kernel(za_ref, qg_ref, kg_ref, wq_ref, wk_ref, wvt_ref, ct_ref, sa_ref, sb_ref,
                     q_ref, k_ref, vt_ref):
    za = za_ref[...]
    cq = _rms(za[:, 0:MLA_Q_RANK], qg_ref[...]).astype(BF16)
    ckv = _rms(za[:, MLA_Q_RANK:MLA_Q_RANK + MLA_KV_RANK], kg_ref[...]).astype(BF16)
    ct, sa, sb = ct_ref[...], sa_ref[...], sb_ref[...]
    kr = _rope_slab(za[:, 5 * LANES:6 * LANES], ct, sa, sb)
    q = _dot(cq, wq_ref[...])
    k = _dot(ckv, wk_ref[...])
    for h in range(MLA_HEADS):
        sl = slice(h * LANES, (h + 1) * LANES)
        q_ref[:, sl] = _rope_slab(q[:, sl], ct, sa, sb).astype(BF16)
        k_ref[:, sl] = (k[:, sl] + kr).astype(BF16)
    vt = _dot_nt(wvt_ref[...], ckv).astype(BF16)
    for c in range(TM // TKV):
        vt_ref[0, c] = vt[:, c * TKV:(c + 1) * TKV]


def _mla_prep(za, qg, kg, wq, wk, wvt, ct, sa, sb, B, S):
    M = B * S
    spt = S // TM
    full = lambda a: pl.BlockSpec(a.shape, lambda i: (0,) * a.ndim)
    row = lambda w: pl.BlockSpec((TM, w), lambda i: (i, 0))
    tab = pl.BlockSpec((TM, LANES), lambda i: (i % spt, 0))
    return pl.pallas_call(
        _mla_prep_kernel,
        grid=(M // TM,),
        in_specs=[row(6 * LANES), full(qg), full(kg), full(wq), full(wk), full(wvt), tab, tab, tab],
        out_specs=[row(8 * LANES), row(8 * LANES),
                   pl.BlockSpec((1, TM // TKV, MLA_HEADS * MLA_V, TKV), lambda i: (i // spt, i % spt, 0, 0))],
        out_shape=[jax.ShapeDtypeStruct((M, 8 * LANES), BF16),
                   jax.ShapeDtypeStruct((M, 8 * LANES), BF16),
                   jax.ShapeDtypeStruct((B, S // TKV, MLA_HEADS * MLA_V, TKV), BF16)],
        compiler_params=_cparams(("parallel",)),
        name="mla_prep",
    )(za, qg, kg, wq, wk, wvt, ct, sa, sb)


def _stage_scores(s, s_ref, mx_ref, b, h):
    mx_ref[b, h] = jnp.max(s, axis=0, keepdims=True)
    s_ref[b, h, 0:s.shape[0], :] = s


ONES_ROWS = 16


def _online_update(vt, s_ref, mx_ref, b, m_ref, acc_ref, h):
    m_old = m_ref[h]
    m_new = jnp.maximum(m_old, mx_ref[b, h])
    p = jnp.exp2(s_ref[b, h] - m_new).astype(BF16)
    vt1 = jnp.concatenate([vt, jnp.ones((ONES_ROWS, vt.shape[1]), BF16)], axis=0)
    acc_ref[h] = jnp.exp2(m_old - m_new) * acc_ref[h] + _dot(vt1, p)
    m_ref[h] = m_new


def _normalised(acc_ref, h, dh):
    return acc_ref[h, 0:dh, :] / acc_ref[h, dh:dh + 1, :]


def _pipelined_sweep(i, score, update):
    @pl.when(i >= 1)
    def _():
        score(0, 0)

    @pl.loop(0, i // 2)
    def _(j):
        score(2 * j + 1, 1)
        update(2 * j, 0)
        score(2 * j + 2, 0)
        update(2 * j + 1, 1)

    @pl.when(i % 2 == 1)
    def _():
        update(i - 1, 0)


def _mla_attn_kernel(q_ref, k_ref, vt_ref, o_ref, m_ref, acc_ref, s_ref, mx_ref):
    i = pl.program_id(1)
    tq = TQ_MLA
    m_ref[...] = jnp.full(m_ref.shape, -jnp.inf, F32)
    acc_ref[...] = jnp.zeros(acc_ref.shape, F32)

    def score(kt, b, keep=None):
        for h in range(MLA_HEADS):
            sl = slice(h * LANES, (h + 1) * LANES)
            k = k_ref[0, pl.ds(pl.multiple_of(kt * TKV, TKV), TKV), sl]
            s = _dot_nt(k, q_ref[0, :, sl])
            if keep is not None:
                s = jnp.where(keep, s, NEG_INF)
            _stage_scores(s, s_ref, mx_ref, b, h)

    def update(kt, b):
        for h in range(MLA_HEADS):
            _online_update(vt_ref[0, kt, h * MLA_V:(h + 1) * MLA_V, :], s_ref, mx_ref, b, m_ref, acc_ref, h)

    _pipelined_sweep(i, score, update)
    score(i, 0, lax.broadcasted_iota(jnp.int32, (TKV, tq), 0) <= lax.broadcasted_iota(jnp.int32, (TKV, tq), 1))
    update(i, 0)
    out = jnp.concatenate([_normalised(acc_ref, h, MLA_V) for h in range(MLA_HEADS)], axis=0)
    o_ref[0] = out.T.astype(BF16)


def _mla_attn(q, k, vt, B, S):
    nkt = S // TKV
    hw = MLA_HEADS * LANES
    return pl.pallas_call(
        _mla_attn_kernel,
        grid=(B, S // TQ_MLA),
        in_specs=[pl.BlockSpec((1, TQ_MLA, hw), lambda b, i: (b, i, 0)),
                  pl.BlockSpec((1, S, hw), lambda b, i: (b, 0, 0)),
                  pl.BlockSpec((1, nkt, MLA_HEADS * MLA_V, TKV), lambda b, i: (b, 0, 0, 0))],
        out_specs=pl.BlockSpec((1, TQ_MLA, MLA_HEADS * MLA_V), lambda b, i: (b, i, 0)),
        out_shape=jax.ShapeDtypeStruct((B, S, MLA_HEADS * MLA_V), BF16),
        scratch_shapes=[pltpu.VMEM((MLA_HEADS, 1, TQ_MLA), F32),
                        pltpu.VMEM((MLA_HEADS, MLA_V + ONES_ROWS, TQ_MLA), F32),
                        pltpu.VMEM((2, MLA_HEADS, TKV, TQ_MLA), F32), pltpu.VMEM((2, MLA_HEADS, 1, TQ_MLA), F32)],
        compiler_params=_cparams(("parallel", "arbitrary")),
        name="mla_attn",
    )(q, k, vt)


def _gelu_tanh(x):
    return 0.5 * x * (1.0 + jnp.tanh(np.sqrt(2.0 / np.pi).astype(np.float32) * (x + 0.044715 * (x * x * x))))


def _compress_kernel(kx_ref, vx_ref, pak_ref, pbk_ref, pav_ref, pbv_ref, wak_ref, wbk_ref, wav_ref, wbv_ref,
                     w2k_ref, w2vt_ref, feat_ref, kc_ref, vct_ref):
    nc = kx_ref.shape[1]

    def hidden(x, pa, pb, wa, wb):
        ya = _dot((x + pa).astype(BF16), wa)
        yb = _dot((x + pb).astype(BF16), wb)
        return _gelu_tanh(ya + pltpu.roll(yb, nc - 1, 0)).astype(BF16)

    hk = hidden(kx_ref[0], pak_ref[...], pbk_ref[...], wak_ref[...], wbk_ref[...])
    hv = hidden(vx_ref[0], pav_ref[...], pbv_ref[...], wav_ref[...], wbv_ref[...])
    kc_ref[0] = (_dot(hk, w2k_ref[...]) + feat_ref[...]).astype(BF16)
    vct_ref[0] = _dot_nt(w2vt_ref[...], hv).astype(BF16)


def _compress(kx, vx, pak, pbk, pav, pbv, wak, wbk, wav, wbv, w2k, w2vt, feat):
    B, nc, width = kx.shape
    full = lambda a: pl.BlockSpec(a.shape, lambda b: (0,) * a.ndim)
    xs = pl.BlockSpec((1, nc, width), lambda b: (b, 0, 0))
    return pl.pallas_call(
        _compress_kernel,
        grid=(B,),
        in_specs=[xs, xs] + [full(a) for a in (pak, pbk, pav, pbv, wak, wbk, wav, wbv, w2k, w2vt, feat)],
        out_specs=[pl.BlockSpec((1, nc, 2 * LANES), lambda b: (b, 0, 0)),
                   pl.BlockSpec((1, 2 * NSA_DH, nc), lambda b: (b, 0, 0))],
        out_shape=[jax.ShapeDtypeStruct((B, nc, 2 * LANES), BF16),
                   jax.ShapeDtypeStruct((B, 2 * NSA_DH, nc), BF16)],
        compiler_params=_cparams(("parallel",)),
        name="nsa_compress",
    )(kx, vx, pak, pbk, pav, pbv, wak, wbk, wav, wbv, w2k, w2vt, feat)


def _nsa_attn_kernel(qn_ref, kc_ref, vct_ref, ks_ref, kw_ref, vst_ref, vwt_ref, gt_ref, ovt_ref, oh_ref,
                     o_ref, ocmp_ref, m_ref, acc_ref, mw_ref, accw_ref, s_ref, mx_ref, qs_ref):
    i = pl.program_id(1)
    tq = TQ_NSA
    G, R, dh = NSA_KV_HEADS, NSA_REP, NSA_DH
    nc = kc_ref.shape[1]
    nblk = ovt_ref.shape[0]
    t0 = i * tq
    tpos = t0 + lax.broadcasted_iota(jnp.int32, (1, tq), 1)
    q_of = lambda h: qn_ref[0, :, h * LANES:(h + 1) * LANES]

    for ref in (m_ref, mw_ref):
        ref[...] = jnp.full(ref.shape, -jnp.inf, F32)
    for ref in (acc_ref, accw_ref):
        ref[...] = jnp.zeros(ref.shape, F32)

    nrow = lax.broadcasted_iota(jnp.int32, (nc, 1), 0)
    bias_c = jnp.where((tpos >= nrow * D_CMP + (L_CMP - 1)) & (nrow < nc - 1), 0.0, NEG_INF)
    has_cmp = tpos >= L_CMP - 1
    for h in range(G * R):
        g = h // R
        _stage_scores(_dot_nt(kc_ref[0, :, g * LANES:(g + 1) * LANES], q_of(h)) + bias_c, s_ref, mx_ref, 0, h)
    imps = [None] * G
    for h in range(G * R):
        g = h // R
        e = jnp.exp2(s_ref[0, h, 0:nc, :] - mx_ref[0, h])
        inv = jnp.where(has_cmp, 1.0 / jnp.sum(e, axis=0, keepdims=True), 0.0)
        ocmp_ref[h] = _dot(vct_ref[0, g * dh:(g + 1) * dh, :], e.astype(BF16)) * inv
        contrib = jnp.dot(ovt_ref[...], e, preferred_element_type=F32, precision=lax.Precision.HIGHEST) * inv
        imps[g] = contrib if imps[g] is None else imps[g] + contrib

    cur = lax.shift_right_logical(tpos, 6)
    jrow = lax.broadcasted_iota(jnp.int32, (nblk, 1), 0)
    sub = lax.broadcasted_iota(jnp.int32, (8, 1), 0)
    n_slab = nblk // 8
    for g in range(G):
        imp = jnp.where(jrow > cur, -jnp.inf, imps[g])
        imp = jnp.where((jrow == 0) | (jrow == cur) | (jrow == cur - 1), jnp.inf, imp)
        slabs = [imp[8 * v:8 * v + 8, :] for v in range(n_slab)]
        ranks = [jnp.zeros((8, tq), F32) for _ in range(n_slab)]
        for jj in range(nblk):
            row = imp[jj:jj + 1, :]
            for v in range(n_slab):
                if v > jj // 8:
                    beats = row >= slabs[v]
                elif v < jj // 8:
                    beats = row > slabs[v]
                else:
                    beats = (row > slabs[v]) | ((row == slabs[v]) & (sub > jj % 8))
                ranks[v] = ranks[v] + jnp.where(beats, 1.0, 0.0)
        rank = jnp.concatenate(ranks, axis=0)
        mb = jnp.where(rank < float(min(N_SEL, nblk)), 0.0, NEG_INF)
        mbt = jnp.concatenate([mb, jnp.zeros((LANES - nblk, tq), F32)], axis=0).T.astype(BF16)
        for r in range(R):
            qs_ref[g * R + r] = jnp.concatenate([q_of(g * R + r), mbt], axis=1)

    rel = lax.broadcasted_iota(jnp.int32, (TKV, tq), 0) - lax.broadcasted_iota(jnp.int32, (TKV, tq), 1)

    def score(kt, b, sel, keep=None):
        rows = pl.ds(pl.multiple_of(kt * TKV, TKV), TKV)
        for g in range(G):
            if sel:
                k = jnp.concatenate([ks_ref[0, rows, g * LANES:(g + 1) * LANES], oh_ref[rows, :]], axis=1)
            else:
                k = kw_ref[0, rows, g * LANES:(g + 1) * LANES]
            for r in range(R):
                h = g * R + r
                s = _dot_nt(k, qs_ref[h] if sel else q_of(h))
                if keep is not None:
                    s = jnp.where(keep, s, NEG_INF)
                _stage_scores(s, s_ref, mx_ref, b, h)

    def update(kt, b, sel):
        vt_ref, state = (vst_ref, (m_ref, acc_ref)) if sel else (vwt_ref, (mw_ref, accw_ref))
        for h in range(G * R):
            g = h // R
            _online_update(vt_ref[0, kt, g * dh:(g + 1) * dh, :], s_ref, mx_ref, b, *state, h)

    _pipelined_sweep(i, functools.partial(score, sel=True), functools.partial(update, sel=True))

    assert W_WIN == 2 * TKV
    causal, near = rel <= 0, rel > 0

    @pl.when(i >= 2)
    def _():
        score(i, 0, True, causal)
        score(i, 1, False, causal)
        score(i - 1, 2, False)
        score(i - 2, 3, False, near)
        update(i, 0, True)
        update(i, 1, False)
        update(i - 1, 2, False)
        update(i - 2, 3, False)

    @pl.when(i < 2)
    def _():
        score(i, 0, True, causal)
        score(i, 1, False, causal)
        update(i, 0, True)
        update(i, 1, False)

    @pl.when(i == 1)
    def _():
        score(0, 2, False)
        update(0, 2, False)

    sig = 1.0 / (1.0 + jnp.exp(-gt_ref[0].T[0:G * 16, :]))
    outs = []
    for h in range(G * R):
        row = (h // R) * 16 + (h % R) * 4
        outs.append(sig[row:row + 1, :] * ocmp_ref[h] + sig[row + 1:row + 2, :] * _normalised(acc_ref, h, dh)
                    + sig[row + 2:row + 3, :] * _normalised(accw_ref, h, dh))
    o_ref[0] = jnp.concatenate(outs, axis=0).T.astype(BF16)


def _nsa_attn(qn, kc, vct, ksw, vt, gt, ovt, oh, B, S):
    nkt = S // TKV
    nc = kc.shape[1]
    G = NSA_KV_HEADS
    qn3 = qn.reshape(B, S, NSA_HEADS * LANES)
    ksw3 = ksw.reshape(B, S, 4 * LANES)
    gt3 = gt.reshape(B, S, LANES)
    H, dh, tq = NSA_HEADS, NSA_DH, TQ_NSA
    state = [pltpu.VMEM((H, 1, tq), F32), pltpu.VMEM((H, dh + ONES_ROWS, tq), F32)]
    return pl.pallas_call(
        _nsa_attn_kernel,
        grid=(B, S // tq),
        in_specs=[pl.BlockSpec((1, tq, H * LANES), lambda b, i: (b, i, 0)),
                  pl.BlockSpec((1, nc, G * LANES), lambda b, i: (b, 0, 0)),
                  pl.BlockSpec((1, G * dh, nc), lambda b, i: (b, 0, 0)),
                  pl.BlockSpec((1, S, G * LANES), lambda b, i: (b, 0, 0)),
                  pl.BlockSpec((1, S, G * LANES), lambda b, i: (b, 0, 1)),
                  pl.BlockSpec((1, nkt, G * dh, TKV), lambda b, i: (b, 0, 0, 0)),
                  pl.BlockSpec((1, nkt, G * dh, TKV), lambda b, i: (b, 0, 1, 0)),
                  pl.BlockSpec((1, tq, LANES), lambda b, i: (b, i, 0)),
                  pl.BlockSpec(ovt.shape, lambda b, i: (0, 0)),
                  pl.BlockSpec(oh.shape, lambda b, i: (0, 0))],
        out_specs=pl.BlockSpec((1, tq, H * dh), lambda b, i: (b, i, 0)),
        out_shape=jax.ShapeDtypeStruct((B, S, H * dh), BF16),
        scratch_shapes=([pltpu.VMEM((H, dh, tq), F32)] + state + state
                        + [pltpu.VMEM((4, H, TKV, tq), F32), pltpu.VMEM((4, H, 1, tq), F32),
                           pltpu.VMEM((H, tq, 2 * LANES), BF16)]),
        compiler_params=_cparams(("parallel", "arbitrary")),
        name="nsa_attn",
    )(qn3, kc, vct, ksw3, ksw3, vt, vt, gt3, ovt, oh)


def _outproj_kernel(h_ref, a_ref, b_ref, wa_ref, wb_ref, o_ref):
    o_ref[...] = h_ref[...] + _dot(a_ref[...], wa_ref[...]) + _dot(b_ref[...], wb_ref[...])


def _outproj(h, oa, ob, wa, wb):
    M = h.shape[0]
    full = lambda a: pl.BlockSpec(a.shape, lambda i: (0,) * a.ndim)
    row = lambda w: pl.BlockSpec((TM, w), lambda i: (i, 0))
    return pl.pallas_call(
        _outproj_kernel,
        grid=(M // TM,),
        in_specs=[row(D_MODEL), row(oa.shape[1]), row(ob.shape[1]), full(wa), full(wb)],
        out_specs=row(D_MODEL),
        out_shape=jax.ShapeDtypeStruct((M, D_MODEL), F32),
        compiler_params=_cparams(("parallel",)),
        name="attn_outproj",
    )(h, oa, ob, wa, wb)


FF_CHUNK = 256
HALO = 8


def _ffn_kernel(x_ref, xp_ref, g_ref, wup_ref, cw_ref, cb_ref, wdn_ref, fg_ref, o_ref, *, final, spt):
    i = pl.program_id(0)
    x = x_ref[...]
    xe = jnp.concatenate([xp_ref[...], x], axis=0)
    n = _rms(xe, g_ref[...]).astype(BF16)
    keep = jnp.where(i % spt == 0, 0.0, 1.0)
    rows = lax.broadcasted_iota(jnp.int32, (HALO + TM, 1), 0)
    halo_scale = jnp.where(rows < HALO, keep, 1.0)
    acc = x

    def conv(hx, col):
        hx = hx * halo_scale
        cw = cw_ref[:, col:col + FF_CHUNK]
        out = cb_ref[:, col:col + FF_CHUNK]
        for j in range(CONV_W):
            lo = HALO - (CONV_W - 1) + j
            out = out + hx[lo:lo + TM, :] * cw[j:j + 1, :]
        return out

    for c in range(D_FF // FF_CHUNK):
        col = c * FF_CHUNK
        gate = conv(_dot(n, wup_ref[:, col:col + FF_CHUNK]), col)
        up = conv(_dot(n, wup_ref[:, D_FF + col:D_FF + col + FF_CHUNK]), D_FF + col)
        act = (gate * (1.0 / (1.0 + jnp.exp(-gate))) * up).astype(BF16)
        acc = acc + _dot(act, wdn_ref[col:col + FF_CHUNK, :])
    if final:
        acc = _rms(acc, fg_ref[...])
    o_ref[...] = acc


def _ffn(h, g, wup, cw, cb, wdn, fg, S, final):
    M = h.shape[0]
    spt = S // TM
    full = lambda a: pl.BlockSpec(a.shape, lambda i: (0,) * a.ndim)
    row = pl.BlockSpec((TM, D_MODEL), lambda i: (i, 0))
    prev = pl.BlockSpec((HALO, D_MODEL), lambda i: (jnp.maximum(i * (TM // HALO) - 1, 0), 0))
    return pl.pallas_call(
        functools.partial(_ffn_kernel, final=final, spt=spt),
        grid=(M // TM,),
        in_specs=[row, prev, full(g), full(wup), full(cw), full(cb), full(wdn), full(fg)],
        out_specs=row,
        out_shape=jax.ShapeDtypeStruct((M, D_MODEL), F32),
        compiler_params=_cparams(("parallel",)),
        name="conv_ffn",
    )(h, h, g, wup, cw, cb, wdn, fg)


def _gather_cols(w, idx):
    idx = np.asarray(idx)
    return jnp.where(jnp.asarray(idx >= 0)[None, :], w[:, np.maximum(idx, 0)], 0.0)


def _slab_idx(n_slabs, width, base_fn):
    idx = -np.ones((n_slabs, LANES), np.int64)
    for s in range(n_slabs):
        idx[s, :width] = base_fn(s) + np.arange(width)
    return idx.reshape(-1)


def _inproj_weights(w_in):
    G, dh = NSA_KV_HEADS, NSA_DH
    ia = -np.ones(6 * LANES, np.int64)
    ia[0:MLA_Q_RANK] = OFF_CQ + np.arange(MLA_Q_RANK)
    ia[MLA_Q_RANK:MLA_Q_RANK + MLA_KV_RANK] = OFF_CKV + np.arange(MLA_KV_RANK)
    ia[5 * LANES + MLA_NOPE:5 * LANES + MLA_NOPE + MLA_ROPE] = OFF_KROPE + np.arange(MLA_ROPE)
    igt = -np.ones(LANES, np.int64)
    for g in range(G):
        for r in range(NSA_REP):
            for c in range(3):
                igt[g * 16 + r * 4 + c] = OFF_GT + (g * NSA_REP + r) * 3 + c
    ib = np.concatenate([igt, OFF_KC + np.arange(G * dh), OFF_VC + np.arange(G * dh)])
    iq = _slab_idx(NSA_HEADS, dh, lambda h: OFF_QN + h * dh)
    ik = np.concatenate([_slab_idx(G, dh, lambda g: OFF_KS + g * dh), _slab_idx(G, dh, lambda g: OFF_KW + g * dh)])
    iv = np.concatenate([OFF_VS + np.arange(G * dh), OFF_VW + np.arange(G * dh)])
    wa = _gather_cols(w_in, ia).astype(BF16)
    wb = _gather_cols(w_in, ib).astype(BF16)
    wq = (_gather_cols(w_in, iq) * (NSA_DH ** -0.5 * LOG2E)).astype(BF16)
    wk = _gather_cols(w_in, ik).astype(BF16)
    wvt = w_in[:, iv].T.astype(BF16)
    return wa, wb, wq, wk, wvt


def _bf16_terms(x, n):
    terms = []
    for _ in range(n):
        bits = np.array(x, np.float32).view(np.uint32)
        bits = (bits + np.uint32(0x7FFF) + ((bits >> np.uint32(16)) & np.uint32(1))) & np.uint32(0xFFFF0000)
        terms.append(float(bits.view(np.float32)))
        x = x - terms[-1]
    return terms


N_POS_TERMS = 3


def _pos_features(pos, n_slabs):
    out = np.zeros((len(pos), n_slabs * LANES), np.float32)
    for s in range(n_slabs):
        base = s * LANES + NSA_DH
        out[:, base:base + N_POS_TERMS] = (pos // L_SEL)[:, None]
        out[:, base + N_POS_TERMS:base + 2 * N_POS_TERMS] = (pos % L_SEL)[:, None]
    return out


def _const_tables(S):
    slopes = 2.0 ** (-8.0 * np.arange(1, NSA_HEADS + 1) / NSA_HEADS)
    terms = np.array(_bf16_terms(LOG2E, N_POS_TERMS))
    qaug = np.zeros((1, NSA_HEADS * LANES), np.float32)
    for h in range(NSA_HEADS):
        base = h * LANES + NSA_DH
        qaug[0, base:base + N_POS_TERMS] = slopes[h] * L_SEL * terms
        qaug[0, base + N_POS_TERMS:base + 2 * N_POS_TERMS] = slopes[h] * terms
    posf = _pos_features(np.arange(S), 4)
    nc = S // D_CMP
    feat = _pos_features(np.arange(nc) * D_CMP + L_CMP - 1, 2)
    n_cmp = (S - L_CMP) // D_CMP + 1
    nblk = S // L_SEL
    cs = np.arange(n_cmp) * D_CMP
    ss = np.arange(nblk) * L_SEL
    ov = ((cs[:, None] < ss[None, :] + L_SEL) & (cs[:, None] + L_CMP > ss[None, :])).astype(np.float32)
    ovt = np.zeros((nblk, nc), np.float32)
    ovt[:, :n_cmp] = ov.T
    assert nblk <= LANES
    onehot = (np.arange(S)[:, None] // L_SEL == np.arange(LANES)[None, :]).astype(np.float32)
    return jnp.asarray(qaug), jnp.asarray(posf), jnp.asarray(feat), jnp.asarray(ovt), jnp.asarray(onehot, BF16)


def _rope_consts(S):
    half = MLA_ROPE // 2
    inv = 1.0 / (ROPE_THETA ** (jnp.arange(0, MLA_ROPE, 2, dtype=F32) / MLA_ROPE))
    ang = jnp.arange(S, dtype=F32)[:, None] * inv[None, :]
    cos, sin = jnp.cos(ang), jnp.sin(ang)
    z = lambda w: jnp.zeros((S, w), F32)
    ct = jnp.concatenate([jnp.ones((S, MLA_NOPE), F32), cos, cos, z(LANES - MLA_QK)], axis=1)
    sa = jnp.concatenate([z(MLA_NOPE), -sin, z(LANES - MLA_NOPE - half)], axis=1)
    sb = jnp.concatenate([z(MLA_NOPE + half), sin, z(LANES - MLA_QK)], axis=1)
    return ct, sa, sb


def _mla_weights(w_uq, w_ukv):
    iq = _slab_idx(MLA_HEADS, MLA_QK, lambda h: h * MLA_QK)
    wq = (_gather_cols(w_uq, iq) * (MLA_QK ** -0.5 * LOG2E)).astype(BF16)
    ik = _slab_idx(MLA_HEADS, MLA_NOPE, lambda h: h * (MLA_NOPE + MLA_V))
    wk = _gather_cols(w_ukv, ik).astype(BF16)
    iv = np.concatenate([h * (MLA_NOPE + MLA_V) + MLA_NOPE + np.arange(MLA_V) for h in range(MLA_HEADS)])
    wvt = w_ukv[:, iv].T.astype(BF16)
    return wq, wk, wvt


def _compress_weights(pos, w1, w2):
    G, dh, half = NSA_KV_HEADS, NSA_DH, L_CMP // 2
    eye = jnp.eye(G, dtype=F32)

    def first_layer(w):
        return jnp.einsum('lde,gk->lgdke', w, eye).reshape(half * G * dh, G * CMP_HID).astype(BF16)

    def pos_row(p):
        return jnp.broadcast_to(p[:, None, :], (half, G, dh)).reshape(1, half * G * dh)

    w2k = jnp.einsum('ed,gk->gekd', jnp.pad(w2, ((0, 0), (0, LANES - dh))), eye).reshape(G * CMP_HID, G * LANES)
    w2vt = jnp.einsum('ed,gk->gdke', w2, eye).reshape(G * dh, G * CMP_HID)
    return (pos_row(pos[:half]), pos_row(pos[half:]), first_layer(w1[:half]), first_layer(w1[half:]),
            w2k.astype(BF16), w2vt.astype(BF16))


def kernel(x, attn_norm, w_in, q_norm, kv_norm, w_uq, w_ukv, cmp_pos_k, cmp_pos_v, cmp_k_w1, cmp_k_w2, cmp_v_w1,
           cmp_v_w2, w_o, ffn_norm, w_up, conv_w, conv_b, w_down, final_norm):
    B, S, D = x.shape
    M = B * S
    assert D == D_MODEL and S % TM == 0 and S % TQ_MLA == 0 and TQ_MLA % TKV == 0 and TQ_NSA == TKV
    qaug, posf, feat, ovt, onehot = _const_tables(S)
    ct, sa, sb = _rope_consts(S)
    nc = S // D_CMP
    h = x.reshape(M, D)
    for l in range(DEPTH):
        wa, wb, wq, wk, wvt = _inproj_weights(w_in[l])
        za, gt, kcr, vcr, qn, ksw, vt = _inproj(h, attn_norm[l][None, :], wa, wb, wq, wk, wvt, qaug, posf, B, S)

        mq, mk, mvt = _mla_weights(w_uq[l], w_ukv[l])
        q, k, vmt = _mla_prep(za, q_norm[l][None, :], kv_norm[l][None, :], mq, mk, mvt, ct, sa, sb, B, S)
        o_mla = _mla_attn(q.reshape(B, S, -1), k.reshape(B, S, -1), vmt, B, S)

        pak, pbk, wak, wbk, w2k, _ = _compress_weights(cmp_pos_k[l], cmp_k_w1[l], cmp_k_w2[l])
        pav, pbv, wav, wbv, _, w2vt = _compress_weights(cmp_pos_v[l], cmp_v_w1[l], cmp_v_w2[l])
        kc, vct = _compress(kcr.reshape(B, nc, -1), vcr.reshape(B, nc, -1), pak, pbk, pav, pbv,
                            wak, wbk, wav, wbv, w2k, w2vt, feat)
        o_nsa = _nsa_attn(qn, kc, vct, ksw, vt, gt, ovt, onehot, B, S)

        wo = w_o[l].astype(BF16)
        half = MLA_HEADS * MLA_V
        h = _outproj(h, o_mla.reshape(M, -1), o_nsa.reshape(M, -1), wo[:half], wo[half:])
        h = _ffn(h, ffn_norm[l][None, :], w_up[l].astype(BF16), conv_w[l], conv_b[l][None, :],
                 w_down[l].astype(BF16), final_norm[None, :], S, final=(l == DEPTH - 1))
    return h.reshape(B, S, D)
```

```python
import functools

import numpy as np
import jax
import jax.numpy as jnp
from jax import lax
from jax.experimental import pallas as pl
from jax.experimental.pallas import tpu as pltpu

F32 = jnp.float32
BF16 = jnp.bfloat16

D_MODEL = 1024
DEPTH = 2
EPS = 1e-6
NEG_INF = -1e30
LOG2E = float(np.log2(np.e))
LANES = 128

MLA_HEADS = 8
MLA_NOPE = 64
MLA_ROPE = 32
MLA_V = 64
MLA_Q_RANK = 384
MLA_KV_RANK = 256
MLA_QK = MLA_NOPE + MLA_ROPE
ROPE_THETA = 10000.0

NSA_HEADS = 8
NSA_KV_HEADS = 2
NSA_REP = NSA_HEADS // NSA_KV_HEADS
NSA_DH = 64
L_CMP = 32
D_CMP = 16
CMP_HID = 128
L_SEL = 64
N_SEL = 16
W_WIN = 512

D_FF = 2816
CONV_W = 3

_SIZES = [MLA_Q_RANK, MLA_KV_RANK, MLA_ROPE, NSA_HEADS * NSA_DH] + [NSA_KV_HEADS * NSA_DH] * 6 + [3 * NSA_HEADS]
_OFFS = [0] + [int(o) for o in np.cumsum(_SIZES)]
(OFF_CQ, OFF_CKV, OFF_KROPE, OFF_QN, OFF_KC, OFF_VC, OFF_KS, OFF_VS, OFF_KW, OFF_VW, OFF_GT, D_IN) = _OFFS

TM = 512
TKV = 256
TQ_MLA = 256
TQ_NSA = 256
VMEM_LIMIT = 56 * 1024 * 1024

W_ZA = 6 * LANES
W_ZB = 3 * LANES
W_QN = NSA_HEADS * LANES
W_KS = 4 * LANES
W_ROW = W_ZA + W_ZB + W_QN + W_KS


def _cparams(sem):
    return pltpu.CompilerParams(dimension_semantics=sem, vmem_limit_bytes=VMEM_LIMIT)


def _layer_spec(a, l):
    return pl.BlockSpec((1,) + a.shape[1:], lambda *_: (l,) + (0,) * (a.ndim - 1))


def _const_spec(a):
    return pl.BlockSpec(a.shape, lambda *_: (0,) * a.ndim)


def _dot(a, b):
    return jnp.dot(a, b, preferred_element_type=F32)


def _dot_nt(a, b):
    return lax.dot_general(a, b, (((1,), (1,)), ((), ())), preferred_element_type=F32)


def _rms(x, g):
    return x * lax.rsqrt(jnp.mean(x * x, axis=-1, keepdims=True) + EPS) * g


def _inproj_kernel(x_ref, g_ref, w_ref, wvt_ref, qaug_ref, posf_ref,
                   za_ref, gt_ref, kcr_ref, vcr_ref, qn_ref, ksw_ref, vt_ref):
    n = _rms(x_ref[...], g_ref[0]).astype(BF16)
    c0, c1, c2 = W_ZA, W_ZA + W_ZB, W_ZA + W_ZB + W_QN
    za_ref[...] = _dot(n, w_ref[0, :, 0:c0])
    zb = _dot(n, w_ref[0, :, c0:c1])
    gt_ref[...] = zb[:, 0:LANES]
    kcr_ref[...] = zb[:, LANES:2 * LANES]
    vcr_ref[...] = zb[:, 2 * LANES:3 * LANES]
    qn_ref[...] = (_dot(n, w_ref[0, :, c1:c2]) + qaug_ref[...]).astype(BF16)
    ksw_ref[...] = (_dot(n, w_ref[0, :, c2:W_ROW]) + posf_ref[...]).astype(BF16)
    vt = _dot_nt(wvt_ref[0], n).astype(BF16)
    for c in range(TM // TKV):
        vt_ref[0, c] = vt[:, c * TKV:(c + 1) * TKV]


def _inproj(h, g, w, wvt, qaug, posf, l, B, S):
    M = B * S
    spt = S // TM
    row = lambda width: pl.BlockSpec((TM, width), lambda i: (i, 0))
    return pl.pallas_call(
        _inproj_kernel,
        grid=(M // TM,),
        in_specs=[row(D_MODEL), _layer_spec(g, l), _layer_spec(w, l), _layer_spec(wvt, l), _const_spec(qaug),
                  pl.BlockSpec((TM, W_KS), lambda i: (i % spt, 0))],
        out_specs=[row(W_ZA), row(LANES), row(LANES), row(LANES), row(W_QN), row(W_KS),
                   pl.BlockSpec((1, TM // TKV, 4 * NSA_DH, TKV), lambda i: (i // spt, i % spt, 0, 0))],
        out_shape=[jax.ShapeDtypeStruct((M, W_ZA), F32),
                   jax.ShapeDtypeStruct((M, LANES), F32),
                   jax.ShapeDtypeStruct((M, LANES), F32),
                   jax.ShapeDtypeStruct((M, LANES), F32),
                   jax.ShapeDtypeStruct((M, W_QN), BF16),
                   jax.ShapeDtypeStruct((M, W_KS), BF16),
                   jax.ShapeDtypeStruct((B, S // TKV, 4 * NSA_DH, TKV), BF16)],
        compiler_params=_cparams(("parallel",)),
        name="attn_inproj",
    )(h, g, w, wvt, qaug, posf)


def _rope_slab(x, ct, sa, sb):
    return x * ct + pltpu.roll(x, LANES - MLA_ROPE // 2, 1) * sa + pltpu.roll(x, MLA_ROPE // 2, 1) * sb


def _mla_prep_kernel(za_ref, qg_ref, kg_ref, wq_ref, wk_ref, wvt_ref, ct_ref, sa_ref, sb_ref,
                     q_ref, k_ref, vt_ref):
    za = za_ref[...]
    cq = _rms(za[:, 0:MLA_Q_RANK], qg_ref[0]).astype(BF16)
    ckv = _rms(za[:, MLA_Q_RANK:MLA_Q_RANK + MLA_KV_RANK], kg_ref[0]).astype(BF16)
    ct, sa, sb = ct_ref[...], sa_ref[...], sb_ref[...]
    kr = _rope_slab(za[:, 5 * LANES:6 * LANES], ct, sa, sb)
    q = _dot(cq, wq_ref[0])
    k = _dot(ckv, wk_ref[0])
    for h in range(MLA_HEADS):
        sl = slice(h * LANES, (h + 1) * LANES)
        q_ref[:, sl] = _rope_slab(q[:, sl], ct, sa, sb).astype(BF16)
        k_ref[:, sl] = (k[:, sl] + kr).astype(BF16)
    vt = _dot_nt(wvt_ref[0], ckv).astype(BF16)
    for c in range(TM // TKV):
        vt_ref[0, c] = vt[:, c * TKV:(c + 1) * TKV]


def _mla_prep(za, qg, kg, wq, wk, wvt, ct, sa, sb, l, B, S):
    M = B * S
    spt = S // TM
    row = lambda width: pl.BlockSpec((TM, width), lambda i: (i, 0))
    tab = pl.BlockSpec((TM, LANES), lambda i: (i % spt, 0))
    return pl.pallas_call(
        _mla_prep_kernel,
        grid=(M // TM,),
        in_specs=[row(W_ZA), _layer_spec(qg, l), _layer_spec(kg, l), _layer_spec(wq, l), _layer_spec(wk, l),
                  _layer_spec(wvt, l), tab, tab, tab],
        out_specs=[row(8 * LANES), row(8 * LANES),
                   pl.BlockSpec((1, TM // TKV, MLA_HEADS * MLA_V, TKV), lambda i: (i // spt, i % spt, 0, 0))],
        out_shape=[jax.ShapeDtypeStruct((M, 8 * LANES), BF16),
                   jax.ShapeDtypeStruct((M, 8 * LANES), BF16),
                   jax.ShapeDtypeStruct((B, S // TKV, MLA_HEADS * MLA_V, TKV), BF16)],
        compiler_params=_cparams(("parallel",)),
        name="mla_prep",
    )(za, qg, kg, wq, wk, wvt, ct, sa, sb)


ONES_ROWS = 16


def _stage_scores(s, s_ref, mx_ref, b, h):
    mx_ref[b, h] = jnp.max(s, axis=0, keepdims=True)
    s_ref[b, h, 0:s.shape[0], :] = s


def _online_update(vt, s_ref, mx_ref, b, m_ref, acc_ref, h):
    m_old = m_ref[h]
    m_new = jnp.maximum(m_old, mx_ref[b, h])
    p = jnp.exp2(s_ref[b, h] - m_new).astype(BF16)
    vt1 = jnp.concatenate([vt, jnp.ones((ONES_ROWS, vt.shape[1]), BF16)], axis=0)
    acc_ref[h] = jnp.exp2(m_old - m_new) * acc_ref[h] + _dot(vt1, p)
    m_ref[h] = m_new


def _normalised(acc_ref, h, dh):
    return acc_ref[h, 0:dh, :] / acc_ref[h, dh:dh + 1, :]


def _pipelined_sweep(i, score, update):
    @pl.when(i >= 1)
    def _():
        score(0, 0)

    @pl.loop(0, i // 2)
    def _(j):
        score(2 * j + 1, 1)
        update(2 * j, 0)
        score(2 * j + 2, 0)
        update(2 * j + 1, 1)

    @pl.when(i % 2 == 1)
    def _():
        update(i - 1, 0)


def _mla_attn_kernel(q_ref, k_ref, vt_ref, o_ref, m_ref, acc_ref, s_ref, mx_ref):
    i = pl.program_id(1)
    tq = TQ_MLA
    m_ref[...] = jnp.full(m_ref.shape, -jnp.inf, F32)
    acc_ref[...] = jnp.zeros(acc_ref.shape, F32)

    def score(kt, b, keep=None):
        for h in range(MLA_HEADS):
            sl = slice(h * LANES, (h + 1) * LANES)
            k = k_ref[0, pl.ds(pl.multiple_of(kt * TKV, TKV), TKV), sl]
            s = _dot_nt(k, q_ref[0, :, sl])
            if keep is not None:
                s = jnp.where(keep, s, NEG_INF)
            _stage_scores(s, s_ref, mx_ref, b, h)

    def update(kt, b):
        for h in range(MLA_HEADS):
            _online_update(vt_ref[0, kt, h * MLA_V:(h + 1) * MLA_V, :], s_ref, mx_ref, b, m_ref, acc_ref, h)

    _pipelined_sweep(i, score, update)
    score(i, 0, lax.broadcasted_iota(jnp.int32, (TKV, tq), 0) <= lax.broadcasted_iota(jnp.int32, (TKV, tq), 1))
    update(i, 0)
    out = jnp.concatenate([_normalised(acc_ref, h, MLA_V) for h in range(MLA_HEADS)], axis=0)
    o_ref[0] = out.T.astype(BF16)


def _mla_attn(q, k, vt, B, S):
    nkt = S // TKV
    hw = MLA_HEADS * LANES
    return pl.pallas_call(
        _mla_attn_kernel,
        grid=(B, S // TQ_MLA),
        in_specs=[pl.BlockSpec((1, TQ_MLA, hw), lambda b, i: (b, i, 0)),
                  pl.BlockSpec((1, S, hw), lambda b, i: (b, 0, 0)),
                  pl.BlockSpec((1, nkt, MLA_HEADS * MLA_V, TKV), lambda b, i: (b, 0, 0, 0))],
        out_specs=pl.BlockSpec((1, TQ_MLA, MLA_HEADS * MLA_V), lambda b, i: (b, i, 0)),
        out_shape=jax.ShapeDtypeStruct((B, S, MLA_HEADS * MLA_V), BF16),
        scratch_shapes=[pltpu.VMEM((MLA_HEADS, 1, TQ_MLA), F32),
                        pltpu.VMEM((MLA_HEADS, MLA_V + ONES_ROWS, TQ_MLA), F32),
                        pltpu.VMEM((2, MLA_HEADS, TKV, TQ_MLA), F32), pltpu.VMEM((2, MLA_HEADS, 1, TQ_MLA), F32)],
        compiler_params=_cparams(("parallel", "arbitrary")),
        name="mla_attn",
    )(q, k, vt)


def _gelu_tanh(x):
    return 0.5 * x * (1.0 + jnp.tanh(np.sqrt(2.0 / np.pi).astype(np.float32) * (x + 0.044715 * (x * x * x))))


def _compress_kernel(kx_ref, vx_ref, pa_ref, pb_ref, wa_ref, wb_ref, w2k_ref, w2vt_ref, feat_ref, kc_ref, vct_ref):
    nc = kx_ref.shape[1]

    def hidden(x, kv):
        ya = _dot((x + pa_ref[0, kv]).astype(BF16), wa_ref[0, kv])
        yb = _dot((x + pb_ref[0, kv]).astype(BF16), wb_ref[0, kv])
        return _gelu_tanh(ya + pltpu.roll(yb, nc - 1, 0)).astype(BF16)

    kc_ref[0] = (_dot(hidden(kx_ref[0], 0), w2k_ref[0]) + feat_ref[...]).astype(BF16)
    vct_ref[0] = _dot_nt(w2vt_ref[0], hidden(vx_ref[0], 1)).astype(BF16)


def _compress(kx, vx, pa, pb, wa, wb, w2k, w2vt, feat, l):
    B, nc, width = kx.shape
    xs = pl.BlockSpec((1, nc, width), lambda b: (b, 0, 0))
    return pl.pallas_call(
        _compress_kernel,
        grid=(B,),
        in_specs=[xs, xs] + [_layer_spec(a, l) for a in (pa, pb, wa, wb, w2k, w2vt)] + [_const_spec(feat)],
        out_specs=[pl.BlockSpec((1, nc, 2 * LANES), lambda b: (b, 0, 0)),
                   pl.BlockSpec((1, 2 * NSA_DH, nc), lambda b: (b, 0, 0))],
        out_shape=[jax.ShapeDtypeStruct((B, nc, 2 * LANES), BF16),
                   jax.ShapeDtypeStruct((B, 2 * NSA_DH, nc), BF16)],
        compiler_params=_cparams(("parallel",)),
        name="nsa_compress",
    )(kx, vx, pa, pb, wa, wb, w2k, w2vt, feat)


def _nsa_attn_kernel(qn_ref, kc_ref, vct_ref, ks_ref, kw_ref, vst_ref, vwt_ref, gt_ref, ovt_ref, oh_ref,
                     o_ref, ocmp_ref, m_ref, acc_ref, mw_ref, accw_ref, s_ref, mx_ref, qs_ref):
    i = pl.program_id(1)
    tq = TQ_NSA
    G, R, dh = NSA_KV_HEADS, NSA_REP, NSA_DH
    nc = kc_ref.shape[1]
    nblk = ovt_ref.shape[0]
    t0 = i * tq
    tpos = t0 + lax.broadcasted_iota(jnp.int32, (1, tq), 1)
    q_of = lambda h: qn_ref[0, :, h * LANES:(h + 1) * LANES]

    for ref in (m_ref, mw_ref):
        ref[...] = jnp.full(ref.shape, -jnp.inf, F32)
    for ref in (acc_ref, accw_ref):
        ref[...] = jnp.zeros(ref.shape, F32)

    nrow = lax.broadcasted_iota(jnp.int32, (nc, 1), 0)
    bias_c = jnp.where((tpos >= nrow * D_CMP + (L_CMP - 1)) & (nrow < nc - 1), 0.0, NEG_INF)
    has_cmp = tpos >= L_CMP - 1
    for h in range(G * R):
        g = h // R
        _stage_scores(_dot_nt(kc_ref[0, :, g * LANES:(g + 1) * LANES], q_of(h)) + bias_c, s_ref, mx_ref, 0, h)
    imps = [None] * G
    for h in range(G * R):
        g = h // R
        e = jnp.exp2(s_ref[0, h, 0:nc, :] - mx_ref[0, h])
        inv = jnp.where(has_cmp, 1.0 / jnp.sum(e, axis=0, keepdims=True), 0.0)
        e_hi = e.astype(BF16)
        e_lo = (e - e_hi.astype(F32)).astype(BF16)
        both = _dot(jnp.concatenate([vct_ref[0, g * dh:(g + 1) * dh, :], ovt_ref[...]], axis=0), e_hi)
        ocmp_ref[h] = both[0:dh, :] * inv
        contrib = (both[dh:dh + nblk, :] + _dot(ovt_ref[...], e_lo)) * inv
        imps[g] = contrib if imps[g] is None else imps[g] + contrib

    cur = lax.shift_right_logical(tpos, 6)
    jrow = lax.broadcasted_iota(jnp.int32, (nblk, 1), 0)
    sub = lax.broadcasted_iota(jnp.int32, (8, 1), 0)
    n_slab = nblk // 8
    for g in range(G):
        imp = jnp.where(jrow > cur, -jnp.inf, imps[g])
        imp = jnp.where((jrow == 0) | (jrow == cur) | (jrow == cur - 1), jnp.inf, imp)
        slabs = [imp[8 * v:8 * v + 8, :] for v in range(n_slab)]
        ranks = [jnp.zeros((8, tq), F32) for _ in range(n_slab)]
        for jj in range(nblk):
            row = imp[jj:jj + 1, :]
            for v in range(n_slab):
                if v > jj // 8:
                    beats = row >= slabs[v]
                elif v < jj // 8:
                    beats = row > slabs[v]
                else:
                    beats = (row > slabs[v]) | ((row == slabs[v]) & (sub > jj % 8))
                ranks[v] = ranks[v] + jnp.where(beats, 1.0, 0.0)
        rank = jnp.concatenate(ranks, axis=0)
        mb = jnp.where(rank < float(min(N_SEL, nblk)), 0.0, NEG_INF)
        mbt = jnp.concatenate([mb, jnp.zeros((LANES - nblk, tq), F32)], axis=0).T.astype(BF16)
        for r in range(R):
            qs_ref[g * R + r] = jnp.concatenate([q_of(g * R + r), mbt], axis=1)

    rel = lax.broadcasted_iota(jnp.int32, (TKV, tq), 0) - lax.broadcasted_iota(jnp.int32, (TKV, tq), 1)

    def score(kt, b, sel, keep=None):
        rows = pl.ds(pl.multiple_of(kt * TKV, TKV), TKV)
        for g in range(G):
            if sel:
                k = jnp.concatenate([ks_ref[0, rows, g * LANES:(g + 1) * LANES], oh_ref[rows, :]], axis=1)
            else:
                k = kw_ref[0, rows, g * LANES:(g + 1) * LANES]
            for r in range(R):
                h = g * R + r
                s = _dot_nt(k, qs_ref[h] if sel else q_of(h))
                if keep is not None:
                    s = jnp.where(keep, s, NEG_INF)
                _stage_scores(s, s_ref, mx_ref, b, h)

    def update(kt, b, sel):
        vt_ref, state = (vst_ref, (m_ref, acc_ref)) if sel else (vwt_ref, (mw_ref, accw_ref))
        for h in range(G * R):
            g = h // R
            _online_update(vt_ref[0, kt, g * dh:(g + 1) * dh, :], s_ref, mx_ref, b, *state, h)

    _pipelined_sweep(i, functools.partial(score, sel=True), functools.partial(update, sel=True))

    assert W_WIN == 2 * TKV
    causal, near = rel <= 0, rel > 0

    @pl.when(i >= 2)
    def _():
        score(i, 0, True, causal)
        score(i, 1, False, causal)
        score(i - 1, 2, False)
        score(i - 2, 3, False, near)
        update(i, 0, True)
        update(i, 1, False)
        update(i - 1, 2, False)
        update(i - 2, 3, False)

    @pl.when(i < 2)
    def _():
        score(i, 0, True, causal)
        score(i, 1, False, causal)
        update(i, 0, True)
        update(i, 1, False)

    @pl.when(i == 1)
    def _():
        score(0, 2, False)
        update(0, 2, False)

    sig = 1.0 / (1.0 + jnp.exp(-gt_ref[0].T[0:G * 16, :]))
    outs = []
    for h in range(G * R):
        row = (h // R) * 16 + (h % R) * 4
        outs.append(sig[row:row + 1, :] * ocmp_ref[h] + sig[row + 1:row + 2, :] * _normalised(acc_ref, h, dh)
                    + sig[row + 2:row + 3, :] * _normalised(accw_ref, h, dh))
    o_ref[0] = jnp.concatenate(outs, axis=0).T.astype(BF16)


def _nsa_attn(qn, kc, vct, ksw, vt, gt, ovt, oh, B, S):
    nkt = S // TKV
    nc = kc.shape[1]
    G = NSA_KV_HEADS
    qn3 = qn.reshape(B, S, W_QN)
    ksw3 = ksw.reshape(B, S, W_KS)
    gt3 = gt.reshape(B, S, LANES)
    H, dh, tq = NSA_HEADS, NSA_DH, TQ_NSA
    state = [pltpu.VMEM((H, 1, tq), F32), pltpu.VMEM((H, dh + ONES_ROWS, tq), F32)]
    return pl.pallas_call(
        _nsa_attn_kernel,
        grid=(B, S // tq),
        in_specs=[pl.BlockSpec((1, tq, W_QN), lambda b, i: (b, i, 0)),
                  pl.BlockSpec((1, nc, G * LANES), lambda b, i: (b, 0, 0)),
                  pl.BlockSpec((1, G * dh, nc), lambda b, i: (b, 0, 0)),
                  pl.BlockSpec((1, S, G * LANES), lambda b, i: (b, 0, 0)),
                  pl.BlockSpec((1, S, G * LANES), lambda b, i: (b, 0, 1)),
                  pl.BlockSpec((1, nkt, G * dh, TKV), lambda b, i: (b, 0, 0, 0)),
                  pl.BlockSpec((1, nkt, G * dh, TKV), lambda b, i: (b, 0, 1, 0)),
                  pl.BlockSpec((1, tq, LANES), lambda b, i: (b, i, 0)),
                  _const_spec(ovt), _const_spec(oh)],
        out_specs=pl.BlockSpec((1, tq, H * dh), lambda b, i: (b, i, 0)),
        out_shape=jax.ShapeDtypeStruct((B, S, H * dh), BF16),
        scratch_shapes=([pltpu.VMEM((H, dh, tq), F32)] + state + state
                        + [pltpu.VMEM((4, H, TKV, tq), F32), pltpu.VMEM((4, H, 1, tq), F32),
                           pltpu.VMEM((H, tq, 2 * LANES), BF16)]),
        compiler_params=_cparams(("parallel", "arbitrary")),
        name="nsa_attn",
    )(qn3, kc, vct, ksw3, ksw3, vt, vt, gt3, ovt, oh)


def _outproj_kernel(h_ref, a_ref, b_ref, w_ref, o_ref):
    ka = a_ref.shape[1]
    o_ref[...] = h_ref[...] + _dot(a_ref[...], w_ref[0, 0:ka, :]) + _dot(b_ref[...], w_ref[0, ka:, :])


def _outproj(h, oa, ob, w, l):
    M = h.shape[0]
    row = lambda width: pl.BlockSpec((TM, width), lambda i: (i, 0))
    return pl.pallas_call(
        _outproj_kernel,
        grid=(M // TM,),
        in_specs=[row(D_MODEL), row(oa.shape[1]), row(ob.shape[1]), _layer_spec(w, l)],
        out_specs=row(D_MODEL),
        out_shape=jax.ShapeDtypeStruct((M, D_MODEL), F32),
        compiler_params=_cparams(("parallel",)),
        name="attn_outproj",
    )(h, oa, ob, w)


FF_CHUNK = 256
HALO = 16


def _ffn_kernel(x_ref, xp_ref, g_ref, wup_ref, cw_ref, cb_ref, wdn_ref, fg_ref, o_ref, n_ref, h_ref, act_ref,
                *, final, spt):
    i = pl.program_id(0)
    n_ref[0:HALO, :] = _rms(xp_ref[...], g_ref[0]).astype(BF16)
    n_ref[HALO:HALO + TM, :] = _rms(x_ref[...], g_ref[0]).astype(BF16)
    keep = jnp.where(i % spt == 0, 0.0, 1.0)

    def conv(slot, col):
        hx = _dot(n_ref[...], wup_ref[0, :, col:col + FF_CHUNK])
        h_ref[slot, 0:HALO, :] = hx[0:HALO, :] * keep
        h_ref[slot, HALO:HALO + TM, :] = hx[HALO:HALO + TM, :]
        out = cb_ref[0, :, col:col + FF_CHUNK]
        for j in range(CONV_W):
            lo = HALO - (CONV_W - 1) + j
            out = out + h_ref[slot, lo:lo + TM, :] * cw_ref[0, j:j + 1, col:col + FF_CHUNK]
        return out

    for c in range(D_FF // FF_CHUNK):
        col = c * FF_CHUNK
        gate = conv(0, col)
        up = conv(1, D_FF + col)
        act_ref[:, col:col + FF_CHUNK] = (gate * (1.0 / (1.0 + jnp.exp(-gate))) * up).astype(BF16)
    out = x_ref[...] + _dot(act_ref[...], wdn_ref[0])
    if final:
        out = _rms(out, fg_ref[...])
    o_ref[...] = out


def _ffn(h, g, wup, cw, cb, wdn, fg, l, S, final):
    M = h.shape[0]
    spt = S // TM
    row = pl.BlockSpec((TM, D_MODEL), lambda i: (i, 0))
    prev = pl.BlockSpec((HALO, D_MODEL), lambda i: (jnp.maximum(i * (TM // HALO) - 1, 0), 0))
    return pl.pallas_call(
        functools.partial(_ffn_kernel, final=final, spt=spt),
        grid=(M // TM,),
        in_specs=[row, prev, _layer_spec(g, l), _layer_spec(wup, l), _layer_spec(cw, l), _layer_spec(cb, l),
                  _layer_spec(wdn, l), _const_spec(fg)],
        out_specs=row,
        out_shape=jax.ShapeDtypeStruct((M, D_MODEL), F32),
        scratch_shapes=[pltpu.VMEM((HALO + TM, D_MODEL), BF16), pltpu.VMEM((2, HALO + TM, FF_CHUNK), F32),
                        pltpu.VMEM((TM, D_FF), BF16)],
        compiler_params=_cparams(("parallel",)),
        name="conv_ffn",
    )(h, h, g, wup, cw, cb, wdn, fg)


def _gather_cols(w, idx, scale=None):
    idx = np.asarray(idx)
    mult = (idx >= 0).astype(np.float32) if scale is None else np.where(idx >= 0, scale, 0.0).astype(np.float32)
    return jnp.take(w, jnp.asarray(np.maximum(idx, 0)), axis=-1) * jnp.asarray(mult)


def _slab_idx(n_slabs, width, base_fn):
    idx = -np.ones((n_slabs, LANES), np.int64)
    for s in range(n_slabs):
        idx[s, :width] = base_fn(s) + np.arange(width)
    return idx.reshape(-1)


def _inproj_weights(w_in):
    G, dh = NSA_KV_HEADS, NSA_DH
    ia = -np.ones(W_ZA, np.int64)
    ia[0:MLA_Q_RANK] = OFF_CQ + np.arange(MLA_Q_RANK)
    ia[MLA_Q_RANK:MLA_Q_RANK + MLA_KV_RANK] = OFF_CKV + np.arange(MLA_KV_RANK)
    ia[5 * LANES + MLA_NOPE:5 * LANES + MLA_NOPE + MLA_ROPE] = OFF_KROPE + np.arange(MLA_ROPE)
    igt = -np.ones(LANES, np.int64)
    for g in range(G):
        for r in range(NSA_REP):
            for c in range(3):
                igt[g * 16 + r * 4 + c] = OFF_GT + (g * NSA_REP + r) * 3 + c
    ib = np.concatenate([igt, OFF_KC + np.arange(G * dh), OFF_VC + np.arange(G * dh)])
    iq = _slab_idx(NSA_HEADS, dh, lambda h: OFF_QN + h * dh)
    ik = np.concatenate([_slab_idx(G, dh, lambda g: OFF_KS + g * dh), _slab_idx(G, dh, lambda g: OFF_KW + g * dh)])
    iv = np.concatenate([OFF_VS + np.arange(G * dh), OFF_VW + np.arange(G * dh)])
    idx = np.concatenate([ia, ib, iq, ik])
    scale = np.ones(W_ROW, np.float32)
    scale[W_ZA + W_ZB:W_ZA + W_ZB + W_QN] = NSA_DH ** -0.5 * LOG2E
    w = _gather_cols(w_in, idx, scale).astype(BF16)
    wvt = jnp.swapaxes(jnp.take(w_in, jnp.asarray(iv), axis=-1), 1, 2).astype(BF16)
    return w, wvt


def _bf16_terms(x, n):
    terms = []
    for _ in range(n):
        bits = np.array(x, np.float32).view(np.uint32)
        bits = (bits + np.uint32(0x7FFF) + ((bits >> np.uint32(16)) & np.uint32(1))) & np.uint32(0xFFFF0000)
        terms.append(float(bits.view(np.float32)))
        x = x - terms[-1]
    return terms


N_POS_TERMS = 3


def _pos_features(pos, n_slabs):
    out = np.zeros((len(pos), n_slabs * LANES), np.float32)
    for s in range(n_slabs):
        base = s * LANES + NSA_DH
        out[:, base:base + N_POS_TERMS] = (pos // L_SEL)[:, None]
        out[:, base + N_POS_TERMS:base + 2 * N_POS_TERMS] = (pos % L_SEL)[:, None]
    return out


def _const_tables(S):
    slopes = 2.0 ** (-8.0 * np.arange(1, NSA_HEADS + 1) / NSA_HEADS)
    terms = np.array(_bf16_terms(LOG2E, N_POS_TERMS))
    qaug = np.zeros((1, W_QN), np.float32)
    for h in range(NSA_HEADS):
        base = h * LANES + NSA_DH
        qaug[0, base:base + N_POS_TERMS] = slopes[h] * L_SEL * terms
        qaug[0, base + N_POS_TERMS:base + 2 * N_POS_TERMS] = slopes[h] * terms
    posf = _pos_features(np.arange(S), 4)
    nc = S // D_CMP
    feat = _pos_features(np.arange(nc) * D_CMP + L_CMP - 1, 2)
    n_cmp = (S - L_CMP) // D_CMP + 1
    nblk = S // L_SEL
    cs = np.arange(n_cmp) * D_CMP
    ss = np.arange(nblk) * L_SEL
    ov = ((cs[:, None] < ss[None, :] + L_SEL) & (cs[:, None] + L_CMP > ss[None, :])).astype(np.float32)
    ovt = np.zeros((nblk, nc), np.float32)
    ovt[:, :n_cmp] = ov.T
    assert nblk <= LANES
    onehot = (np.arange(S)[:, None] // L_SEL == np.arange(LANES)[None, :]).astype(np.float32)
    return (jnp.asarray(qaug), jnp.asarray(posf), jnp.asarray(feat), jnp.asarray(ovt, BF16),
            jnp.asarray(onehot, BF16))


def _rope_consts(S):
    half = MLA_ROPE // 2
    inv = (1.0 / (np.float32(ROPE_THETA) ** (np.arange(0, MLA_ROPE, 2, dtype=np.float32) / MLA_ROPE))).astype(np.float32)
    ang = np.arange(S, dtype=np.float32)[:, None] * inv[None, :]
    cos, sin = np.cos(ang), np.sin(ang)
    z = lambda w: np.zeros((S, w), np.float32)
    ct = np.concatenate([np.ones((S, MLA_NOPE), np.float32), cos, cos, z(LANES - MLA_QK)], axis=1)
    sa = np.concatenate([z(MLA_NOPE), -sin, z(LANES - MLA_NOPE - half)], axis=1)
    sb = np.concatenate([z(MLA_NOPE + half), sin, z(LANES - MLA_QK)], axis=1)
    return jnp.asarray(ct), jnp.asarray(sa), jnp.asarray(sb)


def _mla_weights(w_uq, w_ukv):
    iq = _slab_idx(MLA_HEADS, MLA_QK, lambda h: h * MLA_QK)
    wq = _gather_cols(w_uq, iq, MLA_QK ** -0.5 * LOG2E).astype(BF16)
    ik = _slab_idx(MLA_HEADS, MLA_NOPE, lambda h: h * (MLA_NOPE + MLA_V))
    wk = _gather_cols(w_ukv, ik).astype(BF16)
    iv = np.concatenate([h * (MLA_NOPE + MLA_V) + MLA_NOPE + np.arange(MLA_V) for h in range(MLA_HEADS)])
    wvt = jnp.swapaxes(jnp.take(w_ukv, jnp.asarray(iv), axis=-1), 1, 2).astype(BF16)
    return wq, wk, wvt


def _compress_weights(pos, w1, w2k, w2v):
    G, dh, half = NSA_KV_HEADS, NSA_DH, L_CMP // 2
    L = pos.shape[0]
    eye = jnp.eye(G, dtype=F32)

    def first_layer(w):
        return jnp.einsum('nclde,gk->nclgdke', w, eye).reshape(L, 2, half * G * dh, G * CMP_HID).astype(BF16)

    def pos_row(p):
        return jnp.broadcast_to(p[:, :, :, None, :], (L, 2, half, G, dh)).reshape(L, 2, 1, half * G * dh)

    w2kp = jnp.pad(w2k, ((0, 0), (0, 0), (0, LANES - dh)))
    w2k_bd = jnp.einsum('ned,gk->ngekd', w2kp, eye).reshape(L, G * CMP_HID, G * LANES).astype(BF16)
    w2vt_bd = jnp.einsum('ned,gk->ngdke', w2v, eye).reshape(L, G * dh, G * CMP_HID).astype(BF16)
    return (pos_row(pos[:, :, :half]), pos_row(pos[:, :, half:]), first_layer(w1[:, :, :half]),
            first_layer(w1[:, :, half:]), w2k_bd, w2vt_bd)


def kernel(x, attn_norm, w_in, q_norm, kv_norm, w_uq, w_ukv, cmp_pos_k, cmp_pos_v, cmp_k_w1, cmp_k_w2, cmp_v_w1,
           cmp_v_w2, w_o, ffn_norm, w_up, conv_w, conv_b, w_down, final_norm):
    B, S, D = x.shape
    M = B * S
    assert D == D_MODEL and S % TM == 0 and TQ_MLA == TKV and TQ_NSA == TKV and S % TKV == 0
    qaug, posf, feat, ovt, onehot = _const_tables(S)
    ct, sa, sb = _rope_consts(S)
    nc = S // D_CMP

    w_row, w_vt = _inproj_weights(w_in)
    mq, mk, mvt = _mla_weights(w_uq, w_ukv)
    pa, pb, cwa, cwb, w2k, w2vt = _compress_weights(jnp.stack([cmp_pos_k, cmp_pos_v], axis=1),
                                                    jnp.stack([cmp_k_w1, cmp_v_w1], axis=1), cmp_k_w2, cmp_v_w2)
    wo, wup, wdn = w_o.astype(BF16), w_up.astype(BF16), w_down.astype(BF16)
    g_attn, g_q, g_kv, g_ffn = (a[:, None, :] for a in (attn_norm, q_norm, kv_norm, ffn_norm))
    cb = conv_b[:, None, :]
    fg = final_norm[None, :]

    h = x.reshape(M, D)
    for l in range(DEPTH):
        za, gt, kcr, vcr, qn, ksw, vt = _inproj(h, g_attn, w_row, w_vt, qaug, posf, l, B, S)
        q, k, vmt = _mla_prep(za, g_q, g_kv, mq, mk, mvt, ct, sa, sb, l, B, S)
        o_mla = _mla_attn(q.reshape(B, S, -1), k.reshape(B, S, -1), vmt, B, S)
        kc, vct = _compress(kcr.reshape(B, nc, -1), vcr.reshape(B, nc, -1), pa, pb, cwa, cwb, w2k, w2vt, feat, l)
        o_nsa = _nsa_attn(qn, kc, vct, ksw, vt, gt, ovt, onehot, B, S)
        h = _outproj(h, o_mla.reshape(M, -1), o_nsa.reshape(M, -1), wo, l)
        h = _ffn(h, g_ffn, wup, conv_w, cb, wdn, fg, l, S, final=(l == DEPTH - 1))
    return h.reshape(B, S, D)
```

```python
import functools

import numpy as np
import jax
import jax.numpy as jnp
from jax import lax
from jax.experimental import pallas as pl
from jax.experimental.pallas import tpu as pltpu

F32 = jnp.float32
BF16 = jnp.bfloat16

D_MODEL = 1024
DEPTH = 2
EPS = 1e-6
NEG_INF = -1e30
LOG2E = float(np.log2(np.e))
LANES = 128

MLA_HEADS = 8
MLA_NOPE = 64
MLA_ROPE = 32
MLA_V = 64
MLA_Q_RANK = 384
MLA_KV_RANK = 256
MLA_QK = MLA_NOPE + MLA_ROPE
ROPE_THETA = 10000.0

NSA_HEADS = 8
NSA_KV_HEADS = 2
NSA_REP = NSA_HEADS // NSA_KV_HEADS
NSA_DH = 64
L_CMP = 32
D_CMP = 16
CMP_HID = 128
L_SEL = 64
N_SEL = 16
W_WIN = 512

D_FF = 2816
CONV_W = 3

_SIZES = [MLA_Q_RANK, MLA_KV_RANK, MLA_ROPE, NSA_HEADS * NSA_DH] + [NSA_KV_HEADS * NSA_DH] * 6 + [3 * NSA_HEADS]
_OFFS = [0] + [int(o) for o in np.cumsum(_SIZES)]
(OFF_CQ, OFF_CKV, OFF_KROPE, OFF_QN, OFF_KC, OFF_VC, OFF_KS, OFF_VS, OFF_KW, OFF_VW, OFF_GT, D_IN) = _OFFS

TM = 512
TKV = 256
TQ_MLA = 256
TQ_NSA = 256
VMEM_LIMIT = 56 * 1024 * 1024

W_ZA = 6 * LANES
W_ZB = 3 * LANES
W_KS = 4 * LANES
W_ROW = W_ZA + W_ZB + W_KS
W_QN = NSA_HEADS * LANES


def _cparams(sem):
    return pltpu.CompilerParams(dimension_semantics=sem, vmem_limit_bytes=VMEM_LIMIT)


def _layer_spec(a, l):
    return pl.BlockSpec((1,) + a.shape[1:], lambda *_: (l,) + (0,) * (a.ndim - 1))


def _const_spec(a):
    return pl.BlockSpec(a.shape, lambda *_: (0,) * a.ndim)


def _dot(a, b):
    return jnp.dot(a, b, preferred_element_type=F32)


def _dot_nt(a, b):
    return lax.dot_general(a, b, (((1,), (1,)), ((), ())), preferred_element_type=F32)


def _rms(x, g):
    return x * lax.rsqrt(jnp.mean(x * x, axis=-1, keepdims=True) + EPS) * g


def _store_tiles(ref, xt):
    for c in range(TM // TKV):
        ref[0, c] = xt[:, c * TKV:(c + 1) * TKV]


def _inproj_kernel(x_ref, g_ref, w_ref, wqt_ref, wvt_ref, qaug_ref, posf_ref,
                   za_ref, gt_ref, kcr_ref, vcr_ref, qnt_ref, ksw_ref, vt_ref):
    n = _rms(x_ref[...], g_ref[0]).astype(BF16)
    c0, c1 = W_ZA, W_ZA + W_ZB
    za_ref[...] = _dot(n, w_ref[0, :, 0:c0])
    zb = _dot(n, w_ref[0, :, c0:c1])
    gt_ref[...] = zb[:, 0:LANES]
    kcr_ref[...] = zb[:, LANES:2 * LANES]
    vcr_ref[...] = zb[:, 2 * LANES:3 * LANES]
    ksw_ref[...] = (_dot(n, w_ref[0, :, c1:W_ROW]) + posf_ref[...]).astype(BF16)
    _store_tiles(qnt_ref, (_dot_nt(wqt_ref[0], n) + qaug_ref[...]).astype(BF16))
    _store_tiles(vt_ref, _dot_nt(wvt_ref[0], n).astype(BF16))


def _inproj(h, g, w, wqt, wvt, qaug, posf, l, B, S):
    M = B * S
    spt = S // TM
    row = lambda width: pl.BlockSpec((TM, width), lambda i: (i, 0))
    tiles = lambda rows: pl.BlockSpec((1, TM // TKV, rows, TKV), lambda i: (i // spt, i % spt, 0, 0))
    return pl.pallas_call(
        _inproj_kernel,
        grid=(M // TM,),
        in_specs=[row(D_MODEL), _layer_spec(g, l), _layer_spec(w, l), _layer_spec(wqt, l), _layer_spec(wvt, l),
                  _const_spec(qaug), pl.BlockSpec((TM, W_KS), lambda i: (i % spt, 0))],
        out_specs=[row(W_ZA), row(LANES), row(LANES), row(LANES), tiles(W_QN), row(W_KS), tiles(4 * NSA_DH)],
        out_shape=[jax.ShapeDtypeStruct((M, W_ZA), F32),
                   jax.ShapeDtypeStruct((M, LANES), F32),
                   jax.ShapeDtypeStruct((M, LANES), F32),
                   jax.ShapeDtypeStruct((M, LANES), F32),
                   jax.ShapeDtypeStruct((B, S // TKV, W_QN, TKV), BF16),
                   jax.ShapeDtypeStruct((M, W_KS), BF16),
                   jax.ShapeDtypeStruct((B, S // TKV, 4 * NSA_DH, TKV), BF16)],
        compiler_params=_cparams(("parallel",)),
        name="attn_inproj",
    )(h, g, w, wqt, wvt, qaug, posf)


def _rope_slab(x, ct, sa, sb):
    return x * ct + pltpu.roll(x, LANES - MLA_ROPE // 2, 1) * sa + pltpu.roll(x, MLA_ROPE // 2, 1) * sb


def _mla_prep_kernel(za_ref, qg_ref, kg_ref, wqt_ref, wk_ref, wvt_ref, ct_ref, sa_ref, sb_ref, cost_ref, sint_ref,
                     qt_ref, k_ref, vt_ref):
    za = za_ref[...]
    cq = _rms(za[:, 0:MLA_Q_RANK], qg_ref[0]).astype(BF16)
    ckv = _rms(za[:, MLA_Q_RANK:MLA_Q_RANK + MLA_KV_RANK], kg_ref[0]).astype(BF16)
    kr = _rope_slab(za[:, 5 * LANES:6 * LANES], ct_ref[...], sa_ref[...], sb_ref[...])
    k = _dot(ckv, wk_ref[0])
    for h in range(MLA_HEADS):
        sl = slice(h * LANES, (h + 1) * LANES)
        k_ref[:, sl] = (k[:, sl] + kr).astype(BF16)
    qt = _dot_nt(wqt_ref[0], cq)
    cos, sin = cost_ref[...], sint_ref[...]
    half = MLA_ROPE // 2
    slabs = []
    for h in range(MLA_HEADS):
        base = h * LANES
        x1 = qt[base + MLA_NOPE:base + MLA_NOPE + half, :]
        x2 = qt[base + MLA_NOPE + half:base + MLA_QK, :]
        slabs += [qt[base:base + MLA_NOPE, :], x1 * cos - x2 * sin, x2 * cos + x1 * sin,
                  qt[base + MLA_QK:base + LANES, :]]
    _store_tiles(qt_ref, jnp.concatenate(slabs, axis=0).astype(BF16))
    _store_tiles(vt_ref, _dot_nt(wvt_ref[0], ckv).astype(BF16))


def _mla_prep(za, qg, kg, wqt, wk, wvt, rope, l, B, S):
    M = B * S
    spt = S // TM
    ct, sa, sb, cost, sint = rope
    row = lambda width: pl.BlockSpec((TM, width), lambda i: (i, 0))
    tab = pl.BlockSpec((TM, LANES), lambda i: (i % spt, 0))
    tabt = pl.BlockSpec((MLA_ROPE // 2, TM), lambda i: (0, i % spt))
    tiles = lambda rows: pl.BlockSpec((1, TM // TKV, rows, TKV), lambda i: (i // spt, i % spt, 0, 0))
    return pl.pallas_call(
        _mla_prep_kernel,
        grid=(M // TM,),
        in_specs=[row(W_ZA), _layer_spec(qg, l), _layer_spec(kg, l), _layer_spec(wqt, l), _layer_spec(wk, l),
                  _layer_spec(wvt, l), tab, tab, tab, tabt, tabt],
        out_specs=[tiles(MLA_HEADS * LANES), row(MLA_HEADS * LANES), tiles(MLA_HEADS * MLA_V)],
        out_shape=[jax.ShapeDtypeStruct((B, S // TKV, MLA_HEADS * LANES, TKV), BF16),
                   jax.ShapeDtypeStruct((M, MLA_HEADS * LANES), BF16),
                   jax.ShapeDtypeStruct((B, S // TKV, MLA_HEADS * MLA_V, TKV), BF16)],
        compiler_params=_cparams(("parallel",)),
        name="mla_prep",
    )(za, qg, kg, wqt, wk, wvt, ct, sa, sb, cost, sint)


ONES_ROWS = 16


def _stage_scores(s, s_ref, mx_ref, b, h):
    mx_ref[b, h] = jnp.max(s, axis=0, keepdims=True)
    s_ref[b, h, 0:s.shape[0], :] = s


def _online_update(vt, s_ref, mx_ref, b, m_ref, acc_ref, h):
    m_old = m_ref[h]
    m_new = jnp.maximum(m_old, mx_ref[b, h])
    p = jnp.exp2(s_ref[b, h] - m_new).astype(BF16)
    vt1 = jnp.concatenate([vt, jnp.ones((ONES_ROWS, vt.shape[1]), BF16)], axis=0)
    acc_ref[h] = jnp.exp2(m_old - m_new) * acc_ref[h] + _dot(vt1, p)
    m_ref[h] = m_new


def _normalised(acc_ref, h, dh):
    return acc_ref[h, 0:dh, :] / acc_ref[h, dh:dh + 1, :]


def _pipelined_sweep(i, n_heads, score, update):
    every = range(n_heads)

    @pl.when(i >= 1)
    def _():
        score(0, 0, every)

    @pl.loop(0, i // 2)
    def _(j):
        for h in every:
            score(2 * j + 1, 1, [h])
            update(2 * j, 0, [h])
        for h in every:
            score(2 * j + 2, 0, [h])
            update(2 * j + 1, 1, [h])

    @pl.when(i % 2 == 1)
    def _():
        update(i - 1, 0, every)


def _mla_attn_kernel(qt_ref, k_ref, vt_ref, o_ref, m_ref, acc_ref, s_ref, mx_ref):
    i = pl.program_id(1)
    tq = TQ_MLA
    m_ref[...] = jnp.full(m_ref.shape, -jnp.inf, F32)
    acc_ref[...] = jnp.zeros(acc_ref.shape, F32)

    every = range(MLA_HEADS)

    def score(kt, b, heads, keep=None):
        for h in heads:
            sl = slice(h * LANES, (h + 1) * LANES)
            k = k_ref[0, pl.ds(pl.multiple_of(kt * TKV, TKV), TKV), sl]
            s = _dot(k, qt_ref[0, 0, sl, :])
            if keep is not None:
                s = jnp.where(keep, s, NEG_INF)
            _stage_scores(s, s_ref, mx_ref, b, h)

    def update(kt, b, heads):
        for h in heads:
            _online_update(vt_ref[0, kt, h * MLA_V:(h + 1) * MLA_V, :], s_ref, mx_ref, b, m_ref, acc_ref, h)

    _pipelined_sweep(i, MLA_HEADS, score, update)
    score(i, 0, every,
          lax.broadcasted_iota(jnp.int32, (TKV, tq), 0) <= lax.broadcasted_iota(jnp.int32, (TKV, tq), 1))
    update(i, 0, every)
    out = jnp.concatenate([_normalised(acc_ref, h, MLA_V) for h in range(MLA_HEADS)], axis=0)
    o_ref[0] = out.T.astype(BF16)


def _mla_attn(q, k, vt, B, S):
    nkt = S // TKV
    hw = MLA_HEADS * LANES
    return pl.pallas_call(
        _mla_attn_kernel,
        grid=(B, S // TQ_MLA),
        in_specs=[pl.BlockSpec((1, 1, hw, TQ_MLA), lambda b, i: (b, i, 0, 0)),
                  pl.BlockSpec((1, S, hw), lambda b, i: (b, 0, 0)),
                  pl.BlockSpec((1, nkt, MLA_HEADS * MLA_V, TKV), lambda b, i: (b, 0, 0, 0))],
        out_specs=pl.BlockSpec((1, TQ_MLA, MLA_HEADS * MLA_V), lambda b, i: (b, i, 0)),
        out_shape=jax.ShapeDtypeStruct((B, S, MLA_HEADS * MLA_V), BF16),
        scratch_shapes=[pltpu.VMEM((MLA_HEADS, 1, TQ_MLA), F32),
                        pltpu.VMEM((MLA_HEADS, MLA_V + ONES_ROWS, TQ_MLA), F32),
                        pltpu.VMEM((2, MLA_HEADS, TKV, TQ_MLA), F32), pltpu.VMEM((2, MLA_HEADS, 1, TQ_MLA), F32)],
        compiler_params=_cparams(("parallel", "arbitrary")),
        name="mla_attn",
    )(q, k, vt)


def _gelu_tanh(x):
    return 0.5 * x * (1.0 + jnp.tanh(np.sqrt(2.0 / np.pi).astype(np.float32) * (x + 0.044715 * (x * x * x))))


def _compress_kernel(kx_ref, vx_ref, pa_ref, pb_ref, wa_ref, wb_ref, w2k_ref, w2vt_ref, feat_ref, kc_ref, vct_ref):
    nc = kx_ref.shape[1]

    def hidden(x, kv):
        ya = _dot((x + pa_ref[0, kv]).astype(BF16), wa_ref[0, kv])
        yb = _dot((x + pb_ref[0, kv]).astype(BF16), wb_ref[0, kv])
        return _gelu_tanh(ya + pltpu.roll(yb, nc - 1, 0)).astype(BF16)

    kc_ref[0] = (_dot(hidden(kx_ref[0], 0), w2k_ref[0]) + feat_ref[...]).astype(BF16)
    vct_ref[0] = _dot_nt(w2vt_ref[0], hidden(vx_ref[0], 1)).astype(BF16)


def _compress(kx, vx, pa, pb, wa, wb, w2k, w2vt, feat, l):
    B, nc, width = kx.shape
    xs = pl.BlockSpec((1, nc, width), lambda b: (b, 0, 0))
    return pl.pallas_call(
        _compress_kernel,
        grid=(B,),
        in_specs=[xs, xs] + [_layer_spec(a, l) for a in (pa, pb, wa, wb, w2k, w2vt)] + [_const_spec(feat)],
        out_specs=[pl.BlockSpec((1, nc, 2 * LANES), lambda b: (b, 0, 0)),
                   pl.BlockSpec((1, 2 * NSA_DH, nc), lambda b: (b, 0, 0))],
        out_shape=[jax.ShapeDtypeStruct((B, nc, 2 * LANES), BF16),
                   jax.ShapeDtypeStruct((B, 2 * NSA_DH, nc), BF16)],
        compiler_params=_cparams(("parallel",)),
        name="nsa_compress",
    )(kx, vx, pa, pb, wa, wb, w2k, w2vt, feat)


def _nsa_attn_kernel(qnt_ref, kc_ref, vct_ref, ks_ref, kw_ref, vst_ref, vwt_ref, gt_ref, ovt_ref, oh_ref,
                     o_ref, ocmp_ref, m_ref, acc_ref, mw_ref, accw_ref, s_ref, mx_ref, qs_ref):
    i = pl.program_id(1)
    tq = TQ_NSA
    G, R, dh = NSA_KV_HEADS, NSA_REP, NSA_DH
    nc = kc_ref.shape[1]
    nblk = ovt_ref.shape[0]
    t0 = i * tq
    tpos = t0 + lax.broadcasted_iota(jnp.int32, (1, tq), 1)
    q_of = lambda h: qnt_ref[0, 0, h * LANES:(h + 1) * LANES, :]

    for ref in (m_ref, mw_ref):
        ref[...] = jnp.full(ref.shape, -jnp.inf, F32)
    for ref in (acc_ref, accw_ref):
        ref[...] = jnp.zeros(ref.shape, F32)

    nrow = lax.broadcasted_iota(jnp.int32, (nc, 1), 0)
    bias_c = jnp.where((tpos >= nrow * D_CMP + (L_CMP - 1)) & (nrow < nc - 1), 0.0, NEG_INF)
    has_cmp = tpos >= L_CMP - 1
    for h in range(G * R):
        g = h // R
        _stage_scores(_dot(kc_ref[0, :, g * LANES:(g + 1) * LANES], q_of(h)) + bias_c, s_ref, mx_ref, 0, h)
    imps = [None] * G
    for h in range(G * R):
        g = h // R
        e = jnp.exp2(s_ref[0, h, 0:nc, :] - mx_ref[0, h])
        inv = jnp.where(has_cmp, 1.0 / jnp.sum(e, axis=0, keepdims=True), 0.0)
        e_hi = e.astype(BF16)
        e_lo = (e - e_hi.astype(F32)).astype(BF16)
        both = _dot(jnp.concatenate([vct_ref[0, g * dh:(g + 1) * dh, :], ovt_ref[...]], axis=0), e_hi)
        ocmp_ref[h] = both[0:dh, :] * inv
        contrib = (both[dh:dh + nblk, :] + _dot(ovt_ref[...], e_lo)) * inv
        imps[g] = contrib if imps[g] is None else imps[g] + contrib

    cur = lax.shift_right_logical(tpos, 6)
    jrow = lax.broadcasted_iota(jnp.int32, (nblk, 1), 0)
    sub = lax.broadcasted_iota(jnp.int32, (8, 1), 0)
    n_slab = nblk // 8
    for g in range(G):
        imp = jnp.where(jrow > cur, -jnp.inf, imps[g])
        imp = jnp.where((jrow == 0) | (jrow == cur) | (jrow == cur - 1), jnp.inf, imp)
        slabs = [imp[8 * v:8 * v + 8, :] for v in range(n_slab)]
        ranks = [jnp.zeros((8, tq), F32) for _ in range(n_slab)]
        for jj in range(nblk):
            row = imp[jj:jj + 1, :]
            for v in range(n_slab):
                if v > jj // 8:
                    beats = row >= slabs[v]
                elif v < jj // 8:
                    beats = row > slabs[v]
                else:
                    beats = (row > slabs[v]) | ((row == slabs[v]) & (sub > jj % 8))
                ranks[v] = ranks[v] + jnp.where(beats, 1.0, 0.0)
        rank = jnp.concatenate(ranks, axis=0)
        mb = jnp.where(rank < float(min(N_SEL, nblk)), 0.0, NEG_INF)
        mb = jnp.concatenate([mb, jnp.zeros((LANES - nblk, tq), F32)], axis=0).astype(BF16)
        for r in range(R):
            qs_ref[g * R + r] = jnp.concatenate([q_of(g * R + r), mb], axis=0)

    rel = lax.broadcasted_iota(jnp.int32, (TKV, tq), 0) - lax.broadcasted_iota(jnp.int32, (TKV, tq), 1)

    every = range(G * R)

    def score(kt, b, heads, sel, keep=None):
        rows = pl.ds(pl.multiple_of(kt * TKV, TKV), TKV)
        for h in heads:
            g = h // R
            if sel:
                k = jnp.concatenate([ks_ref[0, rows, g * LANES:(g + 1) * LANES], oh_ref[rows, :]], axis=1)
            else:
                k = kw_ref[0, rows, g * LANES:(g + 1) * LANES]
            s = _dot(k, qs_ref[h] if sel else q_of(h))
            if keep is not None:
                s = jnp.where(keep, s, NEG_INF)
            _stage_scores(s, s_ref, mx_ref, b, h)

    def update(kt, b, heads, sel):
        vt_ref, state = (vst_ref, (m_ref, acc_ref)) if sel else (vwt_ref, (mw_ref, accw_ref))
        for h in heads:
            g = h // R
            _online_update(vt_ref[0, kt, g * dh:(g + 1) * dh, :], s_ref, mx_ref, b, *state, h)

    _pipelined_sweep(i, G * R, functools.partial(score, sel=True), functools.partial(update, sel=True))

    assert W_WIN == 2 * TKV
    causal, near = rel <= 0, rel > 0

    @pl.when(i >= 2)
    def _():
        score(i, 0, every, True, causal)
        score(i, 1, every, False, causal)
        score(i - 1, 2, every, False)
        score(i - 2, 3, every, False, near)
        update(i, 0, every, True)
        update(i, 1, every, False)
        update(i - 1, 2, every, False)
        update(i - 2, 3, every, False)

    @pl.when(i < 2)
    def _():
        score(i, 0, every, True, causal)
        score(i, 1, every, False, causal)
        update(i, 0, every, True)
        update(i, 1, every, False)

    @pl.when(i == 1)
    def _():
        score(0, 2, every, False)
        update(0, 2, every, False)

    sig = 1.0 / (1.0 + jnp.exp(-gt_ref[0].T[0:G * 16, :]))
    outs = []
    for h in range(G * R):
        row = (h // R) * 16 + (h % R) * 4
        outs.append(sig[row:row + 1, :] * ocmp_ref[h] + sig[row + 1:row + 2, :] * _normalised(acc_ref, h, dh)
                    + sig[row + 2:row + 3, :] * _normalised(accw_ref, h, dh))
    o_ref[0] = jnp.concatenate(outs, axis=0).T.astype(BF16)


def _nsa_attn(qnt, kc, vct, ksw, vt, gt, ovt, oh, B, S):
    nkt = S // TKV
    nc = kc.shape[1]
    G = NSA_KV_HEADS
    ksw3 = ksw.reshape(B, S, W_KS)
    gt3 = gt.reshape(B, S, LANES)
    H, dh, tq = NSA_HEADS, NSA_DH, TQ_NSA
    state = [pltpu.VMEM((H, 1, tq), F32), pltpu.VMEM((H, dh + ONES_ROWS, tq), F32)]
    return pl.pallas_call(
        _nsa_attn_kernel,
        grid=(B, S // tq),
        in_specs=[pl.BlockSpec((1, 1, W_QN, tq), lambda b, i: (b, i, 0, 0)),
                  pl.BlockSpec((1, nc, G * LANES), lambda b, i: (b, 0, 0)),
                  pl.BlockSpec((1, G * dh, nc), lambda b, i: (b, 0, 0)),
                  pl.BlockSpec((1, S, G * LANES), lambda b, i: (b, 0, 0)),
                  pl.BlockSpec((1, S, G * LANES), lambda b, i: (b, 0, 1)),
                  pl.BlockSpec((1, nkt, G * dh, TKV), lambda b, i: (b, 0, 0, 0)),
                  pl.BlockSpec((1, nkt, G * dh, TKV), lambda b, i: (b, 0, 1, 0)),
                  pl.BlockSpec((1, tq, LANES), lambda b, i: (b, i, 0)),
                  _const_spec(ovt), _const_spec(oh)],
        out_specs=pl.BlockSpec((1, tq, H * dh), lambda b, i: (b, i, 0)),
        out_shape=jax.ShapeDtypeStruct((B, S, H * dh), BF16),
        scratch_shapes=([pltpu.VMEM((H, dh, tq), F32)] + state + state
                        + [pltpu.VMEM((4, H, TKV, tq), F32), pltpu.VMEM((4, H, 1, tq), F32),
                           pltpu.VMEM((H, 2 * LANES, tq), BF16)]),
        compiler_params=_cparams(("parallel", "arbitrary")),
        name="nsa_attn",
    )(qnt, kc, vct, ksw3, ksw3, vt, vt, gt3, ovt, oh)


def _outproj_kernel(h_ref, a_ref, b_ref, w_ref, o_ref):
    ka = a_ref.shape[1]
    o_ref[...] = h_ref[...] + _dot(a_ref[...], w_ref[0, 0:ka, :]) + _dot(b_ref[...], w_ref[0, ka:, :])


def _outproj(h, oa, ob, w, l):
    M = h.shape[0]
    row = lambda width: pl.BlockSpec((TM, width), lambda i: (i, 0))
    return pl.pallas_call(
        _outproj_kernel,
        grid=(M // TM,),
        in_specs=[row(D_MODEL), row(oa.shape[1]), row(ob.shape[1]), _layer_spec(w, l)],
        out_specs=row(D_MODEL),
        out_shape=jax.ShapeDtypeStruct((M, D_MODEL), F32),
        compiler_params=_cparams(("parallel",)),
        name="attn_outproj",
    )(h, oa, ob, w)


FF_CHUNK = 256
HALO = 16


def _ffn_kernel(x_ref, xp_ref, g_ref, wup_ref, cw_ref, cb_ref, wdn_ref, fg_ref, o_ref, n_ref, h_ref, act_ref,
                *, final, spt):
    i = pl.program_id(0)
    n_ref[0:HALO, :] = _rms(xp_ref[...], g_ref[0]).astype(BF16)
    n_ref[HALO:HALO + TM, :] = _rms(x_ref[...], g_ref[0]).astype(BF16)
    keep = jnp.where(i % spt == 0, 0.0, 1.0)

    def conv(slot, col):
        hx = _dot(n_ref[...], wup_ref[0, :, col:col + FF_CHUNK])
        h_ref[slot, 0:HALO, :] = hx[0:HALO, :] * keep
        h_ref[slot, HALO:HALO + TM, :] = hx[HALO:HALO + TM, :]
        out = cb_ref[0, :, col:col + FF_CHUNK]
        for j in range(CONV_W):
            lo = HALO - (CONV_W - 1) + j
            out = out + h_ref[slot, lo:lo + TM, :] * cw_ref[0, j:j + 1, col:col + FF_CHUNK]
        return out

    for c in range(D_FF // FF_CHUNK):
        col = c * FF_CHUNK
        gate = conv(0, col)
        up = conv(1, D_FF + col)
        act_ref[:, col:col + FF_CHUNK] = (gate * (1.0 / (1.0 + jnp.exp(-gate))) * up).astype(BF16)
    out = x_ref[...] + _dot(act_ref[...], wdn_ref[0])
    if final:
        out = _rms(out, fg_ref[...])
    o_ref[...] = out


def _ffn(h, g, wup, cw, cb, wdn, fg, l, S, final):
    M = h.shape[0]
    spt = S // TM
    row = pl.BlockSpec((TM, D_MODEL), lambda i: (i, 0))
    prev = pl.BlockSpec((HALO, D_MODEL), lambda i: (jnp.maximum(i * (TM // HALO) - 1, 0), 0))
    return pl.pallas_call(
        functools.partial(_ffn_kernel, final=final, spt=spt),
        grid=(M // TM,),
        in_specs=[row, prev, _layer_spec(g, l), _layer_spec(wup, l), _layer_spec(cw, l), _layer_spec(cb, l),
                  _layer_spec(wdn, l), _const_spec(fg)],
        out_specs=row,
        out_shape=jax.ShapeDtypeStruct((M, D_MODEL), F32),
        scratch_shapes=[pltpu.VMEM((HALO + TM, D_MODEL), BF16), pltpu.VMEM((2, HALO + TM, FF_CHUNK), F32),
                        pltpu.VMEM((TM, D_FF), BF16)],
        compiler_params=_cparams(("parallel",)),
        name="conv_ffn",
    )(h, h, g, wup, cw, cb, wdn, fg)


def _slabs(w, n, width, pad_to=LANES):
    lead = w.shape[:-1]
    w = w.reshape(lead + (n, width))
    w = jnp.pad(w, [(0, 0)] * (w.ndim - 1) + [(0, pad_to - width)])
    return w.reshape(lead + (n * pad_to,))


def _inproj_weights(w_in):
    G, dh = NSA_KV_HEADS, NSA_DH
    cols = lambda off, n: w_in[..., off:off + n]
    zeros = lambda n: jnp.zeros(w_in.shape[:-1] + (n,), w_in.dtype)
    kr = jnp.concatenate([zeros(MLA_NOPE), cols(OFF_KROPE, MLA_ROPE), zeros(LANES - MLA_QK)], axis=-1)
    gates = _slabs(cols(OFF_GT, 3 * NSA_HEADS), NSA_HEADS, 3, 4)
    gates = jnp.concatenate([gates, zeros(LANES - 4 * NSA_HEADS)], axis=-1)
    w = jnp.concatenate([cols(OFF_CQ, MLA_Q_RANK), cols(OFF_CKV, MLA_KV_RANK), kr,
                         gates, cols(OFF_KC, G * dh), cols(OFF_VC, G * dh),
                         _slabs(cols(OFF_KS, G * dh), G, dh), _slabs(cols(OFF_KW, G * dh), G, dh)], axis=-1)
    assert w.shape[-1] == W_ROW
    qn = _slabs(cols(OFF_QN, NSA_HEADS * dh), NSA_HEADS, dh) * (NSA_DH ** -0.5 * LOG2E)
    wvt = jnp.concatenate([cols(OFF_VS, G * dh), cols(OFF_VW, G * dh)], axis=-1)
    return w.astype(BF16), jnp.swapaxes(qn, 1, 2).astype(BF16), jnp.swapaxes(wvt, 1, 2).astype(BF16)


def _bf16_terms(x, n):
    terms = []
    for _ in range(n):
        bits = np.array(x, np.float32).view(np.uint32)
        bits = (bits + np.uint32(0x7FFF) + ((bits >> np.uint32(16)) & np.uint32(1))) & np.uint32(0xFFFF0000)
        terms.append(float(bits.view(np.float32)))
        x = x - terms[-1]
    return terms


N_POS_TERMS = 3


def _pos_features(pos, n_slabs):
    out = np.zeros((len(pos), n_slabs * LANES), np.float32)
    for s in range(n_slabs):
        base = s * LANES + NSA_DH
        out[:, base:base + N_POS_TERMS] = (pos // L_SEL)[:, None]
        out[:, base + N_POS_TERMS:base + 2 * N_POS_TERMS] = (pos % L_SEL)[:, None]
    return out


def _const_tables(S):
    slopes = 2.0 ** (-8.0 * np.arange(1, NSA_HEADS + 1) / NSA_HEADS)
    terms = np.array(_bf16_terms(LOG2E, N_POS_TERMS))
    qaug = np.zeros((W_QN, 1), np.float32)
    for h in range(NSA_HEADS):
        base = h * LANES + NSA_DH
        qaug[base:base + N_POS_TERMS, 0] = slopes[h] * L_SEL * terms
        qaug[base + N_POS_TERMS:base + 2 * N_POS_TERMS, 0] = slopes[h] * terms
    posf = _pos_features(np.arange(S), 4)
    nc = S // D_CMP
    feat = _pos_features(np.arange(nc) * D_CMP + L_CMP - 1, 2)
    n_cmp = (S - L_CMP) // D_CMP + 1
    nblk = S // L_SEL
    cs = np.arange(n_cmp) * D_CMP
    ss = np.arange(nblk) * L_SEL
    ov = ((cs[:, None] < ss[None, :] + L_SEL) & (cs[:, None] + L_CMP > ss[None, :])).astype(np.float32)
    ovt = np.zeros((nblk, nc), np.float32)
    ovt[:, :n_cmp] = ov.T
    assert nblk <= LANES
    onehot = (np.arange(S)[:, None] // L_SEL == np.arange(LANES)[None, :]).astype(np.float32)
    return (jnp.asarray(qaug), jnp.asarray(posf), jnp.asarray(feat), jnp.asarray(ovt, BF16),
            jnp.asarray(onehot, BF16))


def _rope_consts(S):
    half = MLA_ROPE // 2
    inv = (1.0 / (np.float32(ROPE_THETA) ** (np.arange(0, MLA_ROPE, 2, dtype=np.float32) / MLA_ROPE))).astype(np.float32)
    ang = np.arange(S, dtype=np.float32)[:, None] * inv[None, :]
    cos, sin = np.cos(ang), np.sin(ang)
    z = lambda w: np.zeros((S, w), np.float32)
    ct = np.concatenate([np.ones((S, MLA_NOPE), np.float32), cos, cos, z(LANES - MLA_QK)], axis=1)
    sa = np.concatenate([z(MLA_NOPE), -sin, z(LANES - MLA_NOPE - half)], axis=1)
    sb = np.concatenate([z(MLA_NOPE + half), sin, z(LANES - MLA_QK)], axis=1)
    return tuple(jnp.asarray(np.ascontiguousarray(t)) for t in (ct, sa, sb, cos.T, sin.T))


def _mla_weights(w_uq, w_ukv):
    wq = _slabs(w_uq, MLA_HEADS, MLA_QK) * (MLA_QK ** -0.5 * LOG2E)
    kv = w_ukv.reshape(w_ukv.shape[:-1] + (MLA_HEADS, MLA_NOPE + MLA_V))
    wk = _slabs(kv[..., :MLA_NOPE].reshape(w_ukv.shape[:-1] + (MLA_HEADS * MLA_NOPE,)), MLA_HEADS, MLA_NOPE)
    wvt = jnp.swapaxes(kv[..., MLA_NOPE:].reshape(w_ukv.shape[:-1] + (MLA_HEADS * MLA_V,)), 1, 2)
    return jnp.swapaxes(wq, 1, 2).astype(BF16), wk.astype(BF16), wvt.astype(BF16)


def _compress_weights(pos, w1, w2k, w2v):
    G, dh, half = NSA_KV_HEADS, NSA_DH, L_CMP // 2
    L = pos.shape[0]
    eye = jnp.eye(G, dtype=F32)

    def first_layer(w):
        return jnp.einsum('nclde,gk->nclgdke', w, eye).reshape(L, 2, half * G * dh, G * CMP_HID).astype(BF16)

    def pos_row(p):
        return jnp.broadcast_to(p[:, :, :, None, :], (L, 2, half, G, dh)).reshape(L, 2, 1, half * G * dh)

    w2kp = jnp.pad(w2k, ((0, 0), (0, 0), (0, LANES - dh)))
    w2k_bd = jnp.einsum('ned,gk->ngekd', w2kp, eye).reshape(L, G * CMP_HID, G * LANES).astype(BF16)
    w2vt_bd = jnp.einsum('ned,gk->ngdke', w2v, eye).reshape(L, G * dh, G * CMP_HID).astype(BF16)
    return (pos_row(pos[:, :, :half]), pos_row(pos[:, :, half:]), first_layer(w1[:, :, :half]),
            first_layer(w1[:, :, half:]), w2k_bd, w2vt_bd)


def kernel(x, attn_norm, w_in, q_norm, kv_norm, w_uq, w_ukv, cmp_pos_k, cmp_pos_v, cmp_k_w1, cmp_k_w2, cmp_v_w1,
           cmp_v_w2, w_o, ffn_norm, w_up, conv_w, conv_b, w_down, final_norm):
    B, S, D = x.shape
    M = B * S
    assert D == D_MODEL and S % TM == 0 and TQ_MLA == TKV and TQ_NSA == TKV and S % TKV == 0
    qaug, posf, feat, ovt, onehot = _const_tables(S)
    rope = _rope_consts(S)
    nc = S // D_CMP

    w_row, w_qt, w_vt = _inproj_weights(w_in)
    mqt, mk, mvt = _mla_weights(w_uq, w_ukv)
    pa, pb, cwa, cwb, w2k, w2vt = _compress_weights(jnp.stack([cmp_pos_k, cmp_pos_v], axis=1),
                                                    jnp.stack([cmp_k_w1, cmp_v_w1], axis=1), cmp_k_w2, cmp_v_w2)
    wo, wup, wdn = w_o.astype(BF16), w_up.astype(BF16), w_down.astype(BF16)
    g_attn, g_q, g_kv, g_ffn = (a[:, None, :] for a in (attn_norm, q_norm, kv_norm, ffn_norm))
    cb = conv_b[:, None, :]
    fg = final_norm[None, :]

    h = x.reshape(M, D)
    for l in range(DEPTH):
        za, gt, kcr, vcr, qnt, ksw, vt = _inproj(h, g_attn, w_row, w_qt, w_vt, qaug, posf, l, B, S)
        qt, k, vmt = _mla_prep(za, g_q, g_kv, mqt, mk, mvt, rope, l, B, S)
        o_mla = _mla_attn(qt, k.reshape(B, S, -1), vmt, B, S)
        kc, vct = _compress(kcr.reshape(B, nc, -1), vcr.reshape(B, nc, -1), pa, pb, cwa, cwb, w2k, w2vt, feat, l)
        o_nsa = _nsa_attn(qnt, kc, vct, ksw, vt, gt, ovt, onehot, B, S)
        h = _outproj(h, o_mla.reshape(M, -1), o_nsa.reshape(M, -1), wo, l)
        h = _ffn(h, g_ffn, wup, conv_w, cb, wdn, fg, l, S, final=(l == DEPTH - 1))
    return h.reshape(B, S, D)
```

```python
import functools

import numpy as np
import jax
import jax.numpy as jnp
from jax import lax
from jax.experimental import pallas as pl
from jax.experimental.pallas import tpu as pltpu

F32 = jnp.float32
BF16 = jnp.bfloat16

D_MODEL = 1024
DEPTH = 2
EPS = 1e-6
NEG_INF = -1e30
LOG2E = float(np.log2(np.e))
LANES = 128

MLA_HEADS = 8
MLA_NOPE = 64
MLA_ROPE = 32
MLA_V = 64
MLA_Q_RANK = 384
MLA_KV_RANK = 256
MLA_QK = MLA_NOPE + MLA_ROPE
ROPE_THETA = 10000.0

NSA_HEADS = 8
NSA_KV_HEADS = 2
NSA_REP = NSA_HEADS // NSA_KV_HEADS
NSA_DH = 64
L_CMP = 32
D_CMP = 16
CMP_HID = 128
L_SEL = 64
N_SEL = 16
W_WIN = 512

D_FF = 2816
CONV_W = 3

_SIZES = [MLA_Q_RANK, MLA_KV_RANK, MLA_ROPE, NSA_HEADS * NSA_DH] + [NSA_KV_HEADS * NSA_DH] * 6 + [3 * NSA_HEADS]
_OFFS = [0] + [int(o) for o in np.cumsum(_SIZES)]
(OFF_CQ, OFF_CKV, OFF_KROPE, OFF_QN, OFF_KC, OFF_VC, OFF_KS, OFF_VS, OFF_KW, OFF_VW, OFF_GT, D_IN) = _OFFS

TM = 512
TKV = 256
TQ_MLA = 256
TQ_NSA = 256
VMEM_LIMIT = 56 * 1024 * 1024

W_ZA = 6 * LANES
W_ZB = 3 * LANES
W_KS = 4 * LANES
W_ROW = W_ZA + W_ZB + W_KS
W_QN = NSA_HEADS * LANES


def _cparams(sem):
    return pltpu.CompilerParams(dimension_semantics=sem, vmem_limit_bytes=VMEM_LIMIT)


def _layer_spec(a, l):
    return pl.BlockSpec((1,) + a.shape[1:], lambda *_: (l,) + (0,) * (a.ndim - 1))


def _const_spec(a):
    return pl.BlockSpec(a.shape, lambda *_: (0,) * a.ndim)


def _dot(a, b):
    return jnp.dot(a, b, preferred_element_type=F32)


def _dot_nt(a, b):
    return lax.dot_general(a, b, (((1,), (1,)), ((), ())), preferred_element_type=F32)


def _rms(x, g):
    return x * lax.rsqrt(jnp.mean(x * x, axis=-1, keepdims=True) + EPS) * g


def _store_tiles(ref, xt):
    for c in range(TM // TKV):
        ref[0, c] = xt[:, c * TKV:(c + 1) * TKV]


def _inproj_kernel(x_ref, g_ref, w_ref, wqt_ref, wvt_ref, qaug_ref, posf_ref,
                   za_ref, gt_ref, kcr_ref, vcr_ref, qnt_ref, ksw_ref, vt_ref):
    n = _rms(x_ref[...], g_ref[0]).astype(BF16)
    c0, c1 = W_ZA, W_ZA + W_ZB
    za_ref[...] = _dot(n, w_ref[0, :, 0:c0])
    zb = _dot(n, w_ref[0, :, c0:c1])
    gt_ref[...] = zb[:, 0:LANES]
    kcr_ref[...] = zb[:, LANES:2 * LANES]
    vcr_ref[...] = zb[:, 2 * LANES:3 * LANES]
    ksw_ref[...] = (_dot(n, w_ref[0, :, c1:W_ROW]) + posf_ref[...]).astype(BF16)
    qt = _dot_nt(wqt_ref[0], n)
    slabs, pad = [], LANES - NSA_DH
    for h in range(NSA_HEADS):
        slabs += [qt[h * NSA_DH:(h + 1) * NSA_DH, :], jnp.broadcast_to(qaug_ref[h * pad:(h + 1) * pad, :], (pad, TM))]
    _store_tiles(qnt_ref, jnp.concatenate(slabs, axis=0).astype(BF16))
    _store_tiles(vt_ref, _dot_nt(wvt_ref[0], n).astype(BF16))


def _inproj(h, g, w, wqt, wvt, qaug, posf, l, B, S):
    M = B * S
    spt = S // TM
    row = lambda width: pl.BlockSpec((TM, width), lambda i: (i, 0))
    tiles = lambda rows: pl.BlockSpec((1, TM // TKV, rows, TKV), lambda i: (i // spt, i % spt, 0, 0))
    return pl.pallas_call(
        _inproj_kernel,
        grid=(M // TM,),
        in_specs=[row(D_MODEL), _layer_spec(g, l), _layer_spec(w, l), _layer_spec(wqt, l), _layer_spec(wvt, l),
                  _const_spec(qaug), pl.BlockSpec((TM, W_KS), lambda i: (i % spt, 0))],
        out_specs=[row(W_ZA), row(LANES), row(LANES), row(LANES), tiles(W_QN), row(W_KS), tiles(4 * NSA_DH)],
        out_shape=[jax.ShapeDtypeStruct((M, W_ZA), F32),
                   jax.ShapeDtypeStruct((M, LANES), F32),
                   jax.ShapeDtypeStruct((M, LANES), F32),
                   jax.ShapeDtypeStruct((M, LANES), F32),
                   jax.ShapeDtypeStruct((B, S // TKV, W_QN, TKV), BF16),
                   jax.ShapeDtypeStruct((M, W_KS), BF16),
                   jax.ShapeDtypeStruct((B, S // TKV, 4 * NSA_DH, TKV), BF16)],
        compiler_params=_cparams(("parallel",)),
        name="attn_inproj",
    )(h, g, w, wqt, wvt, qaug, posf)


def _rope_slab(x, ct, sa, sb):
    return x * ct + pltpu.roll(x, LANES - MLA_ROPE // 2, 1) * sa + pltpu.roll(x, MLA_ROPE // 2, 1) * sb


def _mla_prep_kernel(za_ref, qg_ref, kg_ref, wqt_ref, wk_ref, wvt_ref, ct_ref, sa_ref, sb_ref, cost_ref, sint_ref,
                     qt_ref, k_ref, vt_ref):
    za = za_ref[...]
    cq = _rms(za[:, 0:MLA_Q_RANK], qg_ref[0]).astype(BF16)
    ckv = _rms(za[:, MLA_Q_RANK:MLA_Q_RANK + MLA_KV_RANK], kg_ref[0]).astype(BF16)
    kr = _rope_slab(za[:, 5 * LANES:6 * LANES], ct_ref[...], sa_ref[...], sb_ref[...])
    k = _dot(ckv, wk_ref[0])
    for h in range(MLA_HEADS):
        sl = slice(h * LANES, (h + 1) * LANES)
        k_ref[:, sl] = (k[:, sl] + kr).astype(BF16)
    qt = _dot_nt(wqt_ref[0], cq)
    cos, sin = cost_ref[...], sint_ref[...]
    half = MLA_ROPE // 2
    slabs = []
    for h in range(MLA_HEADS):
        base = h * LANES
        x1 = qt[base + MLA_NOPE:base + MLA_NOPE + half, :]
        x2 = qt[base + MLA_NOPE + half:base + MLA_QK, :]
        slabs += [qt[base:base + MLA_NOPE, :], x1 * cos - x2 * sin, x2 * cos + x1 * sin,
                  qt[base + MLA_QK:base + LANES, :]]
    _store_tiles(qt_ref, jnp.concatenate(slabs, axis=0).astype(BF16))
    _store_tiles(vt_ref, _dot_nt(wvt_ref[0], ckv).astype(BF16))


def _mla_prep(za, qg, kg, wqt, wk, wvt, rope, l, B, S):
    M = B * S
    spt = S // TM
    ct, sa, sb, cost, sint = rope
    row = lambda width: pl.BlockSpec((TM, width), lambda i: (i, 0))
    tab = pl.BlockSpec((TM, LANES), lambda i: (i % spt, 0))
    tabt = pl.BlockSpec((MLA_ROPE // 2, TM), lambda i: (0, i % spt))
    tiles = lambda rows: pl.BlockSpec((1, TM // TKV, rows, TKV), lambda i: (i // spt, i % spt, 0, 0))
    return pl.pallas_call(
        _mla_prep_kernel,
        grid=(M // TM,),
        in_specs=[row(W_ZA), _layer_spec(qg, l), _layer_spec(kg, l), _layer_spec(wqt, l), _layer_spec(wk, l),
                  _layer_spec(wvt, l), tab, tab, tab, tabt, tabt],
        out_specs=[tiles(MLA_HEADS * LANES), row(MLA_HEADS * LANES), tiles(MLA_HEADS * MLA_V)],
        out_shape=[jax.ShapeDtypeStruct((B, S // TKV, MLA_HEADS * LANES, TKV), BF16),
                   jax.ShapeDtypeStruct((M, MLA_HEADS * LANES), BF16),
                   jax.ShapeDtypeStruct((B, S // TKV, MLA_HEADS * MLA_V, TKV), BF16)],
        compiler_params=_cparams(("parallel",)),
        name="mla_prep",
    )(za, qg, kg, wqt, wk, wvt, ct, sa, sb, cost, sint)


ONES_ROWS = 16


def _stage_scores(s, s_ref, mx_ref, b, h):
    mx_ref[b, h] = jnp.max(s, axis=0, keepdims=True)
    s_ref[b, h, 0:s.shape[0], :] = s


def _online_update(vt, s_ref, mx_ref, b, m_ref, acc_ref, h):
    m_old = m_ref[h]
    m_new = jnp.maximum(m_old, mx_ref[b, h])
    p = jnp.exp2(s_ref[b, h] - m_new).astype(BF16)
    vt1 = jnp.concatenate([vt, jnp.ones((ONES_ROWS, vt.shape[1]), BF16)], axis=0)
    acc_ref[h] = jnp.exp2(m_old - m_new) * acc_ref[h] + _dot(vt1, p)
    m_ref[h] = m_new


def _normalised(acc_ref, h, dh):
    return acc_ref[h, 0:dh, :] / acc_ref[h, dh:dh + 1, :]


def _pipelined_sweep(i, n_heads, score, update, score_diag):
    every = range(n_heads)

    @pl.when(i >= 1)
    def _():
        score(0, 0, every)

    @pl.loop(0, i // 2)
    def _(j):
        for h in every:
            score(2 * j + 1, 1, [h])
            update(2 * j, 0, [h])
        for h in every:
            score(2 * j + 2, 0, [h])
            update(2 * j + 1, 1, [h])

    @pl.when(i % 2 == 1)
    def _():
        for h in every:
            score_diag(1, [h])
            update(i - 1, 0, [h])
        update(i, 1, every)

    @pl.when(i % 2 == 0)
    def _():
        score_diag(0, every)
        update(i, 0, every)


def _mla_attn_kernel(qt_ref, k_ref, vt_ref, o_ref, m_ref, acc_ref, s_ref, mx_ref):
    i = pl.program_id(1)
    tq = TQ_MLA
    m_ref[...] = jnp.full(m_ref.shape, -jnp.inf, F32)
    acc_ref[...] = jnp.zeros(acc_ref.shape, F32)

    every = range(MLA_HEADS)

    def score(kt, b, heads, keep=None):
        for h in heads:
            sl = slice(h * LANES, (h + 1) * LANES)
            k = k_ref[0, pl.ds(pl.multiple_of(kt * TKV, TKV), TKV), sl]
            s = _dot(k, qt_ref[0, 0, sl, :])
            if keep is not None:
                s = jnp.where(keep, s, NEG_INF)
            _stage_scores(s, s_ref, mx_ref, b, h)

    def update(kt, b, heads):
        for h in heads:
            _online_update(vt_ref[0, kt, h * MLA_V:(h + 1) * MLA_V, :], s_ref, mx_ref, b, m_ref, acc_ref, h)

    causal = lax.broadcasted_iota(jnp.int32, (TKV, tq), 0) <= lax.broadcasted_iota(jnp.int32, (TKV, tq), 1)
    _pipelined_sweep(i, MLA_HEADS, score, update, lambda b, heads: score(i, b, heads, causal))
    out = jnp.concatenate([_normalised(acc_ref, h, MLA_V) for h in range(MLA_HEADS)], axis=0)
    o_ref[0] = out.T.astype(BF16)


def _mla_attn(q, k, vt, B, S):
    nkt = S // TKV
    hw = MLA_HEADS * LANES
    return pl.pallas_call(
        _mla_attn_kernel,
        grid=(B, S // TQ_MLA),
        in_specs=[pl.BlockSpec((1, 1, hw, TQ_MLA), lambda b, i: (b, i, 0, 0)),
                  pl.BlockSpec((1, S, hw), lambda b, i: (b, 0, 0)),
                  pl.BlockSpec((1, nkt, MLA_HEADS * MLA_V, TKV), lambda b, i: (b, 0, 0, 0))],
        out_specs=pl.BlockSpec((1, TQ_MLA, MLA_HEADS * MLA_V), lambda b, i: (b, i, 0)),
        out_shape=jax.ShapeDtypeStruct((B, S, MLA_HEADS * MLA_V), BF16),
        scratch_shapes=[pltpu.VMEM((MLA_HEADS, 1, TQ_MLA), F32),
                        pltpu.VMEM((MLA_HEADS, MLA_V + ONES_ROWS, TQ_MLA), F32),
                        pltpu.VMEM((2, MLA_HEADS, TKV, TQ_MLA), F32), pltpu.VMEM((2, MLA_HEADS, 1, TQ_MLA), F32)],
        compiler_params=_cparams(("parallel", "arbitrary")),
        name="mla_attn",
    )(q, k, vt)


def _gelu_tanh(x):
    return 0.5 * x * (1.0 + jnp.tanh(np.sqrt(2.0 / np.pi).astype(np.float32) * (x + 0.044715 * (x * x * x))))


def _compress_kernel(kx_ref, vx_ref, pa_ref, pb_ref, wa_ref, wb_ref, w2k_ref, w2vt_ref, feat_ref, kc_ref, vct_ref):
    nc = kx_ref.shape[1]

    def hidden(x, kv):
        ya = _dot((x + pa_ref[0, kv]).astype(BF16), wa_ref[0, kv])
        yb = _dot((x + pb_ref[0, kv]).astype(BF16), wb_ref[0, kv])
        return _gelu_tanh(ya + pltpu.roll(yb, nc - 1, 0)).astype(BF16)

    kc_ref[0] = (_dot(hidden(kx_ref[0], 0), w2k_ref[0]) + feat_ref[...]).astype(BF16)
    vct_ref[0] = _dot_nt(w2vt_ref[0], hidden(vx_ref[0], 1)).astype(BF16)


def _compress(kx, vx, pa, pb, wa, wb, w2k, w2vt, feat, l):
    B, nc, width = kx.shape
    xs = pl.BlockSpec((1, nc, width), lambda b: (b, 0, 0))
    return pl.pallas_call(
        _compress_kernel,
        grid=(B,),
        in_specs=[xs, xs] + [_layer_spec(a, l) for a in (pa, pb, wa, wb, w2k, w2vt)] + [_const_spec(feat)],
        out_specs=[pl.BlockSpec((1, nc, 2 * LANES), lambda b: (b, 0, 0)),
                   pl.BlockSpec((1, 2 * NSA_DH, nc), lambda b: (b, 0, 0))],
        out_shape=[jax.ShapeDtypeStruct((B, nc, 2 * LANES), BF16),
                   jax.ShapeDtypeStruct((B, 2 * NSA_DH, nc), BF16)],
        compiler_params=_cparams(("parallel",)),
        name="nsa_compress",
    )(kx, vx, pa, pb, wa, wb, w2k, w2vt, feat)


def _nsa_attn_kernel(qnt_ref, kc_ref, vct_ref, ks_ref, kw_ref, vst_ref, vwt_ref, gt_ref, ovt_ref, oh_ref,
                     o_ref, ocmp_ref, m_ref, acc_ref, mw_ref, accw_ref, s_ref, mx_ref, qs_ref):
    i = pl.program_id(1)
    tq = TQ_NSA
    G, R, dh = NSA_KV_HEADS, NSA_REP, NSA_DH
    nc = kc_ref.shape[1]
    nblk = ovt_ref.shape[0]
    t0 = i * tq
    tpos = t0 + lax.broadcasted_iota(jnp.int32, (1, tq), 1)
    q_of = lambda h: qnt_ref[0, 0, h * LANES:(h + 1) * LANES, :]

    for ref in (m_ref, mw_ref):
        ref[...] = jnp.full(ref.shape, -jnp.inf, F32)
    for ref in (acc_ref, accw_ref):
        ref[...] = jnp.zeros(ref.shape, F32)

    every = range(G * R)
    rel = lax.broadcasted_iota(jnp.int32, (TKV, tq), 0) - lax.broadcasted_iota(jnp.int32, (TKV, tq), 1)
    causal = rel <= 0

    def score(kt, b, heads, sel, keep=None):
        rows = pl.ds(pl.multiple_of(kt * TKV, TKV), TKV)
        for h in heads:
            g = h // R
            if sel:
                k = jnp.concatenate([ks_ref[0, rows, g * LANES:(g + 1) * LANES], oh_ref[rows, :]], axis=1)
            else:
                k = kw_ref[0, rows, g * LANES:(g + 1) * LANES]
            s = _dot(k, qs_ref[h] if sel else q_of(h))
            if keep is not None:
                s = jnp.where(keep, s, NEG_INF)
            _stage_scores(s, s_ref, mx_ref, b, h)

    def update(kt, b, heads, sel):
        vt_ref, state = (vst_ref, (m_ref, acc_ref)) if sel else (vwt_ref, (mw_ref, accw_ref))
        for h in heads:
            g = h // R
            _online_update(vt_ref[0, kt, g * dh:(g + 1) * dh, :], s_ref, mx_ref, b, *state, h)

    assert W_WIN == 2 * TKV
    kt1, kt2 = jnp.maximum(i - 1, 0), jnp.maximum(i - 2, 0)
    keep1 = jnp.broadcast_to(i >= 1, (TKV, tq))
    keep2 = (rel > 0) & (i >= 2)

    nrow = lax.broadcasted_iota(jnp.int32, (nc, 1), 0)
    bias_c = jnp.where((tpos >= nrow * D_CMP + (L_CMP - 1)) & (nrow < nc - 1), 0.0, NEG_INF)
    has_cmp = tpos >= L_CMP - 1
    for h in every:
        g = h // R
        _stage_scores(_dot(kc_ref[0, :, g * LANES:(g + 1) * LANES], q_of(h)) + bias_c, s_ref, mx_ref, 0, h)
    score(i, 1, every, False, causal)
    for h in every:
        score(kt1, 2, [h], False, keep1)
        update(i, 1, [h], False)

    imps = [None] * G
    for h in every:
        g = h // R
        e = jnp.exp2(s_ref[0, h, 0:nc, :] - mx_ref[0, h])
        inv = jnp.where(has_cmp, 1.0 / jnp.sum(e, axis=0, keepdims=True), 0.0)
        both = _dot(jnp.concatenate([vct_ref[0, g * dh:(g + 1) * dh, :], ovt_ref[...]], axis=0), e.astype(BF16))
        ocmp_ref[h] = both[0:dh, :] * inv
        contrib = both[dh:dh + nblk, :] * inv
        imps[g] = contrib if imps[g] is None else imps[g] + contrib
        score(kt2, 3, [h], False, keep2)
        update(kt1, 2, [h], False)
    update(kt2, 3, every, False)

    cur = lax.shift_right_logical(tpos, 6)
    jrow = lax.broadcasted_iota(jnp.int32, (nblk, 1), 0)
    sub = lax.broadcasted_iota(jnp.int32, (8, 1), 0)
    n_slab = nblk // 8
    for g in range(G):
        imp = jnp.where(jrow > cur, -jnp.inf, imps[g])
        imp = jnp.where((jrow == 0) | (jrow == cur) | (jrow == cur - 1), jnp.inf, imp)
        slabs = [imp[8 * v:8 * v + 8, :] for v in range(n_slab)]
        ranks = [jnp.zeros((8, tq), F32) for _ in range(n_slab)]
        for jj in range(nblk):
            row = imp[jj:jj + 1, :]
            for v in range(n_slab):
                if v > jj // 8:
                    beats = row >= slabs[v]
                elif v < jj // 8:
                    beats = row > slabs[v]
                else:
                    beats = (row > slabs[v]) | ((row == slabs[v]) & (sub > jj % 8))
                ranks[v] = ranks[v] + jnp.where(beats, 1.0, 0.0)
        rank = jnp.concatenate(ranks, axis=0)
        mb = jnp.where(rank < float(min(N_SEL, nblk)), 0.0, NEG_INF)
        mb = jnp.concatenate([mb, jnp.zeros((LANES - nblk, tq), F32)], axis=0).astype(BF16)
        for r in range(R):
            qs_ref[g * R + r] = jnp.concatenate([q_of(g * R + r), mb], axis=0)

    _pipelined_sweep(i, G * R, functools.partial(score, sel=True), functools.partial(update, sel=True),
                     lambda b, heads: score(i, b, heads, True, causal))

    sig = 1.0 / (1.0 + jnp.exp(-gt_ref[0].T[0:G * 16, :]))
    outs = []
    for h in range(G * R):
        row = (h // R) * 16 + (h % R) * 4
        outs.append(sig[row:row + 1, :] * ocmp_ref[h] + sig[row + 1:row + 2, :] * _normalised(acc_ref, h, dh)
                    + sig[row + 2:row + 3, :] * _normalised(accw_ref, h, dh))
    o_ref[0] = jnp.concatenate(outs, axis=0).T.astype(BF16)


def _nsa_attn(qnt, kc, vct, ksw, vt, gt, ovt, oh, B, S):
    nkt = S // TKV
    nc = kc.shape[1]
    G = NSA_KV_HEADS
    ksw3 = ksw.reshape(B, S, W_KS)
    gt3 = gt.reshape(B, S, LANES)
    H, dh, tq = NSA_HEADS, NSA_DH, TQ_NSA
    state = [pltpu.VMEM((H, 1, tq), F32), pltpu.VMEM((H, dh + ONES_ROWS, tq), F32)]
    return pl.pallas_call(
        _nsa_attn_kernel,
        grid=(B, S // tq),
        in_specs=[pl.BlockSpec((1, 1, W_QN, tq), lambda b, i: (b, i, 0, 0)),
                  pl.BlockSpec((1, nc, G * LANES), lambda b, i: (b, 0, 0)),
                  pl.BlockSpec((1, G * dh, nc), lambda b, i: (b, 0, 0)),
                  pl.BlockSpec((1, S, G * LANES), lambda b, i: (b, 0, 0)),
                  pl.BlockSpec((1, S, G * LANES), lambda b, i: (b, 0, 1)),
                  pl.BlockSpec((1, nkt, G * dh, TKV), lambda b, i: (b, 0, 0, 0)),
                  pl.BlockSpec((1, nkt, G * dh, TKV), lambda b, i: (b, 0, 1, 0)),
                  pl.BlockSpec((1, tq, LANES), lambda b, i: (b, i, 0)),
                  _const_spec(ovt), _const_spec(oh)],
        out_specs=pl.BlockSpec((1, tq, H * dh), lambda b, i: (b, i, 0)),
        out_shape=jax.ShapeDtypeStruct((B, S, H * dh), BF16),
        scratch_shapes=([pltpu.VMEM((H, dh, tq), F32)] + state + state
                        + [pltpu.VMEM((4, H, TKV, tq), F32), pltpu.VMEM((4, H, 1, tq), F32),
                           pltpu.VMEM((H, 2 * LANES, tq), BF16)]),
        compiler_params=_cparams(("parallel", "arbitrary")),
        name="nsa_attn",
    )(qnt, kc, vct, ksw3, ksw3, vt, vt, gt3, ovt, oh)


def _outproj_kernel(h_ref, a_ref, b_ref, w_ref, o_ref):
    ka = a_ref.shape[1]
    o_ref[...] = h_ref[...] + _dot(a_ref[...], w_ref[0, 0:ka, :]) + _dot(b_ref[...], w_ref[0, ka:, :])


def _outproj(h, oa, ob, w, l):
    M = h.shape[0]
    row = lambda width: pl.BlockSpec((TM, width), lambda i: (i, 0))
    return pl.pallas_call(
        _outproj_kernel,
        grid=(M // TM,),
        in_specs=[row(D_MODEL), row(oa.shape[1]), row(ob.shape[1]), _layer_spec(w, l)],
        out_specs=row(D_MODEL),
        out_shape=jax.ShapeDtypeStruct((M, D_MODEL), F32),
        compiler_params=_cparams(("parallel",)),
        name="attn_outproj",
    )(h, oa, ob, w)


FF_CHUNK = 256
HALO = 16


def _ffn_kernel(x_ref, xp_ref, g_ref, wup_ref, cw_ref, cb_ref, wdn_ref, fg_ref, o_ref, n_ref, h_ref, act_ref,
                *, final, spt):
    i = pl.program_id(0)
    n_ref[0:HALO, :] = _rms(xp_ref[...], g_ref[0]).astype(BF16)
    n_ref[HALO:HALO + TM, :] = _rms(x_ref[...], g_ref[0]).astype(BF16)
    keep = jnp.where(i % spt == 0, 0.0, 1.0)

    def conv(slot, col):
        hx = _dot(n_ref[...], wup_ref[0, :, col:col + FF_CHUNK])
        h_ref[slot, 0:HALO, :] = hx[0:HALO, :] * keep
        h_ref[slot, HALO:HALO + TM, :] = hx[HALO:HALO + TM, :]
        out = cb_ref[0, :, col:col + FF_CHUNK]
        for j in range(CONV_W):
            lo = HALO - (CONV_W - 1) + j
            out = out + h_ref[slot, lo:lo + TM, :] * cw_ref[0, j:j + 1, col:col + FF_CHUNK]
        return out

    for c in range(D_FF // FF_CHUNK):
        col = c * FF_CHUNK
        gate = conv(0, col)
        up = conv(1, D_FF + col)
        act_ref[:, col:col + FF_CHUNK] = (gate * (1.0 / (1.0 + jnp.exp(-gate))) * up).astype(BF16)
    out = x_ref[...] + _dot(act_ref[...], wdn_ref[0])
    if final:
        out = _rms(out, fg_ref[...])
    o_ref[...] = out


def _ffn(h, g, wup, cw, cb, wdn, fg, l, S, final):
    M = h.shape[0]
    spt = S // TM
    row = pl.BlockSpec((TM, D_MODEL), lambda i: (i, 0))
    prev = pl.BlockSpec((HALO, D_MODEL), lambda i: (jnp.maximum(i * (TM // HALO) - 1, 0), 0))
    return pl.pallas_call(
        functools.partial(_ffn_kernel, final=final, spt=spt),
        grid=(M // TM,),
        in_specs=[row, prev, _layer_spec(g, l), _layer_spec(wup, l), _layer_spec(cw, l), _layer_spec(cb, l),
                  _layer_spec(wdn, l), _const_spec(fg)],
        out_specs=row,
        out_shape=jax.ShapeDtypeStruct((M, D_MODEL), F32),
        scratch_shapes=[pltpu.VMEM((HALO + TM, D_MODEL), BF16), pltpu.VMEM((2, HALO + TM, FF_CHUNK), F32),
                        pltpu.VMEM((TM, D_FF), BF16)],
        compiler_params=_cparams(("parallel",)),
        name="conv_ffn",
    )(h, h, g, wup, cw, cb, wdn, fg)


def _slabs(w, n, width, pad_to=LANES):
    lead = w.shape[:-1]
    w = w.reshape(lead + (n, width))
    w = jnp.pad(w, [(0, 0)] * (w.ndim - 1) + [(0, pad_to - width)])
    return w.reshape(lead + (n * pad_to,))


def _inproj_weights(w_in):
    G, dh = NSA_KV_HEADS, NSA_DH
    cols = lambda off, n: w_in[..., off:off + n]
    zeros = lambda n: jnp.zeros(w_in.shape[:-1] + (n,), w_in.dtype)
    kr = jnp.concatenate([zeros(MLA_NOPE), cols(OFF_KROPE, MLA_ROPE), zeros(LANES - MLA_QK)], axis=-1)
    gates = _slabs(cols(OFF_GT, 3 * NSA_HEADS), NSA_HEADS, 3, 4)
    gates = jnp.concatenate([gates, zeros(LANES - 4 * NSA_HEADS)], axis=-1)
    w = jnp.concatenate([cols(OFF_CQ, MLA_Q_RANK), cols(OFF_CKV, MLA_KV_RANK), kr,
                         gates, cols(OFF_KC, G * dh), cols(OFF_VC, G * dh),
                         _slabs(cols(OFF_KS, G * dh), G, dh), _slabs(cols(OFF_KW, G * dh), G, dh)], axis=-1)
    assert w.shape[-1] == W_ROW
    qn = cols(OFF_QN, NSA_HEADS * dh) * (NSA_DH ** -0.5 * LOG2E)
    wvt = jnp.concatenate([cols(OFF_VS, G * dh), cols(OFF_VW, G * dh)], axis=-1)
    return w.astype(BF16), jnp.swapaxes(qn, 1, 2).astype(BF16), jnp.swapaxes(wvt, 1, 2).astype(BF16)


def _bf16_terms(x, n):
    terms = []
    for _ in range(n):
        bits = np.array(x, np.float32).view(np.uint32)
        bits = (bits + np.uint32(0x7FFF) + ((bits >> np.uint32(16)) & np.uint32(1))) & np.uint32(0xFFFF0000)
        terms.append(float(bits.view(np.float32)))
        x = x - terms[-1]
    return terms


N_POS_TERMS = 3


def _pos_features(pos, n_slabs):
    out = np.zeros((len(pos), n_slabs * LANES), np.float32)
    for s in range(n_slabs):
        base = s * LANES + NSA_DH
        out[:, base:base + N_POS_TERMS] = (pos // L_SEL)[:, None]
        out[:, base + N_POS_TERMS:base + 2 * N_POS_TERMS] = (pos % L_SEL)[:, None]
    return out


def _const_tables(S):
    slopes = 2.0 ** (-8.0 * np.arange(1, NSA_HEADS + 1) / NSA_HEADS)
    terms = np.array(_bf16_terms(LOG2E, N_POS_TERMS))
    pad = LANES - NSA_DH
    qaug = np.zeros((NSA_HEADS * pad, 1), np.float32)
    for h in range(NSA_HEADS):
        base = h * pad
        qaug[base:base + N_POS_TERMS, 0] = slopes[h] * L_SEL * terms
        qaug[base + N_POS_TERMS:base + 2 * N_POS_TERMS, 0] = slopes[h] * terms
    posf = _pos_features(np.arange(S), 4)
    nc = S // D_CMP
    feat = _pos_features(np.arange(nc) * D_CMP + L_CMP - 1, 2)
    n_cmp = (S - L_CMP) // D_CMP + 1
    nblk = S // L_SEL
    cs = np.arange(n_cmp) * D_CMP
    ss = np.arange(nblk) * L_SEL
    ov = ((cs[:, None] < ss[None, :] + L_SEL) & (cs[:, None] + L_CMP > ss[None, :])).astype(np.float32)
    ovt = np.zeros((nblk, nc), np.float32)
    ovt[:, :n_cmp] = ov.T
    assert nblk <= LANES
    onehot = (np.arange(S)[:, None] // L_SEL == np.arange(LANES)[None, :]).astype(np.float32)
    return (jnp.asarray(qaug), jnp.asarray(posf), jnp.asarray(feat), jnp.asarray(ovt, BF16),
            jnp.asarray(onehot, BF16))


def _rope_consts(S):
    half = MLA_ROPE // 2
    inv = (1.0 / (np.float32(ROPE_THETA) ** (np.arange(0, MLA_ROPE, 2, dtype=np.float32) / MLA_ROPE))).astype(np.float32)
    ang = np.arange(S, dtype=np.float32)[:, None] * inv[None, :]
    cos, sin = np.cos(ang), np.sin(ang)
    z = lambda w: np.zeros((S, w), np.float32)
    ct = np.concatenate([np.ones((S, MLA_NOPE), np.float32), cos, cos, z(LANES - MLA_QK)], axis=1)
    sa = np.concatenate([z(MLA_NOPE), -sin, z(LANES - MLA_NOPE - half)], axis=1)
    sb = np.concatenate([z(MLA_NOPE + half), sin, z(LANES - MLA_QK)], axis=1)
    return tuple(jnp.asarray(np.ascontiguousarray(t)) for t in (ct, sa, sb, cos.T, sin.T))


def _mla_weights(w_uq, w_ukv):
    wq = _slabs(w_uq, MLA_HEADS, MLA_QK) * (MLA_QK ** -0.5 * LOG2E)
    kv = w_ukv.reshape(w_ukv.shape[:-1] + (MLA_HEADS, MLA_NOPE + MLA_V))
    wk = _slabs(kv[..., :MLA_NOPE].reshape(w_ukv.shape[:-1] + (MLA_HEADS * MLA_NOPE,)), MLA_HEADS, MLA_NOPE)
    wvt = jnp.swapaxes(kv[..., MLA_NOPE:].reshape(w_ukv.shape[:-1] + (MLA_HEADS * MLA_V,)), 1, 2)
    return jnp.swapaxes(wq, 1, 2).astype(BF16), wk.astype(BF16), wvt.astype(BF16)


def _compress_weights(pos, w1, w2k, w2v):
    G, dh, half = NSA_KV_HEADS, NSA_DH, L_CMP // 2
    L = pos.shape[0]
    eye = jnp.eye(G, dtype=F32)

    def first_layer(w):
        return jnp.einsum('nclde,gk->nclgdke', w, eye).reshape(L, 2, half * G * dh, G * CMP_HID).astype(BF16)

    def pos_row(p):
        return jnp.broadcast_to(p[:, :, :, None, :], (L, 2, half, G, dh)).reshape(L, 2, 1, half * G * dh)

    w2kp = jnp.pad(w2k, ((0, 0), (0, 0), (0, LANES - dh)))
    w2k_bd = jnp.einsum('ned,gk->ngekd', w2kp, eye).reshape(L, G * CMP_HID, G * LANES).astype(BF16)
    w2vt_bd = jnp.einsum('ned,gk->ngdke', w2v, eye).reshape(L, G * dh, G * CMP_HID).astype(BF16)
    return (pos_row(pos[:, :, :half]), pos_row(pos[:, :, half:]), first_layer(w1[:, :, :half]),
            first_layer(w1[:, :, half:]), w2k_bd, w2vt_bd)


def kernel(x, attn_norm, w_in, q_norm, kv_norm, w_uq, w_ukv, cmp_pos_k, cmp_pos_v, cmp_k_w1, cmp_k_w2, cmp_v_w1,
           cmp_v_w2, w_o, ffn_norm, w_up, conv_w, conv_b, w_down, final_norm):
    B, S, D = x.shape
    M = B * S
    assert D == D_MODEL and S % TM == 0 and TQ_MLA == TKV and TQ_NSA == TKV and S % TKV == 0
    qaug, posf, feat, ovt, onehot = _const_tables(S)
    rope = _rope_consts(S)
    nc = S // D_CMP

    w_row, w_qt, w_vt = _inproj_weights(w_in)
    mqt, mk, mvt = _mla_weights(w_uq, w_ukv)
    pa, pb, cwa, cwb, w2k, w2vt = _compress_weights(jnp.stack([cmp_pos_k, cmp_pos_v], axis=1),
                                                    jnp.stack([cmp_k_w1, cmp_v_w1], axis=1), cmp_k_w2, cmp_v_w2)
    wo, wup, wdn = w_o.astype(BF16), w_up.astype(BF16), w_down.astype(BF16)
    g_attn, g_q, g_kv, g_ffn = (a[:, None, :] for a in (attn_norm, q_norm, kv_norm, ffn_norm))
    cb = conv_b[:, None, :]
    fg = final_norm[None, :]

    h = x.reshape(M, D)
    for l in range(DEPTH):
        za, gt, kcr, vcr, qnt, ksw, vt = _inproj(h, g_attn, w_row, w_qt, w_vt, qaug, posf, l, B, S)
        qt, k, vmt = _mla_prep(za, g_q, g_kv, mqt, mk, mvt, rope, l, B, S)
        o_mla = _mla_attn(qt, k.reshape(B, S, -1), vmt, B, S)
        kc, vct = _compress(kcr.reshape(B, nc, -1), vcr.reshape(B, nc, -1), pa, pb, cwa, cwb, w2k, w2vt, feat, l)
        o_nsa = _nsa_attn(qnt, kc, vct, ksw, vt, gt, ovt, onehot, B, S)
        h = _outproj(h, o_mla.reshape(M, -1), o_nsa.reshape(M, -1), wo, l)
        h = _ffn(h, g_ffn, wup, conv_w, cb, wdn, fg, l, S, final=(l == DEPTH - 1))
    return h.reshape(B, S, D)
```

```python
import functools

import numpy as np
import jax
import jax.numpy as jnp
from jax import lax
from jax.experimental import pallas as pl
from jax.experimental.pallas import tpu as pltpu

F32 = jnp.float32
BF16 = jnp.bfloat16

D_MODEL = 1024
DEPTH = 2
EPS = 1e-6
NEG_INF = -1e30
LOG2E = float(np.log2(np.e))
LANES = 128

MLA_HEADS = 8
MLA_NOPE = 64
MLA_ROPE = 32
MLA_V = 64
MLA_Q_RANK = 384
MLA_KV_RANK = 256
MLA_QK = MLA_NOPE + MLA_ROPE
ROPE_THETA = 10000.0

NSA_HEADS = 8
NSA_KV_HEADS = 2
NSA_REP = NSA_HEADS // NSA_KV_HEADS
NSA_DH = 64
L_CMP = 32
D_CMP = 16
CMP_HID = 128
L_SEL = 64
N_SEL = 16
W_WIN = 512

D_FF = 2816
CONV_W = 3

_SIZES = [MLA_Q_RANK, MLA_KV_RANK, MLA_ROPE, NSA_HEADS * NSA_DH] + [NSA_KV_HEADS * NSA_DH] * 6 + [3 * NSA_HEADS]
_OFFS = [0] + [int(o) for o in np.cumsum(_SIZES)]
(OFF_CQ, OFF_CKV, OFF_KROPE, OFF_QN, OFF_KC, OFF_VC, OFF_KS, OFF_VS, OFF_KW, OFF_VW, OFF_GT, D_IN) = _OFFS

TM = 512
TKV = 256
TQ_MLA = 256
TQ_NSA = 256
VMEM_LIMIT = 56 * 1024 * 1024

W_ZA = 6 * LANES
W_ZB = 3 * LANES
W_KS = 4 * LANES
W_ROW = W_ZA + W_ZB + W_KS
W_QN = NSA_HEADS * LANES


def _cparams(sem):
    return pltpu.CompilerParams(dimension_semantics=sem, vmem_limit_bytes=VMEM_LIMIT)


def _layer_spec(a, l):
    return pl.BlockSpec((1,) + a.shape[1:], lambda *_: (l,) + (0,) * (a.ndim - 1))


def _const_spec(a):
    return pl.BlockSpec(a.shape, lambda *_: (0,) * a.ndim)


def _dot(a, b):
    return jnp.dot(a, b, preferred_element_type=F32)


def _dot_nt(a, b):
    return lax.dot_general(a, b, (((1,), (1,)), ((), ())), preferred_element_type=F32)


def _rms(x, g):
    return x * lax.rsqrt(jnp.mean(x * x, axis=-1, keepdims=True) + EPS) * g


def _store_tiles(ref, xt):
    for c in range(TM // TKV):
        ref[0, c] = xt[:, c * TKV:(c + 1) * TKV]


def _inproj_kernel(x_ref, g_ref, w_ref, wqt_ref, wvt_ref, qaug_ref, posf_ref,
                   qg_ref, kg_ref, mwqt_ref, mwk_ref, mwvt_ref, ct_ref, sa_ref, sb_ref, cost_ref, sint_ref,
                   gt_ref, kcr_ref, vcr_ref, qnt_ref, ksw_ref, vt_ref, mqt_ref, mk_ref, mvt_ref):
    n = _rms(x_ref[...], g_ref[0]).astype(BF16)
    c0, c1 = W_ZA, W_ZA + W_ZB
    _mla_prep(_dot(n, w_ref[0, :, 0:c0]), qg_ref, kg_ref, mwqt_ref, mwk_ref, mwvt_ref,
              ct_ref, sa_ref, sb_ref, cost_ref, sint_ref, mqt_ref, mk_ref, mvt_ref)
    zb = _dot(n, w_ref[0, :, c0:c1])
    gt_ref[...] = zb[:, 0:LANES]
    kcr_ref[...] = zb[:, LANES:2 * LANES]
    vcr_ref[...] = zb[:, 2 * LANES:3 * LANES]
    ksw_ref[...] = (_dot(n, w_ref[0, :, c1:W_ROW]) + posf_ref[...]).astype(BF16)
    qt = _dot_nt(wqt_ref[0], n)
    slabs, pad = [], LANES - NSA_DH
    for h in range(NSA_HEADS):
        slabs += [qt[h * NSA_DH:(h + 1) * NSA_DH, :], jnp.broadcast_to(qaug_ref[h * pad:(h + 1) * pad, :], (pad, TM))]
    _store_tiles(qnt_ref, jnp.concatenate(slabs, axis=0).astype(BF16))
    _store_tiles(vt_ref, _dot_nt(wvt_ref[0], n).astype(BF16))


def _inproj(h, g, w, wqt, wvt, qaug, posf, mla, rope, l, B, S):
    M = B * S
    spt = S // TM
    ct, sa, sb, cost, sint = rope
    row = lambda width: pl.BlockSpec((TM, width), lambda i: (i, 0))
    tiles = lambda rows: pl.BlockSpec((1, TM // TKV, rows, TKV), lambda i: (i // spt, i % spt, 0, 0))
    tab = pl.BlockSpec((TM, LANES), lambda i: (i % spt, 0))
    tabt = pl.BlockSpec((MLA_ROPE // 2, TM), lambda i: (0, i % spt))
    return pl.pallas_call(
        _inproj_kernel,
        grid=(M // TM,),
        in_specs=[row(D_MODEL), _layer_spec(g, l), _layer_spec(w, l), _layer_spec(wqt, l), _layer_spec(wvt, l),
                  _const_spec(qaug), pl.BlockSpec((TM, W_KS), lambda i: (i % spt, 0))]
                 + [_layer_spec(a, l) for a in mla] + [tab, tab, tab, tabt, tabt],
        out_specs=[row(LANES), row(LANES), row(LANES), tiles(W_QN), row(W_KS), tiles(4 * NSA_DH),
                   tiles(MLA_HEADS * LANES), row(MLA_HEADS * LANES), tiles(MLA_HEADS * MLA_V)],
        out_shape=[jax.ShapeDtypeStruct((M, LANES), F32),
                   jax.ShapeDtypeStruct((M, LANES), F32),
                   jax.ShapeDtypeStruct((M, LANES), F32),
                   jax.ShapeDtypeStruct((B, S // TKV, W_QN, TKV), BF16),
                   jax.ShapeDtypeStruct((M, W_KS), BF16),
                   jax.ShapeDtypeStruct((B, S // TKV, 4 * NSA_DH, TKV), BF16),
                   jax.ShapeDtypeStruct((B, S // TKV, MLA_HEADS * LANES, TKV), BF16),
                   jax.ShapeDtypeStruct((M, MLA_HEADS * LANES), BF16),
                   jax.ShapeDtypeStruct((B, S // TKV, MLA_HEADS * MLA_V, TKV), BF16)],
        compiler_params=_cparams(("parallel",)),
        name="attn_inproj",
    )(h, g, w, wqt, wvt, qaug, posf, *mla, ct, sa, sb, cost, sint)


def _rope_slab(x, ct, sa, sb):
    return x * ct + pltpu.roll(x, LANES - MLA_ROPE // 2, 1) * sa + pltpu.roll(x, MLA_ROPE // 2, 1) * sb


def _mla_prep(za, qg_ref, kg_ref, wqt_ref, wk_ref, wvt_ref, ct_ref, sa_ref, sb_ref, cost_ref, sint_ref,
              qt_ref, k_ref, vt_ref):
    cq = _rms(za[:, 0:MLA_Q_RANK], qg_ref[0]).astype(BF16)
    ckv = _rms(za[:, MLA_Q_RANK:MLA_Q_RANK + MLA_KV_RANK], kg_ref[0]).astype(BF16)
    kr = _rope_slab(za[:, 5 * LANES:6 * LANES], ct_ref[...], sa_ref[...], sb_ref[...])
    k = _dot(ckv, wk_ref[0])
    for h in range(MLA_HEADS):
        sl = slice(h * LANES, (h + 1) * LANES)
        k_ref[:, sl] = (k[:, sl] + kr).astype(BF16)
    qt = _dot_nt(wqt_ref[0], cq)
    cos, sin = cost_ref[...], sint_ref[...]
    half = MLA_ROPE // 2
    slabs = []
    for h in range(MLA_HEADS):
        base = h * LANES
        x1 = qt[base + MLA_NOPE:base + MLA_NOPE + half, :]
        x2 = qt[base + MLA_NOPE + half:base + MLA_QK, :]
        slabs += [qt[base:base + MLA_NOPE, :], x1 * cos - x2 * sin, x2 * cos + x1 * sin,
                  qt[base + MLA_QK:base + LANES, :]]
    _store_tiles(qt_ref, jnp.concatenate(slabs, axis=0).astype(BF16))
    _store_tiles(vt_ref, _dot_nt(wvt_ref[0], ckv).astype(BF16))


ONES_ROWS = 16


def _stage_scores(s, s_ref, mx_ref, b, h):
    mx_ref[b, h] = jnp.max(s, axis=0, keepdims=True)
    s_ref[b, h, 0:s.shape[0], :] = s


def _online_update(vt, s_ref, mx_ref, b, m_ref, acc_ref, h):
    m_old = m_ref[h]
    m_new = jnp.maximum(m_old, mx_ref[b, h])
    p = jnp.exp2(s_ref[b, h] - m_new).astype(BF16)
    vt1 = jnp.concatenate([vt, jnp.ones((ONES_ROWS, vt.shape[1]), BF16)], axis=0)
    acc_ref[h] = jnp.exp2(m_old - m_new) * acc_ref[h] + _dot(vt1, p)
    m_ref[h] = m_new


def _normalised(acc_ref, h, dh):
    return acc_ref[h, 0:dh, :] / acc_ref[h, dh:dh + 1, :]


def _pipelined_sweep(i, n_heads, score, update, score_diag):
    every = range(n_heads)

    @pl.when(i >= 1)
    def _():
        score(0, 0, every)

    @pl.loop(0, i // 2)
    def _(j):
        for h in every:
            score(2 * j + 1, 1, [h])
            update(2 * j, 0, [h])
        for h in every:
            score(2 * j + 2, 0, [h])
            update(2 * j + 1, 1, [h])

    @pl.when(i % 2 == 1)
    def _():
        for h in every:
            score_diag(1, [h])
            update(i - 1, 0, [h])
        update(i, 1, every)

    @pl.when(i % 2 == 0)
    def _():
        score_diag(0, every)
        update(i, 0, every)


def _mla_attn_kernel(qt_ref, k_ref, vt_ref, o_ref, m_ref, acc_ref, s_ref, mx_ref):
    i = pl.program_id(1)
    tq = TQ_MLA
    m_ref[...] = jnp.full(m_ref.shape, -jnp.inf, F32)
    acc_ref[...] = jnp.zeros(acc_ref.shape, F32)

    every = range(MLA_HEADS)

    def score(kt, b, heads, keep=None):
        for h in heads:
            sl = slice(h * LANES, (h + 1) * LANES)
            k = k_ref[0, pl.ds(pl.multiple_of(kt * TKV, TKV), TKV), sl]
            s = _dot(k, qt_ref[0, 0, sl, :])
            if keep is not None:
                s = jnp.where(keep, s, NEG_INF)
            _stage_scores(s, s_ref, mx_ref, b, h)

    def update(kt, b, heads):
        for h in heads:
            _online_update(vt_ref[0, kt, h * MLA_V:(h + 1) * MLA_V, :], s_ref, mx_ref, b, m_ref, acc_ref, h)

    causal = lax.broadcasted_iota(jnp.int32, (TKV, tq), 0) <= lax.broadcasted_iota(jnp.int32, (TKV, tq), 1)
    _pipelined_sweep(i, MLA_HEADS, score, update, lambda b, heads: score(i, b, heads, causal))
    out = jnp.concatenate([_normalised(acc_ref, h, MLA_V) for h in range(MLA_HEADS)], axis=0)
    o_ref[0] = out.T.astype(BF16)


def _mla_attn(q, k, vt, B, S):
    nkt = S // TKV
    hw = MLA_HEADS * LANES
    return pl.pallas_call(
        _mla_attn_kernel,
        grid=(B, S // TQ_MLA),
        in_specs=[pl.BlockSpec((1, 1, hw, TQ_MLA), lambda b, i: (b, i, 0, 0)),
                  pl.BlockSpec((1, S, hw), lambda b, i: (b, 0, 0)),
                  pl.BlockSpec((1, nkt, MLA_HEADS * MLA_V, TKV), lambda b, i: (b, 0, 0, 0))],
        out_specs=pl.BlockSpec((1, TQ_MLA, MLA_HEADS * MLA_V), lambda b, i: (b, i, 0)),
        out_shape=jax.ShapeDtypeStruct((B, S, MLA_HEADS * MLA_V), BF16),
        scratch_shapes=[pltpu.VMEM((MLA_HEADS, 1, TQ_MLA), F32),
                        pltpu.VMEM((MLA_HEADS, MLA_V + ONES_ROWS, TQ_MLA), F32),
                        pltpu.VMEM((2, MLA_HEADS, TKV, TQ_MLA), F32), pltpu.VMEM((2, MLA_HEADS, 1, TQ_MLA), F32)],
        compiler_params=_cparams(("parallel", "arbitrary")),
        name="mla_attn",
    )(q, k, vt)


def _gelu_tanh(x):
    return 0.5 * x * (1.0 + jnp.tanh(np.sqrt(2.0 / np.pi).astype(np.float32) * (x + 0.044715 * (x * x * x))))


def _compress_kernel(kx_ref, vx_ref, pa_ref, pb_ref, wa_ref, wb_ref, w2k_ref, w2vt_ref, feat_ref, kc_ref, vct_ref):
    nc = kx_ref.shape[1]

    def hidden(x, kv):
        ya = _dot((x + pa_ref[0, kv]).astype(BF16), wa_ref[0, kv])
        yb = _dot((x + pb_ref[0, kv]).astype(BF16), wb_ref[0, kv])
        return _gelu_tanh(ya + pltpu.roll(yb, nc - 1, 0)).astype(BF16)

    kc_ref[0] = (_dot(hidden(kx_ref[0], 0), w2k_ref[0]) + feat_ref[...]).astype(BF16)
    vct_ref[0] = _dot_nt(w2vt_ref[0], hidden(vx_ref[0], 1)).astype(BF16)


def _compress(kx, vx, pa, pb, wa, wb, w2k, w2vt, feat, l):
    B, nc, width = kx.shape
    xs = pl.BlockSpec((1, nc, width), lambda b: (b, 0, 0))
    return pl.pallas_call(
        _compress_kernel,
        grid=(B,),
        in_specs=[xs, xs] + [_layer_spec(a, l) for a in (pa, pb, wa, wb, w2k, w2vt)] + [_const_spec(feat)],
        out_specs=[pl.BlockSpec((1, nc, 2 * LANES), lambda b: (b, 0, 0)),
                   pl.BlockSpec((1, 2 * NSA_DH, nc), lambda b: (b, 0, 0))],
        out_shape=[jax.ShapeDtypeStruct((B, nc, 2 * LANES), BF16),
                   jax.ShapeDtypeStruct((B, 2 * NSA_DH, nc), BF16)],
        compiler_params=_cparams(("parallel",)),
        name="nsa_compress",
    )(kx, vx, pa, pb, wa, wb, w2k, w2vt, feat)


def _nsa_attn_kernel(qnt_ref, kc_ref, vct_ref, ks_ref, kw_ref, vst_ref, vwt_ref, gt_ref, ovt_ref, oh_ref,
                     o_ref, ocmp_ref, m_ref, acc_ref, mw_ref, accw_ref, s_ref, mx_ref, qs_ref):
    i = pl.program_id(1)
    tq = TQ_NSA
    G, R, dh = NSA_KV_HEADS, NSA_REP, NSA_DH
    nc = kc_ref.shape[1]
    nblk = ovt_ref.shape[0]
    t0 = i * tq
    tpos = t0 + lax.broadcasted_iota(jnp.int32, (1, tq), 1)
    q_of = lambda h: qnt_ref[0, 0, h * LANES:(h + 1) * LANES, :]

    for ref in (m_ref, mw_ref):
        ref[...] = jnp.full(ref.shape, -jnp.inf, F32)
    for ref in (acc_ref, accw_ref):
        ref[...] = jnp.zeros(ref.shape, F32)

    every = range(G * R)
    rel = lax.broadcasted_iota(jnp.int32, (TKV, tq), 0) - lax.broadcasted_iota(jnp.int32, (TKV, tq), 1)
    causal = rel <= 0

    def score(kt, b, heads, sel, keep=None):
        rows = pl.ds(pl.multiple_of(kt * TKV, TKV), TKV)
        for h in heads:
            g = h // R
            if sel:
                k = jnp.concatenate([ks_ref[0, rows, g * LANES:(g + 1) * LANES], oh_ref[rows, :]], axis=1)
            else:
                k = kw_ref[0, rows, g * LANES:(g + 1) * LANES]
            s = _dot(k, qs_ref[h] if sel else q_of(h))
            if keep is not None:
                s = jnp.where(keep, s, NEG_INF)
            _stage_scores(s, s_ref, mx_ref, b, h)

    def update(kt, b, heads, sel):
        vt_ref, state = (vst_ref, (m_ref, acc_ref)) if sel else (vwt_ref, (mw_ref, accw_ref))
        for h in heads:
            g = h // R
            _online_update(vt_ref[0, kt, g * dh:(g + 1) * dh, :], s_ref, mx_ref, b, *state, h)

    assert W_WIN == 2 * TKV
    kt1, kt2 = jnp.maximum(i - 1, 0), jnp.maximum(i - 2, 0)
    keep1 = jnp.broadcast_to(i >= 1, (TKV, tq))
    keep2 = (rel > 0) & (i >= 2)

    nrow = lax.broadcasted_iota(jnp.int32, (nc, 1), 0)
    bias_c = jnp.where((tpos >= nrow * D_CMP + (L_CMP - 1)) & (nrow < nc - 1), 0.0, NEG_INF)
    has_cmp = tpos >= L_CMP - 1
    for h in every:
        g = h // R
        _stage_scores(_dot(kc_ref[0, :, g * LANES:(g + 1) * LANES], q_of(h)) + bias_c, s_ref, mx_ref, 0, h)
    score(i, 1, every, False, causal)
    for h in every:
        score(kt1, 2, [h], False, keep1)
        update(i, 1, [h], False)

    imps = [None] * G
    for h in every:
        g = h // R
        e = jnp.exp2(s_ref[0, h, 0:nc, :] - mx_ref[0, h])
        inv = jnp.where(has_cmp, 1.0 / jnp.sum(e, axis=0, keepdims=True), 0.0)
        both = _dot(jnp.concatenate([vct_ref[0, g * dh:(g + 1) * dh, :], ovt_ref[...]], axis=0), e.astype(BF16))
        ocmp_ref[h] = both[0:dh, :] * inv
        contrib = both[dh:dh + nblk, :] * inv
        imps[g] = contrib if imps[g] is None else imps[g] + contrib
        score(kt2, 3, [h], False, keep2)
        update(kt1, 2, [h], False)
    update(kt2, 3, every, False)

    cur = lax.shift_right_logical(tpos, 6)
    jrow = lax.broadcasted_iota(jnp.int32, (nblk, 1), 0)
    sub = lax.broadcasted_iota(jnp.int32, (8, 1), 0)
    n_slab = nblk // 8
    for g in range(G):
        imp = jnp.where(jrow > cur, -jnp.inf, imps[g])
        imp = jnp.where((jrow == 0) | (jrow == cur) | (jrow == cur - 1), jnp.inf, imp)
        slabs = [imp[8 * v:8 * v + 8, :] for v in range(n_slab)]
        ranks = [jnp.zeros((8, tq), F32) for _ in range(n_slab)]
        for jj in range(nblk):
            row = imp[jj:jj + 1, :]
            for v in range(n_slab):
                if v > jj // 8:
                    beats = row >= slabs[v]
                elif v < jj // 8:
                    beats = row > slabs[v]
                else:
                    beats = (row > slabs[v]) | ((row == slabs[v]) & (sub > jj % 8))
                ranks[v] = ranks[v] + jnp.where(beats, 1.0, 0.0)
        rank = jnp.concatenate(ranks, axis=0)
        mb = jnp.where(rank < float(min(N_SEL, nblk)), 0.0, NEG_INF)
        mb = jnp.concatenate([mb, jnp.zeros((LANES - nblk, tq), F32)], axis=0).astype(BF16)
        for r in range(R):
            qs_ref[g * R + r] = jnp.concatenate([q_of(g * R + r), mb], axis=0)

    _pipelined_sweep(i, G * R, functools.partial(score, sel=True), functools.partial(update, sel=True),
                     lambda b, heads: score(i, b, heads, True, causal))

    sig = 1.0 / (1.0 + jnp.exp(-gt_ref[0].T[0:G * 16, :]))
    outs = []
    for h in range(G * R):
        row = (h // R) * 16 + (h % R) * 4
        outs.append(sig[row:row + 1, :] * ocmp_ref[h] + sig[row + 1:row + 2, :] * _normalised(acc_ref, h, dh)
                    + sig[row + 2:row + 3, :] * _normalised(accw_ref, h, dh))
    o_ref[0] = jnp.concatenate(outs, axis=0).T.astype(BF16)


def _nsa_attn(qnt, kc, vct, ksw, vt, gt, ovt, oh, B, S):
    nkt = S // TKV
    nc = kc.shape[1]
    G = NSA_KV_HEADS
    ksw3 = ksw.reshape(B, S, W_KS)
    gt3 = gt.reshape(B, S, LANES)
    H, dh, tq = NSA_HEADS, NSA_DH, TQ_NSA
    state = [pltpu.VMEM((H, 1, tq), F32), pltpu.VMEM((H, dh + ONES_ROWS, tq), F32)]
    return pl.pallas_call(
        _nsa_attn_kernel,
        grid=(B, S // tq),
        in_specs=[pl.BlockSpec((1, 1, W_QN, tq), lambda b, i: (b, i, 0, 0)),
                  pl.BlockSpec((1, nc, G * LANES), lambda b, i: (b, 0, 0)),
                  pl.BlockSpec((1, G * dh, nc), lambda b, i: (b, 0, 0)),
                  pl.BlockSpec((1, S, G * LANES), lambda b, i: (b, 0, 0)),
                  pl.BlockSpec((1, S, G * LANES), lambda b, i: (b, 0, 1)),
                  pl.BlockSpec((1, nkt, G * dh, TKV), lambda b, i: (b, 0, 0, 0)),
                  pl.BlockSpec((1, nkt, G * dh, TKV), lambda b, i: (b, 0, 1, 0)),
                  pl.BlockSpec((1, tq, LANES), lambda b, i: (b, i, 0)),
                  _const_spec(ovt), _const_spec(oh)],
        out_specs=pl.BlockSpec((1, tq, H * dh), lambda b, i: (b, i, 0)),
        out_shape=jax.ShapeDtypeStruct((B, S, H * dh), BF16),
        scratch_shapes=([pltpu.VMEM((H, dh, tq), F32)] + state + state
                        + [pltpu.VMEM((4, H, TKV, tq), F32), pltpu.VMEM((4, H, 1, tq), F32),
                           pltpu.VMEM((H, 2 * LANES, tq), BF16)]),
        compiler_params=_cparams(("parallel", "arbitrary")),
        name="nsa_attn",
    )(qnt, kc, vct, ksw3, ksw3, vt, vt, gt3, ovt, oh)


FF_CHUNK = 256
HALO = 16


def _ffn_kernel(x_ref, xp_ref, a_ref, ap_ref, b_ref, bp_ref, wo_ref, g_ref, wup_ref, cw_ref, cb_ref, wdn_ref, fg_ref,
                o_ref, n_ref, h_ref, act_ref, *, final, spt):
    i = pl.program_id(0)
    ka = a_ref.shape[1]

    def attn_residual(h_rows, a_rows, b_rows):
        return h_rows[...] + _dot(a_rows[...], wo_ref[0, 0:ka, :]) + _dot(b_rows[...], wo_ref[0, ka:, :])

    o_ref[...] = attn_residual(x_ref, a_ref, b_ref)
    n_ref[0:HALO, :] = _rms(attn_residual(xp_ref, ap_ref, bp_ref), g_ref[0]).astype(BF16)
    n_ref[HALO:HALO + TM, :] = _rms(o_ref[...], g_ref[0]).astype(BF16)
    keep = jnp.where(i % spt == 0, 0.0, 1.0)

    def conv(slot, col):
        hx = _dot(n_ref[...], wup_ref[0, :, col:col + FF_CHUNK])
        h_ref[slot, 0:HALO, :] = hx[0:HALO, :] * keep
        h_ref[slot, HALO:HALO + TM, :] = hx[HALO:HALO + TM, :]
        out = cb_ref[0, :, col:col + FF_CHUNK]
        for j in range(CONV_W):
            lo = HALO - (CONV_W - 1) + j
            out = out + h_ref[slot, lo:lo + TM, :] * cw_ref[0, j:j + 1, col:col + FF_CHUNK]
        return out

    for c in range(D_FF // FF_CHUNK):
        col = c * FF_CHUNK
        gate = conv(0, col)
        up = conv(1, D_FF + col)
        act_ref[:, col:col + FF_CHUNK] = (gate * (1.0 / (1.0 + jnp.exp(-gate))) * up).astype(BF16)
    out = o_ref[...] + _dot(act_ref[...], wdn_ref[0])
    if final:
        out = _rms(out, fg_ref[...])
    o_ref[...] = out


def _outproj_ffn(h, oa, ob, wo, g, wup, cw, cb, wdn, fg, l, S, final):
    M = h.shape[0]
    spt = S // TM
    row = lambda width: pl.BlockSpec((TM, width), lambda i: (i, 0))
    prev = lambda width: pl.BlockSpec((HALO, width), lambda i: (jnp.maximum(i * (TM // HALO) - 1, 0), 0))
    return pl.pallas_call(
        functools.partial(_ffn_kernel, final=final, spt=spt),
        grid=(M // TM,),
        in_specs=[row(D_MODEL), prev(D_MODEL), row(oa.shape[1]), prev(oa.shape[1]), row(ob.shape[1]),
                  prev(ob.shape[1]), _layer_spec(wo, l), _layer_spec(g, l), _layer_spec(wup, l), _layer_spec(cw, l),
                  _layer_spec(cb, l), _layer_spec(wdn, l), _const_spec(fg)],
        out_specs=row(D_MODEL),
        out_shape=jax.ShapeDtypeStruct((M, D_MODEL), F32),
        scratch_shapes=[pltpu.VMEM((HALO + TM, D_MODEL), BF16), pltpu.VMEM((2, HALO + TM, FF_CHUNK), F32),
                        pltpu.VMEM((TM, D_FF), BF16)],
        compiler_params=_cparams(("parallel",)),
        name="outproj_ffn",
    )(h, h, oa, oa, ob, ob, wo, g, wup, cw, cb, wdn, fg)


def _slabs(w, n, width, pad_to=LANES):
    lead = w.shape[:-1]
    w = w.reshape(lead + (n, width))
    w = jnp.pad(w, [(0, 0)] * (w.ndim - 1) + [(0, pad_to - width)])
    return w.reshape(lead + (n * pad_to,))


def _inproj_weights(w_in):
    G, dh = NSA_KV_HEADS, NSA_DH
    cols = lambda off, n: w_in[..., off:off + n]
    zeros = lambda n: jnp.zeros(w_in.shape[:-1] + (n,), w_in.dtype)
    kr = jnp.concatenate([zeros(MLA_NOPE), cols(OFF_KROPE, MLA_ROPE), zeros(LANES - MLA_QK)], axis=-1)
    gates = _slabs(cols(OFF_GT, 3 * NSA_HEADS), NSA_HEADS, 3, 4)
    gates = jnp.concatenate([gates, zeros(LANES - 4 * NSA_HEADS)], axis=-1)
    w = jnp.concatenate([cols(OFF_CQ, MLA_Q_RANK), cols(OFF_CKV, MLA_KV_RANK), kr,
                         gates, cols(OFF_KC, G * dh), cols(OFF_VC, G * dh),
                         _slabs(cols(OFF_KS, G * dh), G, dh), _slabs(cols(OFF_KW, G * dh), G, dh)], axis=-1)
    assert w.shape[-1] == W_ROW
    qn = cols(OFF_QN, NSA_HEADS * dh) * (NSA_DH ** -0.5 * LOG2E)
    wvt = jnp.concatenate([cols(OFF_VS, G * dh), cols(OFF_VW, G * dh)], axis=-1)
    return w.astype(BF16), jnp.swapaxes(qn, 1, 2).astype(BF16), jnp.swapaxes(wvt, 1, 2).astype(BF16)


def _bf16_terms(x, n):
    terms = []
    for _ in range(n):
        bits = np.array(x, np.float32).view(np.uint32)
        bits = (bits + np.uint32(0x7FFF) + ((bits >> np.uint32(16)) & np.uint32(1))) & np.uint32(0xFFFF0000)
        terms.append(float(bits.view(np.float32)))
        x = x - terms[-1]
    return terms


N_POS_TERMS = 3


def _pos_features(pos, n_slabs):
    out = np.zeros((len(pos), n_slabs * LANES), np.float32)
    for s in range(n_slabs):
        base = s * LANES + NSA_DH
        out[:, base:base + N_POS_TERMS] = (pos // L_SEL)[:, None]
        out[:, base + N_POS_TERMS:base + 2 * N_POS_TERMS] = (pos % L_SEL)[:, None]
    return out


def _const_tables(S):
    slopes = 2.0 ** (-8.0 * np.arange(1, NSA_HEADS + 1) / NSA_HEADS)
    terms = np.array(_bf16_terms(LOG2E, N_POS_TERMS))
    pad = LANES - NSA_DH
    qaug = np.zeros((NSA_HEADS * pad, 1), np.float32)
    for h in range(NSA_HEADS):
        base = h * pad
        qaug[base:base + N_POS_TERMS, 0] = slopes[h] * L_SEL * terms
        qaug[base + N_POS_TERMS:base + 2 * N_POS_TERMS, 0] = slopes[h] * terms
    posf = _pos_features(np.arange(S), 4)
    nc = S // D_CMP
    feat = _pos_features(np.arange(nc) * D_CMP + L_CMP - 1, 2)
    n_cmp = (S - L_CMP) // D_CMP + 1
    nblk = S // L_SEL
    cs = np.arange(n_cmp) * D_CMP
    ss = np.arange(nblk) * L_SEL
    ov = ((cs[:, None] < ss[None, :] + L_SEL) & (cs[:, None] + L_CMP > ss[None, :])).astype(np.float32)
    ovt = np.zeros((nblk, nc), np.float32)
    ovt[:, :n_cmp] = ov.T
    assert nblk <= LANES
    onehot = (np.arange(S)[:, None] // L_SEL == np.arange(LANES)[None, :]).astype(np.float32)
    return (jnp.asarray(qaug), jnp.asarray(posf), jnp.asarray(feat), jnp.asarray(ovt, BF16),
            jnp.asarray(onehot, BF16))


def _rope_consts(S):
    half = MLA_ROPE // 2
    inv = (1.0 / (np.float32(ROPE_THETA) ** (np.arange(0, MLA_ROPE, 2, dtype=np.float32) / MLA_ROPE))).astype(np.float32)
    ang = np.arange(S, dtype=np.float32)[:, None] * inv[None, :]
    cos, sin = np.cos(ang), np.sin(ang)
    z = lambda w: np.zeros((S, w), np.float32)
    ct = np.concatenate([np.ones((S, MLA_NOPE), np.float32), cos, cos, z(LANES - MLA_QK)], axis=1)
    sa = np.concatenate([z(MLA_NOPE), -sin, z(LANES - MLA_NOPE - half)], axis=1)
    sb = np.concatenate([z(MLA_NOPE + half), sin, z(LANES - MLA_QK)], axis=1)
    return tuple(jnp.asarray(np.ascontiguousarray(t)) for t in (ct, sa, sb, cos.T, sin.T))


def _mla_weights(w_uq, w_ukv):
    wq = _slabs(w_uq, MLA_HEADS, MLA_QK) * (MLA_QK ** -0.5 * LOG2E)
    kv = w_ukv.reshape(w_ukv.shape[:-1] + (MLA_HEADS, MLA_NOPE + MLA_V))
    wk = _slabs(kv[..., :MLA_NOPE].reshape(w_ukv.shape[:-1] + (MLA_HEADS * MLA_NOPE,)), MLA_HEADS, MLA_NOPE)
    wvt = jnp.swapaxes(kv[..., MLA_NOPE:].reshape(w_ukv.shape[:-1] + (MLA_HEADS * MLA_V,)), 1, 2)
    return jnp.swapaxes(wq, 1, 2).astype(BF16), wk.astype(BF16), wvt.astype(BF16)


def _compress_weights(pos, w1, w2k, w2v):
    G, dh, half = NSA_KV_HEADS, NSA_DH, L_CMP // 2
    L = pos.shape[0]
    eye = jnp.eye(G, dtype=F32)

    def first_layer(w):
        return jnp.einsum('nclde,gk->nclgdke', w, eye).reshape(L, 2, half * G * dh, G * CMP_HID).astype(BF16)

    def pos_row(p):
        return jnp.broadcast_to(p[:, :, :, None, :], (L, 2, half, G, dh)).reshape(L, 2, 1, half * G * dh)

    w2kp = jnp.pad(w2k, ((0, 0), (0, 0), (0, LANES - dh)))
    w2k_bd = jnp.einsum('ned,gk->ngekd', w2kp, eye).reshape(L, G * CMP_HID, G * LANES).astype(BF16)
    w2vt_bd = jnp.einsum('ned,gk->ngdke', w2v, eye).reshape(L, G * dh, G * CMP_HID).astype(BF16)
    return (pos_row(pos[:, :, :half]), pos_row(pos[:, :, half:]), first_layer(w1[:, :, :half]),
            first_layer(w1[:, :, half:]), w2k_bd, w2vt_bd)


def kernel(x, attn_norm, w_in, q_norm, kv_norm, w_uq, w_ukv, cmp_pos_k, cmp_pos_v, cmp_k_w1, cmp_k_w2, cmp_v_w1,
           cmp_v_w2, w_o, ffn_norm, w_up, conv_w, conv_b, w_down, final_norm):
    B, S, D = x.shape
    M = B * S
    assert D == D_MODEL and S % TM == 0 and TQ_MLA == TKV and TQ_NSA == TKV and S % TKV == 0
    qaug, posf, feat, ovt, onehot = _const_tables(S)
    rope = _rope_consts(S)
    nc = S // D_CMP

    w_row, w_qt, w_vt = _inproj_weights(w_in)
    mqt, mk, mvt = _mla_weights(w_uq, w_ukv)
    pa, pb, cwa, cwb, w2k, w2vt = _compress_weights(jnp.stack([cmp_pos_k, cmp_pos_v], axis=1),
                                                    jnp.stack([cmp_k_w1, cmp_v_w1], axis=1), cmp_k_w2, cmp_v_w2)
    wo, wup, wdn = w_o.astype(BF16), w_up.astype(BF16), w_down.astype(BF16)
    g_attn, g_q, g_kv, g_ffn = (a[:, None, :] for a in (attn_norm, q_norm, kv_norm, ffn_norm))
    cb = conv_b[:, None, :]
    fg = final_norm[None, :]

    h = x.reshape(M, D)
    for l in range(DEPTH):
        gt, kcr, vcr, qnt, ksw, vt, qt, k, vmt = _inproj(h, g_attn, w_row, w_qt, w_vt, qaug, posf,
                                                          (g_q, g_kv, mqt, mk, mvt), rope, l, B, S)
        o_mla = _mla_attn(qt, k.reshape(B, S, -1), vmt, B, S)
        kc, vct = _compress(kcr.reshape(B, nc, -1), vcr.reshape(B, nc, -1), pa, pb, cwa, cwb, w2k, w2vt, feat, l)
        o_nsa = _nsa_attn(qnt, kc, vct, ksw, vt, gt, ovt, onehot, B, S)
        h = _outproj_ffn(h, o_mla.reshape(M, -1), o_nsa.reshape(M, -1), wo, g_ffn, wup, conv_w, cb, wdn, fg, l, S,
                         final=(l == DEPTH - 1))
    return h.reshape(B, S, D)
```

```python
import functools

import numpy as np
import jax
import jax.numpy as jnp
from jax import lax
from jax.experimental import pallas as pl
from jax.experimental.pallas import tpu as pltpu

F32 = jnp.float32
BF16 = jnp.bfloat16

D_MODEL = 1024
DEPTH = 2
EPS = 1e-6
NEG_INF = -1e30
LOG2E = float(np.log2(np.e))
LANES = 128

MLA_HEADS = 8
MLA_NOPE = 64
MLA_ROPE = 32
MLA_V = 64
MLA_Q_RANK = 384
MLA_KV_RANK = 256
MLA_QK = MLA_NOPE + MLA_ROPE
ROPE_THETA = 10000.0

NSA_HEADS = 8
NSA_KV_HEADS = 2
NSA_REP = NSA_HEADS // NSA_KV_HEADS
NSA_DH = 64
L_CMP = 32
D_CMP = 16
CMP_HID = 128
L_SEL = 64
N_SEL = 16
W_WIN = 512

D_FF = 2816
CONV_W = 3

_SIZES = [MLA_Q_RANK, MLA_KV_RANK, MLA_ROPE, NSA_HEADS * NSA_DH] + [NSA_KV_HEADS * NSA_DH] * 6 + [3 * NSA_HEADS]
_OFFS = [0] + [int(o) for o in np.cumsum(_SIZES)]
(OFF_CQ, OFF_CKV, OFF_KROPE, OFF_QN, OFF_KC, OFF_VC, OFF_KS, OFF_VS, OFF_KW, OFF_VW, OFF_GT, D_IN) = _OFFS

TM = 512
TKV = 256
TQ_MLA = 256
TQ_NSA = 256
VMEM_LIMIT = 56 * 1024 * 1024

W_ZA = 6 * LANES
W_ZB = 3 * LANES
W_KS = 4 * LANES
W_ROW = W_ZA + W_ZB + W_KS
W_QN = NSA_HEADS * LANES


def _cparams(sem):
    return pltpu.CompilerParams(dimension_semantics=sem, vmem_limit_bytes=VMEM_LIMIT)


def _layer_spec(a, l):
    return pl.BlockSpec((1,) + a.shape[1:], lambda *_: (l,) + (0,) * (a.ndim - 1))


def _const_spec(a):
    return pl.BlockSpec(a.shape, lambda *_: (0,) * a.ndim)


def _dot(a, b):
    return jnp.dot(a, b, preferred_element_type=F32)


def _dot_nt(a, b):
    return lax.dot_general(a, b, (((1,), (1,)), ((), ())), preferred_element_type=F32)


def _rms(x, g):
    return x * lax.rsqrt(jnp.mean(x * x, axis=-1, keepdims=True) + EPS) * g


def _store_tiles(ref, xt):
    for c in range(TM // TKV):
        ref[0, c] = xt[:, c * TKV:(c + 1) * TKV]


def _inproj_kernel(x_ref, g_ref, w_ref, wqt_ref, wvt_ref, qaug_ref, posf_ref,
                   qg_ref, kg_ref, mwqt_ref, mwk_ref, mwvt_ref, ct_ref, sa_ref, sb_ref, cost_ref, sint_ref,
                   gt_ref, kcr_ref, vcr_ref, qnt_ref, ksw_ref, vt_ref, mqt_ref, mk_ref, mvt_ref):
    n = _rms(x_ref[...], g_ref[0]).astype(BF16)
    c0, c1 = W_ZA, W_ZA + W_ZB
    _mla_prep(_dot(n, w_ref[0, :, 0:c0]), qg_ref, kg_ref, mwqt_ref, mwk_ref, mwvt_ref,
              ct_ref, sa_ref, sb_ref, cost_ref, sint_ref, mqt_ref, mk_ref, mvt_ref)
    zb = _dot(n, w_ref[0, :, c0:c1])
    gt_ref[...] = zb[:, 0:LANES]
    kcr_ref[...] = zb[:, LANES:2 * LANES]
    vcr_ref[...] = zb[:, 2 * LANES:3 * LANES]
    ksw_ref[...] = (_dot(n, w_ref[0, :, c1:W_ROW]) + posf_ref[...]).astype(BF16)
    qt = _dot_nt(wqt_ref[0], n)
    slabs, pad = [], LANES - NSA_DH
    for h in range(NSA_HEADS):
        slabs += [qt[h * NSA_DH:(h + 1) * NSA_DH, :], jnp.broadcast_to(qaug_ref[h * pad:(h + 1) * pad, :], (pad, TM))]
    _store_tiles(qnt_ref, jnp.concatenate(slabs, axis=0).astype(BF16))
    _store_tiles(vt_ref, _dot_nt(wvt_ref[0], n).astype(BF16))


def _inproj(h, g, w, wqt, wvt, qaug, posf, mla, rope, l, B, S):
    M = B * S
    spt = S // TM
    ct, sa, sb, cost, sint = rope
    row = lambda width: pl.BlockSpec((TM, width), lambda i: (i, 0))
    tiles = lambda rows: pl.BlockSpec((1, TM // TKV, rows, TKV), lambda i: (i // spt, i % spt, 0, 0))
    tab = pl.BlockSpec((TM, LANES), lambda i: (i % spt, 0))
    tabt = pl.BlockSpec((MLA_ROPE // 2, TM), lambda i: (0, i % spt))
    return pl.pallas_call(
        _inproj_kernel,
        grid=(M // TM,),
        in_specs=[row(D_MODEL), _layer_spec(g, l), _layer_spec(w, l), _layer_spec(wqt, l), _layer_spec(wvt, l),
                  _const_spec(qaug), pl.BlockSpec((TM, W_KS), lambda i: (i % spt, 0))]
                 + [_layer_spec(a, l) for a in mla] + [tab, tab, tab, tabt, tabt],
        out_specs=[row(LANES), row(LANES), row(LANES), tiles(W_QN), row(W_KS), tiles(4 * NSA_DH),
                   tiles(MLA_HEADS * LANES), row(MLA_HEADS * LANES), tiles(MLA_HEADS * MLA_V)],
        out_shape=[jax.ShapeDtypeStruct((M, LANES), F32),
                   jax.ShapeDtypeStruct((M, LANES), F32),
                   jax.ShapeDtypeStruct((M, LANES), F32),
                   jax.ShapeDtypeStruct((B, S // TKV, W_QN, TKV), BF16),
                   jax.ShapeDtypeStruct((M, W_KS), BF16),
                   jax.ShapeDtypeStruct((B, S // TKV, 4 * NSA_DH, TKV), BF16),
                   jax.ShapeDtypeStruct((B, S // TKV, MLA_HEADS * LANES, TKV), BF16),
                   jax.ShapeDtypeStruct((M, MLA_HEADS * LANES), BF16),
                   jax.ShapeDtypeStruct((B, S // TKV, MLA_HEADS * MLA_V, TKV), BF16)],
        compiler_params=_cparams(("parallel",)),
        name="attn_inproj",
    )(h, g, w, wqt, wvt, qaug, posf, *mla, ct, sa, sb, cost, sint)


def _rope_slab(x, ct, sa, sb):
    return x * ct + pltpu.roll(x, LANES - MLA_ROPE // 2, 1) * sa + pltpu.roll(x, MLA_ROPE // 2, 1) * sb


def _mla_prep(za, qg_ref, kg_ref, wqt_ref, wk_ref, wvt_ref, ct_ref, sa_ref, sb_ref, cost_ref, sint_ref,
              qt_ref, k_ref, vt_ref):
    cq = _rms(za[:, 0:MLA_Q_RANK], qg_ref[0]).astype(BF16)
    ckv = _rms(za[:, MLA_Q_RANK:MLA_Q_RANK + MLA_KV_RANK], kg_ref[0]).astype(BF16)
    kr = _rope_slab(za[:, 5 * LANES:6 * LANES], ct_ref[...], sa_ref[...], sb_ref[...])
    k = _dot(ckv, wk_ref[0])
    for h in range(MLA_HEADS):
        sl = slice(h * LANES, (h + 1) * LANES)
        k_ref[:, sl] = (k[:, sl] + kr).astype(BF16)
    qt = _dot_nt(wqt_ref[0], cq)
    cos, sin = cost_ref[...], sint_ref[...]
    half = MLA_ROPE // 2
    slabs = []
    for h in range(MLA_HEADS):
        base = h * LANES
        x1 = qt[base + MLA_NOPE:base + MLA_NOPE + half, :]
        x2 = qt[base + MLA_NOPE + half:base + MLA_QK, :]
        slabs += [qt[base:base + MLA_NOPE, :], x1 * cos - x2 * sin, x2 * cos + x1 * sin,
                  qt[base + MLA_QK:base + LANES, :]]
    _store_tiles(qt_ref, jnp.concatenate(slabs, axis=0).astype(BF16))
    _store_tiles(vt_ref, _dot_nt(wvt_ref[0], ckv).astype(BF16))


ONES_ROWS = 16


def _stage_scores(s, s_ref, mx_ref, b, h):
    mx_ref[b, h] = jnp.max(s, axis=0, keepdims=True)
    s_ref[b, h, 0:s.shape[0], :] = s


def _online_update(vt, s_ref, mx_ref, b, m_ref, acc_ref, h):
    m_old = m_ref[h]
    m_new = jnp.maximum(m_old, mx_ref[b, h])
    p = jnp.exp2(s_ref[b, h] - m_new).astype(BF16)
    vt1 = jnp.concatenate([vt, jnp.ones((ONES_ROWS, vt.shape[1]), BF16)], axis=0)
    acc_ref[h] = jnp.exp2(m_old - m_new) * acc_ref[h] + _dot(vt1, p)
    m_ref[h] = m_new


def _normalised(acc_ref, h, dh):
    return acc_ref[h, 0:dh, :] / acc_ref[h, dh:dh + 1, :]


def _pipelined_sweep(i, n_heads, score, update, score_diag, tile0_staged=False):
    every = range(n_heads)

    if not tile0_staged:
        @pl.when(i >= 1)
        def _():
            score(0, 0, every)

    @pl.loop(0, i // 2)
    def _(j):
        for h in every:
            score(2 * j + 1, 1, [h])
            update(2 * j, 0, [h])
        for h in every:
            score(2 * j + 2, 0, [h])
            update(2 * j + 1, 1, [h])

    @pl.when(i % 2 == 1)
    def _():
        for h in every:
            score_diag(1, [h])
            update(i - 1, 0, [h])
        update(i, 1, every)

    @pl.when(i % 2 == 0)
    def _():
        score_diag(0, every)
        update(i, 0, every)


def _mla_attn_kernel(qt_ref, k_ref, vt_ref, o_ref, m_ref, acc_ref, s_ref, mx_ref):
    i = pl.program_id(1)
    tq = TQ_MLA
    m_ref[...] = jnp.full(m_ref.shape, -jnp.inf, F32)
    acc_ref[...] = jnp.zeros(acc_ref.shape, F32)

    every = range(MLA_HEADS)

    def score(kt, b, heads, keep=None):
        for h in heads:
            sl = slice(h * LANES, (h + 1) * LANES)
            k = k_ref[0, pl.ds(pl.multiple_of(kt * TKV, TKV), TKV), sl]
            s = _dot(k, qt_ref[0, 0, sl, :])
            if keep is not None:
                s = jnp.where(keep, s, NEG_INF)
            _stage_scores(s, s_ref, mx_ref, b, h)

    def update(kt, b, heads):
        for h in heads:
            _online_update(vt_ref[0, kt, h * MLA_V:(h + 1) * MLA_V, :], s_ref, mx_ref, b, m_ref, acc_ref, h)

    causal = lax.broadcasted_iota(jnp.int32, (TKV, tq), 0) <= lax.broadcasted_iota(jnp.int32, (TKV, tq), 1)
    _pipelined_sweep(i, MLA_HEADS, score, update, lambda b, heads: score(i, b, heads, causal))
    out = jnp.concatenate([_normalised(acc_ref, h, MLA_V) for h in range(MLA_HEADS)], axis=0)
    o_ref[0] = out.T.astype(BF16)


def _mla_attn(q, k, vt, B, S):
    nkt = S // TKV
    hw = MLA_HEADS * LANES
    return pl.pallas_call(
        _mla_attn_kernel,
        grid=(B, S // TQ_MLA),
        in_specs=[pl.BlockSpec((1, 1, hw, TQ_MLA), lambda b, i: (b, i, 0, 0)),
                  pl.BlockSpec((1, S, hw), lambda b, i: (b, 0, 0)),
                  pl.BlockSpec((1, nkt, MLA_HEADS * MLA_V, TKV), lambda b, i: (b, 0, 0, 0))],
        out_specs=pl.BlockSpec((1, TQ_MLA, MLA_HEADS * MLA_V), lambda b, i: (b, i, 0)),
        out_shape=jax.ShapeDtypeStruct((B, S, MLA_HEADS * MLA_V), BF16),
        scratch_shapes=[pltpu.VMEM((MLA_HEADS, 1, TQ_MLA), F32),
                        pltpu.VMEM((MLA_HEADS, MLA_V + ONES_ROWS, TQ_MLA), F32),
                        pltpu.VMEM((2, MLA_HEADS, TKV, TQ_MLA), F32), pltpu.VMEM((2, MLA_HEADS, 1, TQ_MLA), F32)],
        compiler_params=_cparams(("parallel", "arbitrary")),
        name="mla_attn",
    )(q, k, vt)


def _gelu_tanh(x):
    return 0.5 * x * (1.0 + jnp.tanh(np.sqrt(2.0 / np.pi).astype(np.float32) * (x + 0.044715 * (x * x * x))))


def _compress_kernel(kx_ref, vx_ref, pa_ref, pb_ref, wa_ref, wb_ref, w2k_ref, w2vt_ref, feat_ref, kc_ref, vct_ref):
    half = L_CMP // 2
    nc = kx_ref.shape[1] // half

    def hidden(x_ref, kv):
        x = jnp.concatenate([x_ref[0, pl.ds(l, nc, stride=half), :] for l in range(half)], axis=1)
        ya = _dot((x + pa_ref[0, kv]).astype(BF16), wa_ref[0, kv])
        yb = _dot((x + pb_ref[0, kv]).astype(BF16), wb_ref[0, kv])
        return _gelu_tanh(ya + pltpu.roll(yb, nc - 1, 0)).astype(BF16)

    kc_ref[0] = (_dot(hidden(kx_ref, 0), w2k_ref[0]) + feat_ref[...]).astype(BF16)
    vct_ref[0] = _dot_nt(w2vt_ref[0], hidden(vx_ref, 1)).astype(BF16)


def _compress(kx, vx, pa, pb, wa, wb, w2k, w2vt, feat, l):
    B, S, width = kx.shape
    nc = S // (L_CMP // 2)
    xs = pl.BlockSpec((1, S, width), lambda b: (b, 0, 0))
    return pl.pallas_call(
        _compress_kernel,
        grid=(B,),
        in_specs=[xs, xs] + [_layer_spec(a, l) for a in (pa, pb, wa, wb, w2k, w2vt)] + [_const_spec(feat)],
        out_specs=[pl.BlockSpec((1, nc, 2 * LANES), lambda b: (b, 0, 0)),
                   pl.BlockSpec((1, 2 * NSA_DH, nc), lambda b: (b, 0, 0))],
        out_shape=[jax.ShapeDtypeStruct((B, nc, 2 * LANES), BF16),
                   jax.ShapeDtypeStruct((B, 2 * NSA_DH, nc), BF16)],
        compiler_params=_cparams(("parallel",)),
        name="nsa_compress",
    )(kx, vx, pa, pb, wa, wb, w2k, w2vt, feat)


def _nsa_attn_kernel(qnt_ref, kc_ref, vct_ref, ks_ref, kw_ref, vst_ref, vwt_ref, gt_ref, ovt_ref, oh_ref,
                     o_ref, ocmp_ref, m_ref, acc_ref, mw_ref, accw_ref, s_ref, mx_ref, qs_ref):
    i = pl.program_id(1)
    tq = TQ_NSA
    G, R, dh = NSA_KV_HEADS, NSA_REP, NSA_DH
    nc = kc_ref.shape[1]
    nblk = ovt_ref.shape[0]
    t0 = i * tq
    tpos = t0 + lax.broadcasted_iota(jnp.int32, (1, tq), 1)
    q_of = lambda h: qnt_ref[0, 0, h * LANES:(h + 1) * LANES, :]

    for ref in (m_ref, mw_ref):
        ref[...] = jnp.full(ref.shape, -jnp.inf, F32)
    for ref in (acc_ref, accw_ref):
        ref[...] = jnp.zeros(ref.shape, F32)

    every = range(G * R)
    rel = lax.broadcasted_iota(jnp.int32, (TKV, tq), 0) - lax.broadcasted_iota(jnp.int32, (TKV, tq), 1)
    causal = rel <= 0

    def score(kt, b, heads, sel, keep=None):
        rows = pl.ds(pl.multiple_of(kt * TKV, TKV), TKV)
        for h in heads:
            g = h // R
            if sel:
                k = jnp.concatenate([ks_ref[0, rows, g * LANES:(g + 1) * LANES], oh_ref[rows, :]], axis=1)
            else:
                k = kw_ref[0, rows, g * LANES:(g + 1) * LANES]
            s = _dot(k, qs_ref[h] if sel else q_of(h))
            if keep is not None:
                s = jnp.where(keep, s, NEG_INF)
            _stage_scores(s, s_ref, mx_ref, b, h)

    def update(kt, b, heads, sel):
        vt_ref, state = (vst_ref, (m_ref, acc_ref)) if sel else (vwt_ref, (mw_ref, accw_ref))
        for h in heads:
            g = h // R
            _online_update(vt_ref[0, kt, g * dh:(g + 1) * dh, :], s_ref, mx_ref, b, *state, h)

    assert W_WIN == 2 * TKV
    kt1, kt2 = jnp.maximum(i - 1, 0), jnp.maximum(i - 2, 0)
    keep1 = jnp.broadcast_to(i >= 1, (TKV, tq))
    keep2 = (rel > 0) & (i >= 2)

    nrow = lax.broadcasted_iota(jnp.int32, (nc, 1), 0)
    bias_c = jnp.where((tpos >= nrow * D_CMP + (L_CMP - 1)) & (nrow < nc - 1), 0.0, NEG_INF)
    has_cmp = tpos >= L_CMP - 1
    for h in every:
        g = h // R
        _stage_scores(_dot(kc_ref[0, :, g * LANES:(g + 1) * LANES], q_of(h)) + bias_c, s_ref, mx_ref, 0, h)
    score(i, 1, every, False, causal)
    for h in every:
        score(kt1, 2, [h], False, keep1)
        update(i, 1, [h], False)

    imps = [None] * G
    for h in every:
        g = h // R
        e = jnp.exp2(s_ref[0, h, 0:nc, :] - mx_ref[0, h])
        inv = jnp.where(has_cmp, 1.0 / jnp.sum(e, axis=0, keepdims=True), 0.0)
        both = _dot(jnp.concatenate([vct_ref[0, g * dh:(g + 1) * dh, :], ovt_ref[...]], axis=0), e.astype(BF16))
        ocmp_ref[h] = both[0:dh, :] * inv
        contrib = both[dh:dh + nblk, :] * inv
        imps[g] = contrib if imps[g] is None else imps[g] + contrib
        score(kt2, 3, [h], False, keep2)
        update(kt1, 2, [h], False)
    update(kt2, 3, every, False)

    cur = lax.shift_right_logical(tpos, 6)
    jrow = lax.broadcasted_iota(jnp.int32, (nblk, 1), 0)
    sub = lax.broadcasted_iota(jnp.int32, (8, 1), 0)
    n_slab = nblk // 8
    for g in range(G):
        imp = jnp.where(jrow > cur, -jnp.inf, imps[g])
        imp = jnp.where((jrow == 0) | (jrow == cur) | (jrow == cur - 1), jnp.inf, imp)
        slabs = [imp[8 * v:8 * v + 8, :] for v in range(n_slab)]
        ranks = [jnp.zeros((8, tq), F32) for _ in range(n_slab)]
        for jj in range(nblk):
            row = imp[jj:jj + 1, :]
            for v in range(n_slab):
                if v > jj // 8:
                    beats = row >= slabs[v]
                elif v < jj // 8:
                    beats = row > slabs[v]
                else:
                    beats = (row > slabs[v]) | ((row == slabs[v]) & (sub > jj % 8))
                ranks[v] = ranks[v] + jnp.where(beats, 1.0, 0.0)
        rank = jnp.concatenate(ranks, axis=0)
        mb = jnp.where(rank < float(min(N_SEL, nblk)), 0.0, NEG_INF)
        mb = jnp.concatenate([mb, jnp.zeros((LANES - nblk, tq), F32)], axis=0).astype(BF16)
        for r in range(R):
            qs_ref[g * R + r] = jnp.concatenate([q_of(g * R + r), mb], axis=0)
        score(0, 0, range(g * R, (g + 1) * R), True)

    _pipelined_sweep(i, G * R, functools.partial(score, sel=True), functools.partial(update, sel=True),
                     lambda b, heads: score(i, b, heads, True, causal), tile0_staged=True)

    sig = 1.0 / (1.0 + jnp.exp(-gt_ref[0].T[0:G * 16, :]))
    outs = []
    for h in range(G * R):
        row = (h // R) * 16 + (h % R) * 4
        outs.append(sig[row:row + 1, :] * ocmp_ref[h] + sig[row + 1:row + 2, :] * _normalised(acc_ref, h, dh)
                    + sig[row + 2:row + 3, :] * _normalised(accw_ref, h, dh))
    o_ref[0] = jnp.concatenate(outs, axis=0).T.astype(BF16)


def _nsa_attn(qnt, kc, vct, ksw, vt, gt, ovt, oh, B, S):
    nkt = S // TKV
    nc = kc.shape[1]
    G = NSA_KV_HEADS
    ksw3 = ksw.reshape(B, S, W_KS)
    gt3 = gt.reshape(B, S, LANES)
    H, dh, tq = NSA_HEADS, NSA_DH, TQ_NSA
    state = [pltpu.VMEM((H, 1, tq), F32), pltpu.VMEM((H, dh + ONES_ROWS, tq), F32)]
    return pl.pallas_call(
        _nsa_attn_kernel,
        grid=(B, S // tq),
        in_specs=[pl.BlockSpec((1, 1, W_QN, tq), lambda b, i: (b, i, 0, 0)),
                  pl.BlockSpec((1, nc, G * LANES), lambda b, i: (b, 0, 0)),
                  pl.BlockSpec((1, G * dh, nc), lambda b, i: (b, 0, 0)),
                  pl.BlockSpec((1, S, G * LANES), lambda b, i: (b, 0, 0)),
                  pl.BlockSpec((1, S, G * LANES), lambda b, i: (b, 0, 1)),
                  pl.BlockSpec((1, nkt, G * dh, TKV), lambda b, i: (b, 0, 0, 0)),
                  pl.BlockSpec((1, nkt, G * dh, TKV), lambda b, i: (b, 0, 1, 0)),
                  pl.BlockSpec((1, tq, LANES), lambda b, i: (b, i, 0)),
                  _const_spec(ovt), _const_spec(oh)],
        out_specs=pl.BlockSpec((1, tq, H * dh), lambda b, i: (b, i, 0)),
        out_shape=jax.ShapeDtypeStruct((B, S, H * dh), BF16),
        scratch_shapes=([pltpu.VMEM((H, dh, tq), F32)] + state + state
                        + [pltpu.VMEM((4, H, TKV, tq), F32), pltpu.VMEM((4, H, 1, tq), F32),
                           pltpu.VMEM((H, 2 * LANES, tq), BF16)]),
        compiler_params=_cparams(("parallel", "arbitrary")),
        name="nsa_attn",
    )(qnt, kc, vct, ksw3, ksw3, vt, vt, gt3, ovt, oh)


FF_CHUNK = 256
HALO = 16


def _ffn_kernel(x_ref, xp_ref, a_ref, ap_ref, b_ref, bp_ref, wo_ref, g_ref, wup_ref, cw_ref, cb_ref, wdn_ref, fg_ref,
                o_ref, n_ref, h_ref, act_ref, *, final, spt):
    i = pl.program_id(0)
    ka = a_ref.shape[1]

    def attn_residual(h_rows, a_rows, b_rows):
        return h_rows[...] + _dot(a_rows[...], wo_ref[0, 0:ka, :]) + _dot(b_rows[...], wo_ref[0, ka:, :])

    o_ref[...] = attn_residual(x_ref, a_ref, b_ref)
    n_ref[0:HALO, :] = _rms(attn_residual(xp_ref, ap_ref, bp_ref), g_ref[0]).astype(BF16)
    n_ref[HALO:HALO + TM, :] = _rms(o_ref[...], g_ref[0]).astype(BF16)
    keep = jnp.where(i % spt == 0, 0.0, 1.0)

    def conv(slot, col):
        hx = _dot(n_ref[...], wup_ref[0, :, col:col + FF_CHUNK])
        h_ref[slot, 0:HALO, :] = hx[0:HALO, :] * keep
        h_ref[slot, HALO:HALO + TM, :] = hx[HALO:HALO + TM, :]
        out = cb_ref[0, :, col:col + FF_CHUNK]
        for j in range(CONV_W):
            lo = HALO - (CONV_W - 1) + j
            out = out + h_ref[slot, lo:lo + TM, :] * cw_ref[0, j:j + 1, col:col + FF_CHUNK]
        return out

    for c in range(D_FF // FF_CHUNK):
        col = c * FF_CHUNK
        gate = conv(0, col)
        up = conv(1, D_FF + col)
        act_ref[:, col:col + FF_CHUNK] = (gate * (1.0 / (1.0 + jnp.exp(-gate))) * up).astype(BF16)
    out = o_ref[...] + _dot(act_ref[...], wdn_ref[0])
    if final:
        out = _rms(out, fg_ref[...])
    o_ref[...] = out


def _outproj_ffn(h, oa, ob, wo, g, wup, cw, cb, wdn, fg, l, S, final):
    M = h.shape[0]
    spt = S // TM
    row = lambda width: pl.BlockSpec((TM, width), lambda i: (i, 0))
    prev = lambda width: pl.BlockSpec((HALO, width), lambda i: (jnp.maximum(i * (TM // HALO) - 1, 0), 0))
    return pl.pallas_call(
        functools.partial(_ffn_kernel, final=final, spt=spt),
        grid=(M // TM,),
        in_specs=[row(D_MODEL), prev(D_MODEL), row(oa.shape[1]), prev(oa.shape[1]), row(ob.shape[1]),
                  prev(ob.shape[1]), _layer_spec(wo, l), _layer_spec(g, l), _layer_spec(wup, l), _layer_spec(cw, l),
                  _layer_spec(cb, l), _layer_spec(wdn, l), _const_spec(fg)],
        out_specs=row(D_MODEL),
        out_shape=jax.ShapeDtypeStruct((M, D_MODEL), F32),
        scratch_shapes=[pltpu.VMEM((HALO + TM, D_MODEL), BF16), pltpu.VMEM((2, HALO + TM, FF_CHUNK), F32),
                        pltpu.VMEM((TM, D_FF), BF16)],
        compiler_params=_cparams(("parallel",)),
        name="outproj_ffn",
    )(h, h, oa, oa, ob, ob, wo, g, wup, cw, cb, wdn, fg)


def _slabs(w, n, width, pad_to=LANES):
    lead = w.shape[:-1]
    w = w.reshape(lead + (n, width))
    w = jnp.pad(w, [(0, 0)] * (w.ndim - 1) + [(0, pad_to - width)])
    return w.reshape(lead + (n * pad_to,))


def _inproj_weights(w_in):
    G, dh = NSA_KV_HEADS, NSA_DH
    cols = lambda off, n: w_in[..., off:off + n]
    zeros = lambda n: jnp.zeros(w_in.shape[:-1] + (n,), w_in.dtype)
    kr = jnp.concatenate([zeros(MLA_NOPE), cols(OFF_KROPE, MLA_ROPE), zeros(LANES - MLA_QK)], axis=-1)
    gates = _slabs(cols(OFF_GT, 3 * NSA_HEADS), NSA_HEADS, 3, 4)
    gates = jnp.concatenate([gates, zeros(LANES - 4 * NSA_HEADS)], axis=-1)
    w = jnp.concatenate([cols(OFF_CQ, MLA_Q_RANK), cols(OFF_CKV, MLA_KV_RANK), kr,
                         gates, cols(OFF_KC, G * dh), cols(OFF_VC, G * dh),
                         _slabs(cols(OFF_KS, G * dh), G, dh), _slabs(cols(OFF_KW, G * dh), G, dh)], axis=-1)
    assert w.shape[-1] == W_ROW
    qn = cols(OFF_QN, NSA_HEADS * dh) * (NSA_DH ** -0.5 * LOG2E)
    wvt = jnp.concatenate([cols(OFF_VS, G * dh), cols(OFF_VW, G * dh)], axis=-1)
    return w.astype(BF16), jnp.swapaxes(qn, 1, 2).astype(BF16), jnp.swapaxes(wvt, 1, 2).astype(BF16)


def _bf16_terms(x, n):
    terms = []
    for _ in range(n):
        bits = np.array(x, np.float32).view(np.uint32)
        bits = (bits + np.uint32(0x7FFF) + ((bits >> np.uint32(16)) & np.uint32(1))) & np.uint32(0xFFFF0000)
        terms.append(float(bits.view(np.float32)))
        x = x - terms[-1]
    return terms


N_POS_TERMS = 3


def _pos_features(pos, n_slabs):
    out = np.zeros((len(pos), n_slabs * LANES), np.float32)
    for s in range(n_slabs):
        base = s * LANES + NSA_DH
        out[:, base:base + N_POS_TERMS] = (pos // L_SEL)[:, None]
        out[:, base + N_POS_TERMS:base + 2 * N_POS_TERMS] = (pos % L_SEL)[:, None]
    return out


def _const_tables(S):
    slopes = 2.0 ** (-8.0 * np.arange(1, NSA_HEADS + 1) / NSA_HEADS)
    terms = np.array(_bf16_terms(LOG2E, N_POS_TERMS))
    pad = LANES - NSA_DH
    qaug = np.zeros((NSA_HEADS * pad, 1), np.float32)
    for h in range(NSA_HEADS):
        base = h * pad
        qaug[base:base + N_POS_TERMS, 0] = slopes[h] * L_SEL * terms
        qaug[base + N_POS_TERMS:base + 2 * N_POS_TERMS, 0] = slopes[h] * terms
    posf = _pos_features(np.arange(S), 4)
    nc = S // D_CMP
    feat = _pos_features(np.arange(nc) * D_CMP + L_CMP - 1, 2)
    n_cmp = (S - L_CMP) // D_CMP + 1
    nblk = S // L_SEL
    cs = np.arange(n_cmp) * D_CMP
    ss = np.arange(nblk) * L_SEL
    ov = ((cs[:, None] < ss[None, :] + L_SEL) & (cs[:, None] + L_CMP > ss[None, :])).astype(np.float32)
    ovt = np.zeros((nblk, nc), np.float32)
    ovt[:, :n_cmp] = ov.T
    assert nblk <= LANES
    onehot = (np.arange(S)[:, None] // L_SEL == np.arange(LANES)[None, :]).astype(np.float32)
    return (jnp.asarray(qaug), jnp.asarray(posf), jnp.asarray(feat), jnp.asarray(ovt, BF16),
            jnp.asarray(onehot, BF16))


def _rope_consts(S):
    half = MLA_ROPE // 2
    inv = (1.0 / (np.float32(ROPE_THETA) ** (np.arange(0, MLA_ROPE, 2, dtype=np.float32) / MLA_ROPE))).astype(np.float32)
    ang = np.arange(S, dtype=np.float32)[:, None] * inv[None, :]
    cos, sin = np.cos(ang), np.sin(ang)
    z = lambda w: np.zeros((S, w), np.float32)
    ct = np.concatenate([np.ones((S, MLA_NOPE), np.float32), cos, cos, z(LANES - MLA_QK)], axis=1)
    sa = np.concatenate([z(MLA_NOPE), -sin, z(LANES - MLA_NOPE - half)], axis=1)
    sb = np.concatenate([z(MLA_NOPE + half), sin, z(LANES - MLA_QK)], axis=1)
    return tuple(jnp.asarray(np.ascontiguousarray(t)) for t in (ct, sa, sb, cos.T, sin.T))


def _mla_weights(w_uq, w_ukv):
    wq = _slabs(w_uq, MLA_HEADS, MLA_QK) * (MLA_QK ** -0.5 * LOG2E)
    kv = w_ukv.reshape(w_ukv.shape[:-1] + (MLA_HEADS, MLA_NOPE + MLA_V))
    wk = _slabs(kv[..., :MLA_NOPE].reshape(w_ukv.shape[:-1] + (MLA_HEADS * MLA_NOPE,)), MLA_HEADS, MLA_NOPE)
    wvt = jnp.swapaxes(kv[..., MLA_NOPE:].reshape(w_ukv.shape[:-1] + (MLA_HEADS * MLA_V,)), 1, 2)
    return jnp.swapaxes(wq, 1, 2).astype(BF16), wk.astype(BF16), wvt.astype(BF16)


def _compress_weights(pos, w1, w2k, w2v):
    G, dh, half = NSA_KV_HEADS, NSA_DH, L_CMP // 2
    L = pos.shape[0]
    eye = jnp.eye(G, dtype=F32)

    def first_layer(w):
        return jnp.einsum('nclde,gk->nclgdke', w, eye).reshape(L, 2, half * G * dh, G * CMP_HID).astype(BF16)

    def pos_row(p):
        return jnp.broadcast_to(p[:, :, :, None, :], (L, 2, half, G, dh)).reshape(L, 2, 1, half * G * dh)

    w2kp = jnp.pad(w2k, ((0, 0), (0, 0), (0, LANES - dh)))
    w2k_bd = jnp.einsum('ned,gk->ngekd', w2kp, eye).reshape(L, G * CMP_HID, G * LANES).astype(BF16)
    w2vt_bd = jnp.einsum('ned,gk->ngdke', w2v, eye).reshape(L, G * dh, G * CMP_HID).astype(BF16)
    return (pos_row(pos[:, :, :half]), pos_row(pos[:, :, half:]), first_layer(w1[:, :, :half]),
            first_layer(w1[:, :, half:]), w2k_bd, w2vt_bd)


def kernel(x, attn_norm, w_in, q_norm, kv_norm, w_uq, w_ukv, cmp_pos_k, cmp_pos_v, cmp_k_w1, cmp_k_w2, cmp_v_w1,
           cmp_v_w2, w_o, ffn_norm, w_up, conv_w, conv_b, w_down, final_norm):
    B, S, D = x.shape
    M = B * S
    assert D == D_MODEL and S % TM == 0 and TQ_MLA == TKV and TQ_NSA == TKV and S % TKV == 0
    qaug, posf, feat, ovt, onehot = _const_tables(S)
    rope = _rope_consts(S)

    w_row, w_qt, w_vt = _inproj_weights(w_in)
    mqt, mk, mvt = _mla_weights(w_uq, w_ukv)
    pa, pb, cwa, cwb, w2k, w2vt = _compress_weights(jnp.stack([cmp_pos_k, cmp_pos_v], axis=1),
                                                    jnp.stack([cmp_k_w1, cmp_v_w1], axis=1), cmp_k_w2, cmp_v_w2)
    wo, wup, wdn = w_o.astype(BF16), w_up.astype(BF16), w_down.astype(BF16)
    g_attn, g_q, g_kv, g_ffn = (a[:, None, :] for a in (attn_norm, q_norm, kv_norm, ffn_norm))
    cb = conv_b[:, None, :]
    fg = final_norm[None, :]

    h = x.reshape(M, D)
    for l in range(DEPTH):
        gt, kcr, vcr, qnt, ksw, vt, qt, k, vmt = _inproj(h, g_attn, w_row, w_qt, w_vt, qaug, posf,
                                                          (g_q, g_kv, mqt, mk, mvt), rope, l, B, S)
        o_mla = _mla_attn(qt, k.reshape(B, S, -1), vmt, B, S)
        kc, vct = _compress(kcr.reshape(B, S, -1), vcr.reshape(B, S, -1), pa, pb, cwa, cwb, w2k, w2vt, feat, l)
        o_nsa = _nsa_attn(qnt, kc, vct, ksw, vt, gt, ovt, onehot, B, S)
        h = _outproj_ffn(h, o_mla.reshape(M, -1), o_nsa.reshape(M, -1), wo, g_ffn, wup, conv_w, cb, wdn, fg, l, S,
                         final=(l == DEPTH - 1))
    return h.reshape(B, S, D)
```

```python
import functools

import numpy as np
import jax
import jax.numpy as jnp
from jax import lax
from jax.experimental import pallas as pl
from jax.experimental.pallas import tpu as pltpu

F32 = jnp.float32
BF16 = jnp.bfloat16

D_MODEL = 1024
DEPTH = 2
EPS = 1e-6
NEG_INF = -1e30
LOG2E = float(np.log2(np.e))
LANES = 128

MLA_HEADS = 8
MLA_NOPE = 64
MLA_ROPE = 32
MLA_V = 64
MLA_Q_RANK = 384
MLA_KV_RANK = 256
MLA_QK = MLA_NOPE + MLA_ROPE
ROPE_THETA = 10000.0

NSA_HEADS = 8
NSA_KV_HEADS = 2
NSA_REP = NSA_HEADS // NSA_KV_HEADS
NSA_DH = 64
L_CMP = 32
D_CMP = 16
CMP_HID = 128
L_SEL = 64
N_SEL = 16
W_WIN = 512

D_FF = 2816
CONV_W = 3

_SIZES = [MLA_Q_RANK, MLA_KV_RANK, MLA_ROPE, NSA_HEADS * NSA_DH] + [NSA_KV_HEADS * NSA_DH] * 6 + [3 * NSA_HEADS]
_OFFS = [0] + [int(o) for o in np.cumsum(_SIZES)]
(OFF_CQ, OFF_CKV, OFF_KROPE, OFF_QN, OFF_KC, OFF_VC, OFF_KS, OFF_VS, OFF_KW, OFF_VW, OFF_GT, D_IN) = _OFFS

TM = 512
TKV = 256
TQ_MLA = 256
TQ_NSA = 256
VMEM_LIMIT = 56 * 1024 * 1024

W_ZA = 6 * LANES
W_ZB = 3 * LANES
W_KS = 4 * LANES
W_ROW = W_ZA + W_ZB + W_KS
W_QN = NSA_HEADS * LANES


def _cparams(sem):
    return pltpu.CompilerParams(dimension_semantics=sem, vmem_limit_bytes=VMEM_LIMIT)


def _layer_spec(a, l):
    return pl.BlockSpec((1,) + a.shape[1:], lambda *_: (l,) + (0,) * (a.ndim - 1))


def _const_spec(a):
    return pl.BlockSpec(a.shape, lambda *_: (0,) * a.ndim)


def _dot(a, b):
    return jnp.dot(a, b, preferred_element_type=F32)


def _dot_nt(a, b):
    return lax.dot_general(a, b, (((1,), (1,)), ((), ())), preferred_element_type=F32)


def _rms(x, g):
    return x * lax.rsqrt(jnp.mean(x * x, axis=-1, keepdims=True) + EPS) * g


def _store_tiles(ref, xt):
    for c in range(TM // TKV):
        ref[0, c] = xt[:, c * TKV:(c + 1) * TKV]


def _inproj_kernel(x_ref, g_ref, w_ref, wqt_ref, wvt_ref, qaug_ref, posf_ref,
                   qg_ref, kg_ref, mwqt_ref, mwk_ref, mwvt_ref, ct_ref, sa_ref, sb_ref, cost_ref, sint_ref,
                   gt_ref, kcr_ref, vcr_ref, qnt_ref, ksw_ref, vt_ref, mqt_ref, mk_ref, mvt_ref):
    n = _rms(x_ref[...], g_ref[0]).astype(BF16)
    c0, c1 = W_ZA, W_ZA + W_ZB
    _mla_prep(_dot(n, w_ref[0, :, 0:c0]), qg_ref, kg_ref, mwqt_ref, mwk_ref, mwvt_ref,
              ct_ref, sa_ref, sb_ref, cost_ref, sint_ref, mqt_ref, mk_ref, mvt_ref)
    zb = _dot(n, w_ref[0, :, c0:c1])
    gt_ref[...] = zb[:, 0:LANES]
    kcr_ref[...] = zb[:, LANES:2 * LANES]
    vcr_ref[...] = zb[:, 2 * LANES:3 * LANES]
    ksw_ref[...] = (_dot(n, w_ref[0, :, c1:W_ROW]) + posf_ref[...]).astype(BF16)
    qt = _dot_nt(wqt_ref[0], n)
    slabs, pad = [], LANES - NSA_DH
    for h in range(NSA_HEADS):
        slabs += [qt[h * NSA_DH:(h + 1) * NSA_DH, :], jnp.broadcast_to(qaug_ref[h * pad:(h + 1) * pad, :], (pad, TM))]
    _store_tiles(qnt_ref, jnp.concatenate(slabs, axis=0).astype(BF16))
    _store_tiles(vt_ref, _dot_nt(wvt_ref[0], n).astype(BF16))


def _inproj(h, g, w, wqt, wvt, qaug, posf, mla, rope, l, B, S):
    M = B * S
    spt = S // TM
    ct, sa, sb, cost, sint = rope
    row = lambda width: pl.BlockSpec((TM, width), lambda i: (i, 0))
    tiles = lambda rows: pl.BlockSpec((1, TM // TKV, rows, TKV), lambda i: (i // spt, i % spt, 0, 0))
    tab = pl.BlockSpec((TM, LANES), lambda i: (i % spt, 0))
    tabt = pl.BlockSpec((MLA_ROPE // 2, TM), lambda i: (0, i % spt))
    return pl.pallas_call(
        _inproj_kernel,
        grid=(M // TM,),
        in_specs=[row(D_MODEL), _layer_spec(g, l), _layer_spec(w, l), _layer_spec(wqt, l), _layer_spec(wvt, l),
                  _const_spec(qaug), pl.BlockSpec((TM, W_KS), lambda i: (i % spt, 0))]
                 + [_layer_spec(a, l) for a in mla] + [tab, tab, tab, tabt, tabt],
        out_specs=[row(LANES), row(LANES), row(LANES), tiles(W_QN), row(W_KS), tiles(4 * NSA_DH),
                   tiles(MLA_HEADS * LANES), row(MLA_HEADS * LANES), tiles(MLA_HEADS * MLA_V)],
        out_shape=[jax.ShapeDtypeStruct((M, LANES), F32),
                   jax.ShapeDtypeStruct((M, LANES), F32),
                   jax.ShapeDtypeStruct((M, LANES), F32),
                   jax.ShapeDtypeStruct((B, S // TKV, W_QN, TKV), BF16),
                   jax.ShapeDtypeStruct((M, W_KS), BF16),
                   jax.ShapeDtypeStruct((B, S // TKV, 4 * NSA_DH, TKV), BF16),
                   jax.ShapeDtypeStruct((B, S // TKV, MLA_HEADS * LANES, TKV), BF16),
                   jax.ShapeDtypeStruct((M, MLA_HEADS * LANES), BF16),
                   jax.ShapeDtypeStruct((B, S // TKV, MLA_HEADS * MLA_V, TKV), BF16)],
        compiler_params=_cparams(("parallel",)),
        name="attn_inproj",
    )(h, g, w, wqt, wvt, qaug, posf, *mla, ct, sa, sb, cost, sint)


def _rope_slab(x, ct, sa, sb):
    return x * ct + pltpu.roll(x, LANES - MLA_ROPE // 2, 1) * sa + pltpu.roll(x, MLA_ROPE // 2, 1) * sb


def _mla_prep(za, qg_ref, kg_ref, wqt_ref, wk_ref, wvt_ref, ct_ref, sa_ref, sb_ref, cost_ref, sint_ref,
              qt_ref, k_ref, vt_ref):
    cq = _rms(za[:, 0:MLA_Q_RANK], qg_ref[0]).astype(BF16)
    ckv = _rms(za[:, MLA_Q_RANK:MLA_Q_RANK + MLA_KV_RANK], kg_ref[0]).astype(BF16)
    kr = _rope_slab(za[:, 5 * LANES:6 * LANES], ct_ref[...], sa_ref[...], sb_ref[...])
    k = _dot(ckv, wk_ref[0])
    for h in range(MLA_HEADS):
        sl = slice(h * LANES, (h + 1) * LANES)
        k_ref[:, sl] = (k[:, sl] + kr).astype(BF16)
    qt = _dot_nt(wqt_ref[0], cq)
    cos, sin = cost_ref[...], sint_ref[...]
    half = MLA_ROPE // 2
    slabs = []
    for h in range(MLA_HEADS):
        base = h * LANES
        x1 = qt[base + MLA_NOPE:base + MLA_NOPE + half, :]
        x2 = qt[base + MLA_NOPE + half:base + MLA_QK, :]
        slabs += [qt[base:base + MLA_NOPE, :], x1 * cos - x2 * sin, x2 * cos + x1 * sin,
                  qt[base + MLA_QK:base + LANES, :]]
    _store_tiles(qt_ref, jnp.concatenate(slabs, axis=0).astype(BF16))
    _store_tiles(vt_ref, _dot_nt(wvt_ref[0], ckv).astype(BF16))


ONES_ROWS = 16


def _stage_scores(s, s_ref, mx_ref, b, h):
    mx_ref[b, h] = jnp.max(s, axis=0, keepdims=True)
    s_ref[b, h, 0:s.shape[0], :] = s


def _online_update(vt, s_ref, mx_ref, b, m_ref, acc_ref, h):
    m_old = m_ref[h]
    m_new = jnp.maximum(m_old, mx_ref[b, h])
    p = jnp.exp2(s_ref[b, h] - m_new).astype(BF16)
    vt1 = jnp.concatenate([vt, jnp.ones((ONES_ROWS, vt.shape[1]), BF16)], axis=0)
    acc_ref[h] = jnp.exp2(m_old - m_new) * acc_ref[h] + _dot(vt1, p)
    m_ref[h] = m_new


def _normalised(acc_ref, h, dh):
    return acc_ref[h, 0:dh, :] / acc_ref[h, dh:dh + 1, :]


def _pipelined_sweep(i, n_heads, score, update, score_diag, tile0_staged=False):
    every = range(n_heads)

    if not tile0_staged:
        @pl.when(i >= 1)
        def _():
            score(0, 0, every)

    @pl.loop(0, i // 2)
    def _(j):
        for h in every:
            score(2 * j + 1, 1, [h])
            update(2 * j, 0, [h])
        for h in every:
            score(2 * j + 2, 0, [h])
            update(2 * j + 1, 1, [h])

    @pl.when(i % 2 == 1)
    def _():
        for h in every:
            score_diag(1, [h])
            update(i - 1, 0, [h])
        update(i, 1, every)

    @pl.when(i % 2 == 0)
    def _():
        score_diag(0, every)
        update(i, 0, every)


def _mla_attn_kernel(qt_ref, k_ref, vt_ref, o_ref, m_ref, acc_ref, s_ref, mx_ref):
    i = pl.program_id(1)
    tq = TQ_MLA
    m_ref[...] = jnp.full(m_ref.shape, -jnp.inf, F32)
    acc_ref[...] = jnp.zeros(acc_ref.shape, F32)

    every = range(MLA_HEADS)

    def score(kt, b, heads, keep=None):
        for h in heads:
            sl = slice(h * LANES, (h + 1) * LANES)
            k = k_ref[0, pl.ds(pl.multiple_of(kt * TKV, TKV), TKV), sl]
            s = _dot(k, qt_ref[0, 0, sl, :])
            if keep is not None:
                s = jnp.where(keep, s, NEG_INF)
            _stage_scores(s, s_ref, mx_ref, b, h)

    def update(kt, b, heads):
        for h in heads:
            _online_update(vt_ref[0, kt, h * MLA_V:(h + 1) * MLA_V, :], s_ref, mx_ref, b, m_ref, acc_ref, h)

    causal = lax.broadcasted_iota(jnp.int32, (TKV, tq), 0) <= lax.broadcasted_iota(jnp.int32, (TKV, tq), 1)
    _pipelined_sweep(i, MLA_HEADS, score, update, lambda b, heads: score(i, b, heads, causal))
    out = jnp.concatenate([_normalised(acc_ref, h, MLA_V) for h in range(MLA_HEADS)], axis=0)
    o_ref[0] = out.T.astype(BF16)


def _mla_attn(q, k, vt, B, S):
    nkt = S // TKV
    hw = MLA_HEADS * LANES
    return pl.pallas_call(
        _mla_attn_kernel,
        grid=(B, S // TQ_MLA),
        in_specs=[pl.BlockSpec((1, 1, hw, TQ_MLA), lambda b, i: (b, i, 0, 0)),
                  pl.BlockSpec((1, S, hw), lambda b, i: (b, 0, 0)),
                  pl.BlockSpec((1, nkt, MLA_HEADS * MLA_V, TKV), lambda b, i: (b, 0, 0, 0))],
        out_specs=pl.BlockSpec((1, TQ_MLA, MLA_HEADS * MLA_V), lambda b, i: (b, i, 0)),
        out_shape=jax.ShapeDtypeStruct((B, S, MLA_HEADS * MLA_V), BF16),
        scratch_shapes=[pltpu.VMEM((MLA_HEADS, 1, TQ_MLA), F32),
                        pltpu.VMEM((MLA_HEADS, MLA_V + ONES_ROWS, TQ_MLA), F32),
                        pltpu.VMEM((2, MLA_HEADS, TKV, TQ_MLA), F32), pltpu.VMEM((2, MLA_HEADS, 1, TQ_MLA), F32)],
        compiler_params=_cparams(("parallel", "arbitrary")),
        name="mla_attn",
    )(q, k, vt)


def _gelu_tanh(x):
    return 0.5 * x * (1.0 + jnp.tanh(np.sqrt(2.0 / np.pi).astype(np.float32) * (x + 0.044715 * (x * x * x))))


def _compress_kernel(kx_ref, vx_ref, pa_ref, pb_ref, wa_ref, wb_ref, w2k_ref, w2vt_ref, feat_ref, kc_ref, vct_ref):
    half = L_CMP // 2
    nc = kx_ref.shape[1] // half

    def hidden(x_ref, kv):
        x = jnp.concatenate([x_ref[0, pl.ds(l, nc, stride=half), :] for l in range(half)], axis=1)
        ya = _dot((x + pa_ref[0, kv]).astype(BF16), wa_ref[0, kv])
        yb = _dot((x + pb_ref[0, kv]).astype(BF16), wb_ref[0, kv])
        return _gelu_tanh(ya + pltpu.roll(yb, nc - 1, 0)).astype(BF16)

    kc_ref[0] = (_dot(hidden(kx_ref, 0), w2k_ref[0]) + feat_ref[...]).astype(BF16)
    vct_ref[0] = _dot_nt(w2vt_ref[0], hidden(vx_ref, 1)).astype(BF16)


def _compress(kx, vx, pa, pb, wa, wb, w2k, w2vt, feat, l):
    B, S, width = kx.shape
    nc = S // (L_CMP // 2)
    xs = pl.BlockSpec((1, S, width), lambda b: (b, 0, 0))
    return pl.pallas_call(
        _compress_kernel,
        grid=(B,),
        in_specs=[xs, xs] + [_layer_spec(a, l) for a in (pa, pb, wa, wb, w2k, w2vt)] + [_const_spec(feat)],
        out_specs=[pl.BlockSpec((1, nc, 2 * LANES), lambda b: (b, 0, 0)),
                   pl.BlockSpec((1, 2 * NSA_DH, nc), lambda b: (b, 0, 0))],
        out_shape=[jax.ShapeDtypeStruct((B, nc, 2 * LANES), BF16),
                   jax.ShapeDtypeStruct((B, 2 * NSA_DH, nc), BF16)],
        compiler_params=_cparams(("parallel",)),
        name="nsa_compress",
    )(kx, vx, pa, pb, wa, wb, w2k, w2vt, feat)


def _nsa_attn_kernel(qnt_ref, kc_ref, vct_ref, ks_ref, kw_ref, vst_ref, vwt_ref, gt_ref, ovt_ref, oh_ref,
                     o_ref, ocmp_ref, m_ref, acc_ref, mw_ref, accw_ref, s_ref, mx_ref, qs_ref):
    i = pl.program_id(1)
    tq = TQ_NSA
    G, R, dh = NSA_KV_HEADS, NSA_REP, NSA_DH
    nc = kc_ref.shape[1]
    nblk = ovt_ref.shape[0]
    t0 = i * tq
    tpos = t0 + lax.broadcasted_iota(jnp.int32, (1, tq), 1)
    q_of = lambda h: qnt_ref[0, 0, h * LANES:(h + 1) * LANES, :]

    for ref in (m_ref, mw_ref):
        ref[...] = jnp.full(ref.shape, -jnp.inf, F32)
    for ref in (acc_ref, accw_ref):
        ref[...] = jnp.zeros(ref.shape, F32)

    every = range(G * R)
    rel = lax.broadcasted_iota(jnp.int32, (TKV, tq), 0) - lax.broadcasted_iota(jnp.int32, (TKV, tq), 1)
    causal = rel <= 0

    def score(kt, b, heads, sel, keep=None):
        rows = pl.ds(pl.multiple_of(kt * TKV, TKV), TKV)
        for h in heads:
            g = h // R
            if sel:
                k = jnp.concatenate([ks_ref[0, rows, g * LANES:(g + 1) * LANES], oh_ref[rows, :]], axis=1)
            else:
                k = kw_ref[0, rows, g * LANES:(g + 1) * LANES]
            s = _dot(k, qs_ref[h] if sel else q_of(h))
            if keep is not None:
                s = jnp.where(keep, s, NEG_INF)
            _stage_scores(s, s_ref, mx_ref, b, h)

    def update(kt, b, heads, sel):
        vt_ref, state = (vst_ref, (m_ref, acc_ref)) if sel else (vwt_ref, (mw_ref, accw_ref))
        for h in heads:
            g = h // R
            _online_update(vt_ref[0, kt, g * dh:(g + 1) * dh, :], s_ref, mx_ref, b, *state, h)

    assert W_WIN == 2 * TKV
    kt1, kt2 = jnp.maximum(i - 1, 0), jnp.maximum(i - 2, 0)
    keep1 = jnp.broadcast_to(i >= 1, (TKV, tq))
    keep2 = (rel > 0) & (i >= 2)

    nrow = lax.broadcasted_iota(jnp.int32, (nc, 1), 0)
    bias_c = jnp.where((tpos >= nrow * D_CMP + (L_CMP - 1)) & (nrow < nc - 1), 0.0, NEG_INF)
    has_cmp = tpos >= L_CMP - 1
    for h in every:
        g = h // R
        _stage_scores(_dot(kc_ref[0, :, g * LANES:(g + 1) * LANES], q_of(h)) + bias_c, s_ref, mx_ref, 0, h)
    score(i, 1, every, False, causal)
    for h in every:
        score(kt1, 2, [h], False, keep1)
        update(i, 1, [h], False)

    imps = [None] * G
    for h in every:
        g = h // R
        e = jnp.exp2(s_ref[0, h, 0:nc, :] - mx_ref[0, h])
        inv = jnp.where(has_cmp, 1.0 / jnp.sum(e, axis=0, keepdims=True), 0.0)
        both = _dot(jnp.concatenate([vct_ref[0, g * dh:(g + 1) * dh, :], ovt_ref[...]], axis=0), e.astype(BF16))
        ocmp_ref[h] = both[0:dh, :] * inv
        contrib = both[dh:dh + nblk, :] * inv
        imps[g] = contrib if imps[g] is None else imps[g] + contrib
        score(kt2, 3, [h], False, keep2)
        update(kt1, 2, [h], False)
    update(kt2, 3, every, False)

    cur = lax.shift_right_logical(tpos, 6)
    jrow = lax.broadcasted_iota(jnp.int32, (nblk, 1), 0)
    sub = lax.broadcasted_iota(jnp.int32, (8, 1), 0)
    n_slab = nblk // 8
    for g in range(G):
        imp = jnp.where(jrow > cur, -jnp.inf, imps[g])
        imp = jnp.where((jrow == 0) | (jrow == cur) | (jrow == cur - 1), jnp.inf, imp)
        slabs = [imp[8 * v:8 * v + 8, :] for v in range(n_slab)]
        ranks = [jnp.zeros((8, tq), F32) for _ in range(n_slab)]
        for jj in range(nblk):
            row = imp[jj:jj + 1, :]
            for v in range(n_slab):
                if v > jj // 8:
                    beats = row >= slabs[v]
                elif v < jj // 8:
                    beats = row > slabs[v]
                else:
                    beats = (row > slabs[v]) | ((row == slabs[v]) & (sub > jj % 8))
                ranks[v] = ranks[v] + jnp.where(beats, 1.0, 0.0)
        rank = jnp.concatenate(ranks, axis=0)
        mb = jnp.where(rank < float(min(N_SEL, nblk)), 0.0, NEG_INF)
        mb = jnp.concatenate([mb, jnp.zeros((LANES - nblk, tq), F32)], axis=0).astype(BF16)
        for r in range(R):
            qs_ref[g * R + r] = jnp.concatenate([q_of(g * R + r), mb], axis=0)
        score(0, 0, range(g * R, (g + 1) * R), True)

    _pipelined_sweep(i, G * R, functools.partial(score, sel=True), functools.partial(update, sel=True),
                     lambda b, heads: score(i, b, heads, True, causal), tile0_staged=True)

    sig = 1.0 / (1.0 + jnp.exp(-gt_ref[0].T[0:G * 16, :]))
    outs = []
    for h in range(G * R):
        row = (h // R) * 16 + (h % R) * 4
        outs.append(sig[row:row + 1, :] * ocmp_ref[h] + sig[row + 1:row + 2, :] * _normalised(acc_ref, h, dh)
                    + sig[row + 2:row + 3, :] * _normalised(accw_ref, h, dh))
    o_ref[0] = jnp.concatenate(outs, axis=0).T.astype(BF16)


def _nsa_attn(qnt, kc, vct, ksw, vt, gt, ovt, oh, B, S):
    nkt = S // TKV
    nc = kc.shape[1]
    G = NSA_KV_HEADS
    ksw3 = ksw.reshape(B, S, W_KS)
    gt3 = gt.reshape(B, S, LANES)
    H, dh, tq = NSA_HEADS, NSA_DH, TQ_NSA
    state = [pltpu.VMEM((H, 1, tq), F32), pltpu.VMEM((H, dh + ONES_ROWS, tq), F32)]
    return pl.pallas_call(
        _nsa_attn_kernel,
        grid=(B, S // tq),
        in_specs=[pl.BlockSpec((1, 1, W_QN, tq), lambda b, i: (b, i, 0, 0)),
                  pl.BlockSpec((1, nc, G * LANES), lambda b, i: (b, 0, 0)),
                  pl.BlockSpec((1, G * dh, nc), lambda b, i: (b, 0, 0)),
                  pl.BlockSpec((1, S, G * LANES), lambda b, i: (b, 0, 0)),
                  pl.BlockSpec((1, S, G * LANES), lambda b, i: (b, 0, 1)),
                  pl.BlockSpec((1, nkt, G * dh, TKV), lambda b, i: (b, 0, 0, 0)),
                  pl.BlockSpec((1, nkt, G * dh, TKV), lambda b, i: (b, 0, 1, 0)),
                  pl.BlockSpec((1, tq, LANES), lambda b, i: (b, i, 0)),
                  _const_spec(ovt), _const_spec(oh)],
        out_specs=pl.BlockSpec((1, tq, H * dh), lambda b, i: (b, i, 0)),
        out_shape=jax.ShapeDtypeStruct((B, S, H * dh), BF16),
        scratch_shapes=([pltpu.VMEM((H, dh, tq), F32)] + state + state
                        + [pltpu.VMEM((4, H, TKV, tq), F32), pltpu.VMEM((4, H, 1, tq), F32),
                           pltpu.VMEM((H, 2 * LANES, tq), BF16)]),
        compiler_params=_cparams(("parallel", "arbitrary")),
        name="nsa_attn",
    )(qnt, kc, vct, ksw3, ksw3, vt, vt, gt3, ovt, oh)


FF_CHUNK = 256
HALO = 16
SUB = 8
QV = TM // SUB


def _ffn_kernel(x_ref, xp_ref, a_ref, ap_ref, b_ref, bp_ref, wo_ref, g_ref, wup_ref, cw_ref, cb_ref, wdn_ref, fg_ref,
                o_ref, n_ref, p_ref, act_ref, *, final, spt):
    assert CONV_W == 3
    i = pl.program_id(0)
    ka = a_ref.shape[1]
    n_slab = D_MODEL // LANES

    def attn_residual(h_rows, a_rows, b_rows):
        return h_rows[...] + _dot(a_rows[...], wo_ref[0, 0:ka, :]) + _dot(b_rows[...], wo_ref[0, ka:, :])

    o_ref[...] = attn_residual(x_ref, a_ref, b_ref)
    keep = jnp.where(i % spt == 0, 0.0, 1.0)
    n_ref[0:HALO, :] = (_rms(attn_residual(xp_ref, ap_ref, bp_ref), g_ref[0]) * keep).astype(BF16)
    nt = _rms(o_ref[...], g_ref[0])
    for c in range(n_slab):
        for s in range(SUB):
            p_ref[c, pl.ds(s, QV, stride=SUB), :] = nt[s * QV:(s + 1) * QV, c * LANES:(c + 1) * LANES]
    for c in range(n_slab):
        n_ref[HALO:HALO + TM, c * LANES:(c + 1) * LANES] = p_ref[c].astype(BF16)

    first_sublane = lax.broadcasted_iota(jnp.int32, (SUB, 1), 0) == 0

    def conv(col):
        hx = _dot(n_ref[...], wup_ref[0, :, col:col + FF_CHUNK])
        main = hx[HALO:HALO + TM, :]
        wrap1 = jnp.where(first_sublane, hx[HALO - 1:HALO, :], pltpu.roll(main[TM - SUB:TM, :], 1, 0))
        wrap2 = jnp.where(first_sublane, hx[HALO - 2:HALO - 1, :], pltpu.roll(main[TM - 2 * SUB:TM - SUB, :], 1, 0))
        tap1 = jnp.concatenate([wrap1, main[0:TM - SUB, :]], axis=0)
        tap2 = jnp.concatenate([wrap2, wrap1, main[0:TM - 2 * SUB, :]], axis=0)
        cw = cw_ref[0, :, col:col + FF_CHUNK]
        return cb_ref[0, :, col:col + FF_CHUNK] + tap2 * cw[0:1, :] + tap1 * cw[1:2, :] + main * cw[2:3, :]

    for c in range(D_FF // FF_CHUNK):
        col = c * FF_CHUNK
        gate = conv(col)
        up = conv(D_FF + col)
        act_ref[:, col:col + FF_CHUNK] = (gate * (1.0 / (1.0 + jnp.exp(-gate))) * up).astype(BF16)
    down = _dot(act_ref[...], wdn_ref[0])
    for c in range(n_slab):
        p_ref[c] = down[:, c * LANES:(c + 1) * LANES]
    for c in range(n_slab):
        for s in range(SUB):
            rows, lanes = slice(s * QV, (s + 1) * QV), slice(c * LANES, (c + 1) * LANES)
            o_ref[rows, lanes] = o_ref[rows, lanes] + p_ref[c, pl.ds(s, QV, stride=SUB), :]
    if final:
        o_ref[...] = _rms(o_ref[...], fg_ref[...])


def _outproj_ffn(h, oa, ob, wo, g, wup, cw, cb, wdn, fg, l, S, final):
    M = h.shape[0]
    spt = S // TM
    row = lambda width: pl.BlockSpec((TM, width), lambda i: (i, 0))
    prev = lambda width: pl.BlockSpec((HALO, width), lambda i: (jnp.maximum(i * (TM // HALO) - 1, 0), 0))
    return pl.pallas_call(
        functools.partial(_ffn_kernel, final=final, spt=spt),
        grid=(M // TM,),
        in_specs=[row(D_MODEL), prev(D_MODEL), row(oa.shape[1]), prev(oa.shape[1]), row(ob.shape[1]),
                  prev(ob.shape[1]), _layer_spec(wo, l), _layer_spec(g, l), _layer_spec(wup, l), _layer_spec(cw, l),
                  _layer_spec(cb, l), _layer_spec(wdn, l), _const_spec(fg)],
        out_specs=row(D_MODEL),
        out_shape=jax.ShapeDtypeStruct((M, D_MODEL), F32),
        scratch_shapes=[pltpu.VMEM((HALO + TM, D_MODEL), BF16), pltpu.VMEM((D_MODEL // LANES, TM, LANES), F32),
                        pltpu.VMEM((TM, D_FF), BF16)],
        compiler_params=_cparams(("parallel",)),
        name="outproj_ffn",
    )(h, h, oa, oa, ob, ob, wo, g, wup, cw, cb, wdn, fg)


def _slabs(w, n, width, pad_to=LANES):
    lead = w.shape[:-1]
    w = w.reshape(lead + (n, width))
    w = jnp.pad(w, [(0, 0)] * (w.ndim - 1) + [(0, pad_to - width)])
    return w.reshape(lead + (n * pad_to,))


def _inproj_weights(w_in):
    G, dh = NSA_KV_HEADS, NSA_DH
    cols = lambda off, n: w_in[..., off:off + n]
    zeros = lambda n: jnp.zeros(w_in.shape[:-1] + (n,), w_in.dtype)
    kr = jnp.concatenate([zeros(MLA_NOPE), cols(OFF_KROPE, MLA_ROPE), zeros(LANES - MLA_QK)], axis=-1)
    gates = _slabs(cols(OFF_GT, 3 * NSA_HEADS), NSA_HEADS, 3, 4)
    gates = jnp.concatenate([gates, zeros(LANES - 4 * NSA_HEADS)], axis=-1)
    w = jnp.concatenate([cols(OFF_CQ, MLA_Q_RANK), cols(OFF_CKV, MLA_KV_RANK), kr,
                         gates, cols(OFF_KC, G * dh), cols(OFF_VC, G * dh),
                         _slabs(cols(OFF_KS, G * dh), G, dh), _slabs(cols(OFF_KW, G * dh), G, dh)], axis=-1)
    assert w.shape[-1] == W_ROW
    qn = cols(OFF_QN, NSA_HEADS * dh) * (NSA_DH ** -0.5 * LOG2E)
    wvt = jnp.concatenate([cols(OFF_VS, G * dh), cols(OFF_VW, G * dh)], axis=-1)
    return w.astype(BF16), jnp.swapaxes(qn, 1, 2).astype(BF16), jnp.swapaxes(wvt, 1, 2).astype(BF16)


def _bf16_terms(x, n):
    terms = []
    for _ in range(n):
        bits = np.array(x, np.float32).view(np.uint32)
        bits = (bits + np.uint32(0x7FFF) + ((bits >> np.uint32(16)) & np.uint32(1))) & np.uint32(0xFFFF0000)
        terms.append(float(bits.view(np.float32)))
        x = x - terms[-1]
    return terms


N_POS_TERMS = 3


def _pos_features(pos, n_slabs):
    out = np.zeros((len(pos), n_slabs * LANES), np.float32)
    for s in range(n_slabs):
        base = s * LANES + NSA_DH
        out[:, base:base + N_POS_TERMS] = (pos // L_SEL)[:, None]
        out[:, base + N_POS_TERMS:base + 2 * N_POS_TERMS] = (pos % L_SEL)[:, None]
    return out


def _const_tables(S):
    slopes = 2.0 ** (-8.0 * np.arange(1, NSA_HEADS + 1) / NSA_HEADS)
    terms = np.array(_bf16_terms(LOG2E, N_POS_TERMS))
    pad = LANES - NSA_DH
    qaug = np.zeros((NSA_HEADS * pad, 1), np.float32)
    for h in range(NSA_HEADS):
        base = h * pad
        qaug[base:base + N_POS_TERMS, 0] = slopes[h] * L_SEL * terms
        qaug[base + N_POS_TERMS:base + 2 * N_POS_TERMS, 0] = slopes[h] * terms
    posf = _pos_features(np.arange(S), 4)
    nc = S // D_CMP
    feat = _pos_features(np.arange(nc) * D_CMP + L_CMP - 1, 2)
    n_cmp = (S - L_CMP) // D_CMP + 1
    nblk = S // L_SEL
    cs = np.arange(n_cmp) * D_CMP
    ss = np.arange(nblk) * L_SEL
    ov = ((cs[:, None] < ss[None, :] + L_SEL) & (cs[:, None] + L_CMP > ss[None, :])).astype(np.float32)
    ovt = np.zeros((nblk, nc), np.float32)
    ovt[:, :n_cmp] = ov.T
    assert nblk <= LANES
    onehot = (np.arange(S)[:, None] // L_SEL == np.arange(LANES)[None, :]).astype(np.float32)
    return (jnp.asarray(qaug), jnp.asarray(posf), jnp.asarray(feat), jnp.asarray(ovt, BF16),
            jnp.asarray(onehot, BF16))


def _rope_consts(S):
    half = MLA_ROPE // 2
    inv = (1.0 / (np.float32(ROPE_THETA) ** (np.arange(0, MLA_ROPE, 2, dtype=np.float32) / MLA_ROPE))).astype(np.float32)
    ang = np.arange(S, dtype=np.float32)[:, None] * inv[None, :]
    cos, sin = np.cos(ang), np.sin(ang)
    z = lambda w: np.zeros((S, w), np.float32)
    ct = np.concatenate([np.ones((S, MLA_NOPE), np.float32), cos, cos, z(LANES - MLA_QK)], axis=1)
    sa = np.concatenate([z(MLA_NOPE), -sin, z(LANES - MLA_NOPE - half)], axis=1)
    sb = np.concatenate([z(MLA_NOPE + half), sin, z(LANES - MLA_QK)], axis=1)
    return tuple(jnp.asarray(np.ascontiguousarray(t)) for t in (ct, sa, sb, cos.T, sin.T))


def _mla_weights(w_uq, w_ukv):
    wq = _slabs(w_uq, MLA_HEADS, MLA_QK) * (MLA_QK ** -0.5 * LOG2E)
    kv = w_ukv.reshape(w_ukv.shape[:-1] + (MLA_HEADS, MLA_NOPE + MLA_V))
    wk = _slabs(kv[..., :MLA_NOPE].reshape(w_ukv.shape[:-1] + (MLA_HEADS * MLA_NOPE,)), MLA_HEADS, MLA_NOPE)
    wvt = jnp.swapaxes(kv[..., MLA_NOPE:].reshape(w_ukv.shape[:-1] + (MLA_HEADS * MLA_V,)), 1, 2)
    return jnp.swapaxes(wq, 1, 2).astype(BF16), wk.astype(BF16), wvt.astype(BF16)


def _compress_weights(pos, w1, w2k, w2v):
    G, dh, half = NSA_KV_HEADS, NSA_DH, L_CMP // 2
    L = pos.shape[0]
    eye = jnp.eye(G, dtype=F32)

    def first_layer(w):
        return jnp.einsum('nclde,gk->nclgdke', w, eye).reshape(L, 2, half * G * dh, G * CMP_HID).astype(BF16)

    def pos_row(p):
        return jnp.broadcast_to(p[:, :, :, None, :], (L, 2, half, G, dh)).reshape(L, 2, 1, half * G * dh)

    w2kp = jnp.pad(w2k, ((0, 0), (0, 0), (0, LANES - dh)))
    w2k_bd = jnp.einsum('ned,gk->ngekd', w2kp, eye).reshape(L, G * CMP_HID, G * LANES).astype(BF16)
    w2vt_bd = jnp.einsum('ned,gk->ngdke', w2v, eye).reshape(L, G * dh, G * CMP_HID).astype(BF16)
    return (pos_row(pos[:, :, :half]), pos_row(pos[:, :, half:]), first_layer(w1[:, :, :half]),
            first_layer(w1[:, :, half:]), w2k_bd, w2vt_bd)


def kernel(x, attn_norm, w_in, q_norm, kv_norm, w_uq, w_ukv, cmp_pos_k, cmp_pos_v, cmp_k_w1, cmp_k_w2, cmp_v_w1,
           cmp_v_w2, w_o, ffn_norm, w_up, conv_w, conv_b, w_down, final_norm):
    B, S, D = x.shape
    M = B * S
    assert D == D_MODEL and S % TM == 0 and TQ_MLA == TKV and TQ_NSA == TKV and S % TKV == 0
    qaug, posf, feat, ovt, onehot = _const_tables(S)
    rope = _rope_consts(S)

    w_row, w_qt, w_vt = _inproj_weights(w_in)
    mqt, mk, mvt = _mla_weights(w_uq, w_ukv)
    pa, pb, cwa, cwb, w2k, w2vt = _compress_weights(jnp.stack([cmp_pos_k, cmp_pos_v], axis=1),
                                                    jnp.stack([cmp_k_w1, cmp_v_w1], axis=1), cmp_k_w2, cmp_v_w2)
    wo, wup, wdn = w_o.astype(BF16), w_up.astype(BF16), w_down.astype(BF16)
    g_attn, g_q, g_kv, g_ffn = (a[:, None, :] for a in (attn_norm, q_norm, kv_norm, ffn_norm))
    cb = conv_b[:, None, :]
    fg = final_norm[None, :]

    h = x.reshape(M, D)
    for l in range(DEPTH):
        gt, kcr, vcr, qnt, ksw, vt, qt, k, vmt = _inproj(h, g_attn, w_row, w_qt, w_vt, qaug, posf,
                                                          (g_q, g_kv, mqt, mk, mvt), rope, l, B, S)
        o_mla = _mla_attn(qt, k.reshape(B, S, -1), vmt, B, S)
        kc, vct = _compress(kcr.reshape(B, S, -1), vcr.reshape(B, S, -1), pa, pb, cwa, cwb, w2k, w2vt, feat, l)
        o_nsa = _nsa_attn(qnt, kc, vct, ksw, vt, gt, ovt, onehot, B, S)
        h = _outproj_ffn(h, o_mla.reshape(M, -1), o_nsa.reshape(M, -1), wo, g_ffn, wup, conv_w, cb, wdn, fg, l, S,
                         final=(l == DEPTH - 1))
    return h.reshape(B, S, D)
```

```python
import functools

import numpy as np
import jax
import jax.numpy as jnp
from jax import lax
from jax.experimental import pallas as pl
from jax.experimental.pallas import tpu as pltpu

F32 = jnp.float32
BF16 = jnp.bfloat16

D_MODEL = 1024
DEPTH = 2
EPS = 1e-6
NEG_INF = -1e30
LOG2E = float(np.log2(np.e))
LANES = 128

MLA_HEADS = 8
MLA_NOPE = 64
MLA_ROPE = 32
MLA_V = 64
MLA_Q_RANK = 384
MLA_KV_RANK = 256
MLA_QK = MLA_NOPE + MLA_ROPE
ROPE_THETA = 10000.0

NSA_HEADS = 8
NSA_KV_HEADS = 2
NSA_REP = NSA_HEADS // NSA_KV_HEADS
NSA_DH = 64
L_CMP = 32
D_CMP = 16
CMP_HID = 128
L_SEL = 64
N_SEL = 16
W_WIN = 512

D_FF = 2816
CONV_W = 3

_SIZES = [MLA_Q_RANK, MLA_KV_RANK, MLA_ROPE, NSA_HEADS * NSA_DH] + [NSA_KV_HEADS * NSA_DH] * 6 + [3 * NSA_HEADS]
_OFFS = [0] + [int(o) for o in np.cumsum(_SIZES)]
(OFF_CQ, OFF_CKV, OFF_KROPE, OFF_QN, OFF_KC, OFF_VC, OFF_KS, OFF_VS, OFF_KW, OFF_VW, OFF_GT, D_IN) = _OFFS

TM = 512
TKV = 256
TQ_MLA = 256
TQ_NSA = 256
VMEM_LIMIT = 56 * 1024 * 1024

W_ZA = 6 * LANES
W_ZB = 3 * LANES
W_KS = 4 * LANES
W_ROW = W_ZA + W_ZB + W_KS
W_QN = NSA_HEADS * LANES


def _cparams(sem):
    return pltpu.CompilerParams(dimension_semantics=sem, vmem_limit_bytes=VMEM_LIMIT)


def _layer_spec(a, l):
    return pl.BlockSpec((1,) + a.shape[1:], lambda *_: (l,) + (0,) * (a.ndim - 1))


def _const_spec(a):
    return pl.BlockSpec(a.shape, lambda *_: (0,) * a.ndim)


def _dot(a, b):
    return jnp.dot(a, b, preferred_element_type=F32)


def _dot_nt(a, b):
    return lax.dot_general(a, b, (((1,), (1,)), ((), ())), preferred_element_type=F32)


def _rms(x, g):
    return x * lax.rsqrt(jnp.mean(x * x, axis=-1, keepdims=True) + EPS) * g


def _store_tiles(ref, xt):
    for c in range(TM // TKV):
        ref[0, c] = xt[:, c * TKV:(c + 1) * TKV]


def _inproj_kernel(x_ref, g_ref, w_ref, wqt_ref, wvt_ref, qaug_ref, posf_ref,
                   qg_ref, kg_ref, mwqt_ref, mwk_ref, mwvt_ref, ct_ref, sa_ref, sb_ref, cost_ref, sint_ref,
                   gt_ref, kcr_ref, vcr_ref, qnt_ref, ksw_ref, vt_ref, mqt_ref, mk_ref, mvt_ref):
    n = _rms(x_ref[...], g_ref[0]).astype(BF16)
    c0, c1 = W_ZA, W_ZA + W_ZB
    cq, ckv, kr = _mla_norms(_dot(n, w_ref[0, :, 0:c0]), qg_ref, kg_ref, ct_ref, sa_ref, sb_ref)
    zb = _dot(n, w_ref[0, :, c0:c1])
    gt_ref[...] = zb[:, 0:LANES]
    kcr_ref[...] = zb[:, LANES:2 * LANES]
    vcr_ref[...] = zb[:, 2 * LANES:3 * LANES]
    ksw_ref[...] = (_dot(n, w_ref[0, :, c1:W_ROW]) + posf_ref[...]).astype(BF16)
    qt = _dot_nt(wqt_ref[0], n)
    slabs, pad = [], LANES - NSA_DH
    for h in range(NSA_HEADS):
        slabs += [qt[h * NSA_DH:(h + 1) * NSA_DH, :], jnp.broadcast_to(qaug_ref[h * pad:(h + 1) * pad, :], (pad, TM))]
    _store_tiles(qnt_ref, jnp.concatenate(slabs, axis=0).astype(BF16))
    _store_tiles(vt_ref, _dot_nt(wvt_ref[0], n).astype(BF16))
    _mla_project(cq, ckv, kr, mwqt_ref, mwk_ref, mwvt_ref, cost_ref, sint_ref, mqt_ref, mk_ref, mvt_ref)


def _inproj(h, g, w, wqt, wvt, qaug, posf, mla, rope, l, B, S):
    M = B * S
    spt = S // TM
    ct, sa, sb, cost, sint = rope
    row = lambda width: pl.BlockSpec((TM, width), lambda i: (i, 0))
    tiles = lambda rows: pl.BlockSpec((1, TM // TKV, rows, TKV), lambda i: (i // spt, i % spt, 0, 0))
    tab = pl.BlockSpec((TM, LANES), lambda i: (i % spt, 0))
    tabt = pl.BlockSpec((MLA_ROPE // 2, TM), lambda i: (0, i % spt))
    return pl.pallas_call(
        _inproj_kernel,
        grid=(M // TM,),
        in_specs=[row(D_MODEL), _layer_spec(g, l), _layer_spec(w, l), _layer_spec(wqt, l), _layer_spec(wvt, l),
                  _const_spec(qaug), pl.BlockSpec((TM, W_KS), lambda i: (i % spt, 0))]
                 + [_layer_spec(a, l) for a in mla] + [tab, tab, tab, tabt, tabt],
        out_specs=[row(LANES), row(LANES), row(LANES), tiles(W_QN), row(W_KS), tiles(4 * NSA_DH),
                   tiles(MLA_HEADS * LANES), row(MLA_HEADS * LANES), tiles(MLA_HEADS * MLA_V)],
        out_shape=[jax.ShapeDtypeStruct((M, LANES), F32),
                   jax.ShapeDtypeStruct((M, LANES), F32),
                   jax.ShapeDtypeStruct((M, LANES), F32),
                   jax.ShapeDtypeStruct((B, S // TKV, W_QN, TKV), BF16),
                   jax.ShapeDtypeStruct((M, W_KS), BF16),
                   jax.ShapeDtypeStruct((B, S // TKV, 4 * NSA_DH, TKV), BF16),
                   jax.ShapeDtypeStruct((B, S // TKV, MLA_HEADS * LANES, TKV), BF16),
                   jax.ShapeDtypeStruct((M, MLA_HEADS * LANES), BF16),
                   jax.ShapeDtypeStruct((B, S // TKV, MLA_HEADS * MLA_V, TKV), BF16)],
        compiler_params=_cparams(("parallel",)),
        name="attn_inproj",
    )(h, g, w, wqt, wvt, qaug, posf, *mla, ct, sa, sb, cost, sint)


def _rope_slab(x, ct, sa, sb):
    return x * ct + pltpu.roll(x, LANES - MLA_ROPE // 2, 1) * sa + pltpu.roll(x, MLA_ROPE // 2, 1) * sb


def _mla_norms(za, qg_ref, kg_ref, ct_ref, sa_ref, sb_ref):
    cq = _rms(za[:, 0:MLA_Q_RANK], qg_ref[0]).astype(BF16)
    ckv = _rms(za[:, MLA_Q_RANK:MLA_Q_RANK + MLA_KV_RANK], kg_ref[0]).astype(BF16)
    kr = _rope_slab(za[:, 5 * LANES:6 * LANES], ct_ref[...], sa_ref[...], sb_ref[...])
    return cq, ckv, kr


def _mla_project(cq, ckv, kr, wqt_ref, wk_ref, wvt_ref, cost_ref, sint_ref, qt_ref, k_ref, vt_ref):
    k = _dot(ckv, wk_ref[0])
    for h in range(MLA_HEADS):
        sl = slice(h * LANES, (h + 1) * LANES)
        k_ref[:, sl] = (k[:, sl] + kr).astype(BF16)
    qt = _dot_nt(wqt_ref[0], cq)
    cos, sin = cost_ref[...], sint_ref[...]
    half = MLA_ROPE // 2
    slabs = []
    for h in range(MLA_HEADS):
        base = h * LANES
        x1 = qt[base + MLA_NOPE:base + MLA_NOPE + half, :]
        x2 = qt[base + MLA_NOPE + half:base + MLA_QK, :]
        slabs += [qt[base:base + MLA_NOPE, :], x1 * cos - x2 * sin, x2 * cos + x1 * sin,
                  qt[base + MLA_QK:base + LANES, :]]
    _store_tiles(qt_ref, jnp.concatenate(slabs, axis=0).astype(BF16))
    _store_tiles(vt_ref, _dot_nt(wvt_ref[0], ckv).astype(BF16))


ONES_ROWS = 16


def _stage_scores(s, s_ref, mx_ref, b, h):
    mx_ref[b, h] = jnp.max(s, axis=0, keepdims=True)
    s_ref[b, h, 0:s.shape[0], :] = s


def _online_update(vt, s_ref, mx_ref, b, m_ref, acc_ref, h):
    m_old = m_ref[h]
    m_new = jnp.maximum(m_old, mx_ref[b, h])
    p = jnp.exp2(s_ref[b, h] - m_new).astype(BF16)
    vt1 = jnp.concatenate([vt, jnp.ones((ONES_ROWS, vt.shape[1]), BF16)], axis=0)
    acc_ref[h] = jnp.exp2(m_old - m_new) * acc_ref[h] + _dot(vt1, p)
    m_ref[h] = m_new


def _normalised(acc_ref, h, dh):
    return acc_ref[h, 0:dh, :] / acc_ref[h, dh:dh + 1, :]


def _pipelined_sweep(i, n_heads, score, update, score_diag, tile0_staged=False):
    every = range(n_heads)

    if not tile0_staged:
        @pl.when(i >= 1)
        def _():
            score(0, 0, every)

    @pl.loop(0, i // 2)
    def _(j):
        for h in every:
            score(2 * j + 1, 1, [h])
            update(2 * j, 0, [h])
        for h in every:
            score(2 * j + 2, 0, [h])
            update(2 * j + 1, 1, [h])

    @pl.when(i % 2 == 1)
    def _():
        for h in every:
            score_diag(1, [h])
            update(i - 1, 0, [h])
        update(i, 1, every)

    @pl.when(i % 2 == 0)
    def _():
        score_diag(0, every)
        update(i, 0, every)


def _mla_attn_kernel(qt_ref, k_ref, vt_ref, o_ref, m_ref, acc_ref, s_ref, mx_ref):
    i = pl.program_id(1)
    tq = TQ_MLA
    m_ref[...] = jnp.full(m_ref.shape, -jnp.inf, F32)
    acc_ref[...] = jnp.zeros(acc_ref.shape, F32)

    every = range(MLA_HEADS)

    def score(kt, b, heads, keep=None):
        for h in heads:
            sl = slice(h * LANES, (h + 1) * LANES)
            k = k_ref[0, pl.ds(pl.multiple_of(kt * TKV, TKV), TKV), sl]
            s = _dot(k, qt_ref[0, 0, sl, :])
            if keep is not None:
                s = jnp.where(keep, s, NEG_INF)
            _stage_scores(s, s_ref, mx_ref, b, h)

    def update(kt, b, heads):
        for h in heads:
            _online_update(vt_ref[0, kt, h * MLA_V:(h + 1) * MLA_V, :], s_ref, mx_ref, b, m_ref, acc_ref, h)

    causal = lax.broadcasted_iota(jnp.int32, (TKV, tq), 0) <= lax.broadcasted_iota(jnp.int32, (TKV, tq), 1)
    _pipelined_sweep(i, MLA_HEADS, score, update, lambda b, heads: score(i, b, heads, causal))
    out = jnp.concatenate([_normalised(acc_ref, h, MLA_V) for h in range(MLA_HEADS)], axis=0)
    o_ref[0] = out.T.astype(BF16)


def _mla_attn(q, k, vt, B, S):
    nkt = S // TKV
    hw = MLA_HEADS * LANES
    return pl.pallas_call(
        _mla_attn_kernel,
        grid=(B, S // TQ_MLA),
        in_specs=[pl.BlockSpec((1, 1, hw, TQ_MLA), lambda b, i: (b, i, 0, 0)),
                  pl.BlockSpec((1, S, hw), lambda b, i: (b, 0, 0)),
                  pl.BlockSpec((1, nkt, MLA_HEADS * MLA_V, TKV), lambda b, i: (b, 0, 0, 0))],
        out_specs=pl.BlockSpec((1, TQ_MLA, MLA_HEADS * MLA_V), lambda b, i: (b, i, 0)),
        out_shape=jax.ShapeDtypeStruct((B, S, MLA_HEADS * MLA_V), BF16),
        scratch_shapes=[pltpu.VMEM((MLA_HEADS, 1, TQ_MLA), F32),
                        pltpu.VMEM((MLA_HEADS, MLA_V + ONES_ROWS, TQ_MLA), F32),
                        pltpu.VMEM((2, MLA_HEADS, TKV, TQ_MLA), F32), pltpu.VMEM((2, MLA_HEADS, 1, TQ_MLA), F32)],
        compiler_params=_cparams(("parallel", "arbitrary")),
        name="mla_attn",
    )(q, k, vt)


def _gelu_tanh(x):
    return 0.5 * x * (1.0 + jnp.tanh(np.sqrt(2.0 / np.pi).astype(np.float32) * (x + 0.044715 * (x * x * x))))


def _compress_kernel(kx_ref, vx_ref, pa_ref, pb_ref, wa_ref, wb_ref, w2k_ref, w2vt_ref, feat_ref, kc_ref, vct_ref):
    half = L_CMP // 2
    nc = kx_ref.shape[1] // half

    def hidden(x_ref, kv):
        x = jnp.concatenate([x_ref[0, pl.ds(l, nc, stride=half), :] for l in range(half)], axis=1)
        ya = _dot((x + pa_ref[0, kv]).astype(BF16), wa_ref[0, kv])
        yb = _dot((x + pb_ref[0, kv]).astype(BF16), wb_ref[0, kv])
        return _gelu_tanh(ya + pltpu.roll(yb, nc - 1, 0)).astype(BF16)

    kc_ref[0] = (_dot(hidden(kx_ref, 0), w2k_ref[0]) + feat_ref[...]).astype(BF16)
    vct_ref[0] = _dot_nt(w2vt_ref[0], hidden(vx_ref, 1)).astype(BF16)


def _compress(kx, vx, pa, pb, wa, wb, w2k, w2vt, feat, l):
    B, S, width = kx.shape
    nc = S // (L_CMP // 2)
    xs = pl.BlockSpec((1, S, width), lambda b: (b, 0, 0))
    return pl.pallas_call(
        _compress_kernel,
        grid=(B,),
        in_specs=[xs, xs] + [_layer_spec(a, l) for a in (pa, pb, wa, wb, w2k, w2vt)] + [_const_spec(feat)],
        out_specs=[pl.BlockSpec((1, nc, 2 * LANES), lambda b: (b, 0, 0)),
                   pl.BlockSpec((1, 2 * NSA_DH, nc), lambda b: (b, 0, 0))],
        out_shape=[jax.ShapeDtypeStruct((B, nc, 2 * LANES), BF16),
                   jax.ShapeDtypeStruct((B, 2 * NSA_DH, nc), BF16)],
        compiler_params=_cparams(("parallel",)),
        name="nsa_compress",
    )(kx, vx, pa, pb, wa, wb, w2k, w2vt, feat)


def _nsa_attn_kernel(qnt_ref, kc_ref, vct_ref, ks_ref, kw_ref, vst_ref, vwt_ref, gt_ref, ovt_ref, oh_ref,
                     o_ref, ocmp_ref, m_ref, acc_ref, mw_ref, accw_ref, s_ref, mx_ref, qs_ref):
    i = pl.program_id(1)
    tq = TQ_NSA
    G, R, dh = NSA_KV_HEADS, NSA_REP, NSA_DH
    nc = kc_ref.shape[1]
    nblk = ovt_ref.shape[0]
    t0 = i * tq
    tpos = t0 + lax.broadcasted_iota(jnp.int32, (1, tq), 1)
    q_of = lambda h: qnt_ref[0, 0, h * LANES:(h + 1) * LANES, :]

    for ref in (m_ref, mw_ref):
        ref[...] = jnp.full(ref.shape, -jnp.inf, F32)
    for ref in (acc_ref, accw_ref):
        ref[...] = jnp.zeros(ref.shape, F32)

    every = range(G * R)
    rel = lax.broadcasted_iota(jnp.int32, (TKV, tq), 0) - lax.broadcasted_iota(jnp.int32, (TKV, tq), 1)
    causal = rel <= 0

    def score(kt, b, heads, sel, keep=None):
        rows = pl.ds(pl.multiple_of(kt * TKV, TKV), TKV)
        for h in heads:
            g = h // R
            if sel:
                k = jnp.concatenate([ks_ref[0, rows, g * LANES:(g + 1) * LANES], oh_ref[rows, :]], axis=1)
            else:
                k = kw_ref[0, rows, g * LANES:(g + 1) * LANES]
            s = _dot(k, qs_ref[h] if sel else q_of(h))
            if keep is not None:
                s = jnp.where(keep, s, NEG_INF)
            _stage_scores(s, s_ref, mx_ref, b, h)

    def update(kt, b, heads, sel):
        vt_ref, state = (vst_ref, (m_ref, acc_ref)) if sel else (vwt_ref, (mw_ref, accw_ref))
        for h in heads:
            g = h // R
            _online_update(vt_ref[0, kt, g * dh:(g + 1) * dh, :], s_ref, mx_ref, b, *state, h)

    assert W_WIN == 2 * TKV
    kt1, kt2 = jnp.maximum(i - 1, 0), jnp.maximum(i - 2, 0)
    keep1 = jnp.broadcast_to(i >= 1, (TKV, tq))
    keep2 = (rel > 0) & (i >= 2)

    nrow = lax.broadcasted_iota(jnp.int32, (nc, 1), 0)
    bias_c = jnp.where((tpos >= nrow * D_CMP + (L_CMP - 1)) & (nrow < nc - 1), 0.0, NEG_INF)
    has_cmp = tpos >= L_CMP - 1
    for h in every:
        g = h // R
        _stage_scores(_dot(kc_ref[0, :, g * LANES:(g + 1) * LANES], q_of(h)) + bias_c, s_ref, mx_ref, 0, h)
    score(i, 1, every, False, causal)
    for h in every:
        score(kt1, 2, [h], False, keep1)
        update(i, 1, [h], False)

    imps = [None] * G
    for h in every:
        g = h // R
        e = jnp.exp2(s_ref[0, h, 0:nc, :] - mx_ref[0, h])
        inv = jnp.where(has_cmp, 1.0 / jnp.sum(e, axis=0, keepdims=True), 0.0)
        both = _dot(jnp.concatenate([vct_ref[0, g * dh:(g + 1) * dh, :], ovt_ref[...]], axis=0), e.astype(BF16))
        ocmp_ref[h] = both[0:dh, :] * inv
        contrib = both[dh:dh + nblk, :] * inv
        imps[g] = contrib if imps[g] is None else imps[g] + contrib
        score(kt2, 3, [h], False, keep2)
        update(kt1, 2, [h], False)
    update(kt2, 3, every, False)

    cur = lax.shift_right_logical(tpos, 6)
    jrow = lax.broadcasted_iota(jnp.int32, (nblk, 1), 0)
    sub = lax.broadcasted_iota(jnp.int32, (8, 1), 0)
    n_slab = nblk // 8

    for g in range(G):
        imp = jnp.where(jrow > cur, -jnp.inf, imps[g])
        imp = jnp.where((jrow == 0) | (jrow == cur) | (jrow == cur - 1), jnp.inf, imp)
        slabs = [imp[8 * v:8 * v + 8, :] for v in range(n_slab)]
        ranks = [jnp.zeros((8, tq), F32) for _ in range(n_slab)]
        for jj in range(nblk):
            row = imp[jj:jj + 1, :]
            for v in range(n_slab):
                if v > jj // 8:
                    beats = row >= slabs[v]
                elif v < jj // 8:
                    beats = row > slabs[v]
                else:
                    beats = (row > slabs[v]) | ((row == slabs[v]) & (sub > jj % 8))
                ranks[v] = ranks[v] + jnp.where(beats, 1.0, 0.0)
        mb = jnp.where(jnp.concatenate(ranks, axis=0) < float(min(N_SEL, nblk)), 0.0, NEG_INF)
        mb = jnp.concatenate([mb, jnp.zeros((LANES - nblk, tq), F32)], axis=0).astype(BF16)
        for r in range(R):
            qs_ref[g * R + r] = jnp.concatenate([q_of(g * R + r), mb], axis=0)
        score(0, 0, range(g * R, (g + 1) * R), True)

    _pipelined_sweep(i, G * R, functools.partial(score, sel=True), functools.partial(update, sel=True),
                     lambda b, heads: score(i, b, heads, True, causal), tile0_staged=True)

    sig = 1.0 / (1.0 + jnp.exp(-gt_ref[0].T[0:G * 16, :]))
    outs = []
    for h in range(G * R):
        row = (h // R) * 16 + (h % R) * 4
        outs.append(sig[row:row + 1, :] * ocmp_ref[h] + sig[row + 1:row + 2, :] * _normalised(acc_ref, h, dh)
                    + sig[row + 2:row + 3, :] * _normalised(accw_ref, h, dh))
    o_ref[0] = jnp.concatenate(outs, axis=0).T.astype(BF16)


def _nsa_attn(qnt, kc, vct, ksw, vt, gt, ovt, oh, B, S):
    nkt = S // TKV
    nc = kc.shape[1]
    G = NSA_KV_HEADS
    ksw3 = ksw.reshape(B, S, W_KS)
    gt3 = gt.reshape(B, S, LANES)
    H, dh, tq = NSA_HEADS, NSA_DH, TQ_NSA
    state = [pltpu.VMEM((H, 1, tq), F32), pltpu.VMEM((H, dh + ONES_ROWS, tq), F32)]
    return pl.pallas_call(
        _nsa_attn_kernel,
        grid=(B, S // tq),
        in_specs=[pl.BlockSpec((1, 1, W_QN, tq), lambda b, i: (b, i, 0, 0)),
                  pl.BlockSpec((1, nc, G * LANES), lambda b, i: (b, 0, 0)),
                  pl.BlockSpec((1, G * dh, nc), lambda b, i: (b, 0, 0)),
                  pl.BlockSpec((1, S, G * LANES), lambda b, i: (b, 0, 0)),
                  pl.BlockSpec((1, S, G * LANES), lambda b, i: (b, 0, 1)),
                  pl.BlockSpec((1, nkt, G * dh, TKV), lambda b, i: (b, 0, 0, 0)),
                  pl.BlockSpec((1, nkt, G * dh, TKV), lambda b, i: (b, 0, 1, 0)),
                  pl.BlockSpec((1, tq, LANES), lambda b, i: (b, i, 0)),
                  _const_spec(ovt), _const_spec(oh)],
        out_specs=pl.BlockSpec((1, tq, H * dh), lambda b, i: (b, i, 0)),
        out_shape=jax.ShapeDtypeStruct((B, S, H * dh), BF16),
        scratch_shapes=([pltpu.VMEM((H, dh, tq), F32)] + state + state
                        + [pltpu.VMEM((4, H, TKV, tq), F32), pltpu.VMEM((4, H, 1, tq), F32),
                           pltpu.VMEM((H, 2 * LANES, tq), BF16)]),
        compiler_params=_cparams(("parallel", "arbitrary")),
        name="nsa_attn",
    )(qnt, kc, vct, ksw3, ksw3, vt, vt, gt3, ovt, oh)


FF_CHUNK = 256
HALO = 16
SUB = 8
QV = TM // SUB


def _ffn_kernel(x_ref, xp_ref, a_ref, ap_ref, b_ref, bp_ref, wo_ref, g_ref, wup_ref, cw_ref, cb_ref, wdn_ref, fg_ref,
                o_ref, n_ref, p_ref, act_ref, *, final, spt):
    assert CONV_W == 3
    i = pl.program_id(0)
    ka = a_ref.shape[1]
    n_slab = D_MODEL // LANES

    def attn_residual(h_rows, a_rows, b_rows):
        return h_rows[...] + _dot(a_rows[...], wo_ref[0, 0:ka, :]) + _dot(b_rows[...], wo_ref[0, ka:, :])

    o_ref[...] = attn_residual(x_ref, a_ref, b_ref)
    keep = jnp.where(i % spt == 0, 0.0, 1.0)
    n_ref[0:HALO, :] = (_rms(attn_residual(xp_ref, ap_ref, bp_ref), g_ref[0]) * keep).astype(BF16)
    nt = _rms(o_ref[...], g_ref[0])
    for c in range(n_slab):
        for s in range(SUB):
            p_ref[c, pl.ds(s, QV, stride=SUB), :] = nt[s * QV:(s + 1) * QV, c * LANES:(c + 1) * LANES]
    for c in range(n_slab):
        n_ref[HALO:HALO + TM, c * LANES:(c + 1) * LANES] = p_ref[c].astype(BF16)

    first_sublane = lax.broadcasted_iota(jnp.int32, (SUB, 1), 0) == 0

    def conv(col):
        hx = _dot(n_ref[...], wup_ref[0, :, col:col + FF_CHUNK])
        main = hx[HALO:HALO + TM, :]
        wrap1 = jnp.where(first_sublane, hx[HALO - 1:HALO, :], pltpu.roll(main[TM - SUB:TM, :], 1, 0))
        wrap2 = jnp.where(first_sublane, hx[HALO - 2:HALO - 1, :], pltpu.roll(main[TM - 2 * SUB:TM - SUB, :], 1, 0))
        tap1 = jnp.concatenate([wrap1, main[0:TM - SUB, :]], axis=0)
        tap2 = jnp.concatenate([wrap2, wrap1, main[0:TM - 2 * SUB, :]], axis=0)
        cw = cw_ref[0, :, col:col + FF_CHUNK]
        return cb_ref[0, :, col:col + FF_CHUNK] + tap2 * cw[0:1, :] + tap1 * cw[1:2, :] + main * cw[2:3, :]

    for c in range(D_FF // FF_CHUNK):
        col = c * FF_CHUNK
        gate = conv(col)
        up = conv(D_FF + col)
        act_ref[:, col:col + FF_CHUNK] = (gate * (1.0 / (1.0 + jnp.exp(-gate))) * up).astype(BF16)
    down = _dot(act_ref[...], wdn_ref[0])
    for c in range(n_slab):
        p_ref[c] = down[:, c * LANES:(c + 1) * LANES]
    for c in range(n_slab):
        for s in range(SUB):
            rows, lanes = slice(s * QV, (s + 1) * QV), slice(c * LANES, (c + 1) * LANES)
            o_ref[rows, lanes] = o_ref[rows, lanes] + p_ref[c, pl.ds(s, QV, stride=SUB), :]
    if final:
        o_ref[...] = _rms(o_ref[...], fg_ref[...])


def _outproj_ffn(h, oa, ob, wo, g, wup, cw, cb, wdn, fg, l, S, final):
    M = h.shape[0]
    spt = S // TM
    row = lambda width: pl.BlockSpec((TM, width), lambda i: (i, 0))
    prev = lambda width: pl.BlockSpec((HALO, width), lambda i: (jnp.maximum(i * (TM // HALO) - 1, 0), 0))
    return pl.pallas_call(
        functools.partial(_ffn_kernel, final=final, spt=spt),
        grid=(M // TM,),
        in_specs=[row(D_MODEL), prev(D_MODEL), row(oa.shape[1]), prev(oa.shape[1]), row(ob.shape[1]),
                  prev(ob.shape[1]), _layer_spec(wo, l), _layer_spec(g, l), _layer_spec(wup, l), _layer_spec(cw, l),
                  _layer_spec(cb, l), _layer_spec(wdn, l), _const_spec(fg)],
        out_specs=row(D_MODEL),
        out_shape=jax.ShapeDtypeStruct((M, D_MODEL), F32),
        scratch_shapes=[pltpu.VMEM((HALO + TM, D_MODEL), BF16), pltpu.VMEM((D_MODEL // LANES, TM, LANES), F32),
                        pltpu.VMEM((TM, D_FF), BF16)],
        compiler_params=_cparams(("parallel",)),
        name="outproj_ffn",
    )(h, h, oa, oa, ob, ob, wo, g, wup, cw, cb, wdn, fg)


def _slabs(w, n, width, pad_to=LANES):
    lead = w.shape[:-1]
    w = w.reshape(lead + (n, width))
    w = jnp.pad(w, [(0, 0)] * (w.ndim - 1) + [(0, pad_to - width)])
    return w.reshape(lead + (n * pad_to,))


def _inproj_weights(w_in):
    G, dh = NSA_KV_HEADS, NSA_DH
    cols = lambda off, n: w_in[..., off:off + n]
    zeros = lambda n: jnp.zeros(w_in.shape[:-1] + (n,), w_in.dtype)
    kr = jnp.concatenate([zeros(MLA_NOPE), cols(OFF_KROPE, MLA_ROPE), zeros(LANES - MLA_QK)], axis=-1)
    gates = _slabs(cols(OFF_GT, 3 * NSA_HEADS), NSA_HEADS, 3, 4)
    gates = jnp.concatenate([gates, zeros(LANES - 4 * NSA_HEADS)], axis=-1)
    w = jnp.concatenate([cols(OFF_CQ, MLA_Q_RANK), cols(OFF_CKV, MLA_KV_RANK), kr,
                         gates, cols(OFF_KC, G * dh), cols(OFF_VC, G * dh),
                         _slabs(cols(OFF_KS, G * dh), G, dh), _slabs(cols(OFF_KW, G * dh), G, dh)], axis=-1)
    assert w.shape[-1] == W_ROW
    qn = cols(OFF_QN, NSA_HEADS * dh) * (NSA_DH ** -0.5 * LOG2E)
    wvt = jnp.concatenate([cols(OFF_VS, G * dh), cols(OFF_VW, G * dh)], axis=-1)
    return w.astype(BF16), jnp.swapaxes(qn, 1, 2).astype(BF16), jnp.swapaxes(wvt, 1, 2).astype(BF16)


def _bf16_terms(x, n):
    terms = []
    for _ in range(n):
        bits = np.array(x, np.float32).view(np.uint32)
        bits = (bits + np.uint32(0x7FFF) + ((bits >> np.uint32(16)) & np.uint32(1))) & np.uint32(0xFFFF0000)
        terms.append(float(bits.view(np.float32)))
        x = x - terms[-1]
    return terms


N_POS_TERMS = 3


def _pos_features(pos, n_slabs):
    out = np.zeros((len(pos), n_slabs * LANES), np.float32)
    for s in range(n_slabs):
        base = s * LANES + NSA_DH
        out[:, base:base + N_POS_TERMS] = (pos // L_SEL)[:, None]
        out[:, base + N_POS_TERMS:base + 2 * N_POS_TERMS] = (pos % L_SEL)[:, None]
    return out


def _const_tables(S):
    slopes = 2.0 ** (-8.0 * np.arange(1, NSA_HEADS + 1) / NSA_HEADS)
    terms = np.array(_bf16_terms(LOG2E, N_POS_TERMS))
    pad = LANES - NSA_DH
    qaug = np.zeros((NSA_HEADS * pad, 1), np.float32)
    for h in range(NSA_HEADS):
        base = h * pad
        qaug[base:base + N_POS_TERMS, 0] = slopes[h] * L_SEL * terms
        qaug[base + N_POS_TERMS:base + 2 * N_POS_TERMS, 0] = slopes[h] * terms
    posf = _pos_features(np.arange(S), 4)
    nc = S // D_CMP
    feat = _pos_features(np.arange(nc) * D_CMP + L_CMP - 1, 2)
    n_cmp = (S - L_CMP) // D_CMP + 1
    nblk = S // L_SEL
    cs = np.arange(n_cmp) * D_CMP
    ss = np.arange(nblk) * L_SEL
    ov = ((cs[:, None] < ss[None, :] + L_SEL) & (cs[:, None] + L_CMP > ss[None, :])).astype(np.float32)
    ovt = np.zeros((nblk, nc), np.float32)
    ovt[:, :n_cmp] = ov.T
    assert nblk <= LANES
    onehot = (np.arange(S)[:, None] // L_SEL == np.arange(LANES)[None, :]).astype(np.float32)
    return (jnp.asarray(qaug), jnp.asarray(posf), jnp.asarray(feat), jnp.asarray(ovt, BF16),
            jnp.asarray(onehot, BF16))


def _rope_consts(S):
    half = MLA_ROPE // 2
    inv = (1.0 / (np.float32(ROPE_THETA) ** (np.arange(0, MLA_ROPE, 2, dtype=np.float32) / MLA_ROPE))).astype(np.float32)
    ang = np.arange(S, dtype=np.float32)[:, None] * inv[None, :]
    cos, sin = np.cos(ang), np.sin(ang)
    z = lambda w: np.zeros((S, w), np.float32)
    ct = np.concatenate([np.ones((S, MLA_NOPE), np.float32), cos, cos, z(LANES - MLA_QK)], axis=1)
    sa = np.concatenate([z(MLA_NOPE), -sin, z(LANES - MLA_NOPE - half)], axis=1)
    sb = np.concatenate([z(MLA_NOPE + half), sin, z(LANES - MLA_QK)], axis=1)
    return tuple(jnp.asarray(np.ascontiguousarray(t)) for t in (ct, sa, sb, cos.T, sin.T))


def _mla_weights(w_uq, w_ukv):
    wq = _slabs(w_uq, MLA_HEADS, MLA_QK) * (MLA_QK ** -0.5 * LOG2E)
    kv = w_ukv.reshape(w_ukv.shape[:-1] + (MLA_HEADS, MLA_NOPE + MLA_V))
    wk = _slabs(kv[..., :MLA_NOPE].reshape(w_ukv.shape[:-1] + (MLA_HEADS * MLA_NOPE,)), MLA_HEADS, MLA_NOPE)
    wvt = jnp.swapaxes(kv[..., MLA_NOPE:].reshape(w_ukv.shape[:-1] + (MLA_HEADS * MLA_V,)), 1, 2)
    return jnp.swapaxes(wq, 1, 2).astype(BF16), wk.astype(BF16), wvt.astype(BF16)


def _compress_weights(pos, w1, w2k, w2v):
    G, dh, half = NSA_KV_HEADS, NSA_DH, L_CMP // 2
    L = pos.shape[0]
    eye = jnp.eye(G, dtype=F32)

    def first_layer(w):
        return jnp.einsum('nclde,gk->nclgdke', w, eye).reshape(L, 2, half * G * dh, G * CMP_HID).astype(BF16)

    def pos_row(p):
        return jnp.broadcast_to(p[:, :, :, None, :], (L, 2, half, G, dh)).reshape(L, 2, 1, half * G * dh)

    w2kp = jnp.pad(w2k, ((0, 0), (0, 0), (0, LANES - dh)))
    w2k_bd = jnp.einsum('ned,gk->ngekd', w2kp, eye).reshape(L, G * CMP_HID, G * LANES).astype(BF16)
    w2vt_bd = jnp.einsum('ned,gk->ngdke', w2v, eye).reshape(L, G * dh, G * CMP_HID).astype(BF16)
    return (pos_row(pos[:, :, :half]), pos_row(pos[:, :, half:]), first_layer(w1[:, :, :half]),
            first_layer(w1[:, :, half:]), w2k_bd, w2vt_bd)


def kernel(x, attn_norm, w_in, q_norm, kv_norm, w_uq, w_ukv, cmp_pos_k, cmp_pos_v, cmp_k_w1, cmp_k_w2, cmp_v_w1,
           cmp_v_w2, w_o, ffn_norm, w_up, conv_w, conv_b, w_down, final_norm):
    B, S, D = x.shape
    M = B * S
    assert D == D_MODEL and S % TM == 0 and TQ_MLA == TKV and TQ_NSA == TKV and S % TKV == 0
    qaug, posf, feat, ovt, onehot = _const_tables(S)
    rope = _rope_consts(S)

    w_row, w_qt, w_vt = _inproj_weights(w_in)
    mqt, mk, mvt = _mla_weights(w_uq, w_ukv)
    pa, pb, cwa, cwb, w2k, w2vt = _compress_weights(jnp.stack([cmp_pos_k, cmp_pos_v], axis=1),
                                                    jnp.stack([cmp_k_w1, cmp_v_w1], axis=1), cmp_k_w2, cmp_v_w2)
    wo, wup, wdn = w_o.astype(BF16), w_up.astype(BF16), w_down.astype(BF16)
    g_attn, g_q, g_kv, g_ffn = (a[:, None, :] for a in (attn_norm, q_norm, kv_norm, ffn_norm))
    cb = conv_b[:, None, :]
    fg = final_norm[None, :]

    h = x.reshape(M, D)
    for l in range(DEPTH):
        gt, kcr, vcr, qnt, ksw, vt, qt, k, vmt = _inproj(h, g_attn, w_row, w_qt, w_vt, qaug, posf,
                                                          (g_q, g_kv, mqt, mk, mvt), rope, l, B, S)
        o_mla = _mla_attn(qt, k.reshape(B, S, -1), vmt, B, S)
        kc, vct = _compress(kcr.reshape(B, S, -1), vcr.reshape(B, S, -1), pa, pb, cwa, cwb, w2k, w2vt, feat, l)
        o_nsa = _nsa_attn(qnt, kc, vct, ksw, vt, gt, ovt, onehot, B, S)
        h = _outproj_ffn(h, o_mla.reshape(M, -1), o_nsa.reshape(M, -1), wo, g_ffn, wup, conv_w, cb, wdn, fg, l, S,
                         final=(l == DEPTH - 1))
    return h.reshape(B, S, D)
```

```python
import functools

import numpy as np
import jax
import jax.numpy as jnp
from jax import lax
from jax.experimental import pallas as pl
from jax.experimental.pallas import tpu as pltpu

F32 = jnp.float32
BF16 = jnp.bfloat16

D_MODEL = 1024
DEPTH = 2
EPS = 1e-6
NEG_INF = -1e30
LOG2E = float(np.log2(np.e))
LANES = 128

MLA_HEADS = 8
MLA_NOPE = 64
MLA_ROPE = 32
MLA_V = 64
MLA_Q_RANK = 384
MLA_KV_RANK = 256
MLA_QK = MLA_NOPE + MLA_ROPE
ROPE_THETA = 10000.0

NSA_HEADS = 8
NSA_KV_HEADS = 2
NSA_REP = NSA_HEADS // NSA_KV_HEADS
NSA_DH = 64
L_CMP = 32
D_CMP = 16
CMP_HID = 128
L_SEL = 64
N_SEL = 16
W_WIN = 512

D_FF = 2816
CONV_W = 3

_SIZES = [MLA_Q_RANK, MLA_KV_RANK, MLA_ROPE, NSA_HEADS * NSA_DH] + [NSA_KV_HEADS * NSA_DH] * 6 + [3 * NSA_HEADS]
_OFFS = [0] + [int(o) for o in np.cumsum(_SIZES)]
(OFF_CQ, OFF_CKV, OFF_KROPE, OFF_QN, OFF_KC, OFF_VC, OFF_KS, OFF_VS, OFF_KW, OFF_VW, OFF_GT, D_IN) = _OFFS

TM = 512
TKV = 256
TQ_MLA = 256
TQ_NSA = 256
VMEM_LIMIT = 56 * 1024 * 1024

W_ZA = 6 * LANES
W_ZB = 3 * LANES
W_KS = 4 * LANES
W_ROW = W_ZA + W_ZB + W_KS
W_QN = NSA_HEADS * LANES


def _cparams(sem):
    return pltpu.CompilerParams(dimension_semantics=sem, vmem_limit_bytes=VMEM_LIMIT)


def _layer_spec(a, l):
    return pl.BlockSpec((1,) + a.shape[1:], lambda *_: (l,) + (0,) * (a.ndim - 1))


def _const_spec(a):
    return pl.BlockSpec(a.shape, lambda *_: (0,) * a.ndim)


def _dot(a, b):
    return jnp.dot(a, b, preferred_element_type=F32)


def _dot_nt(a, b):
    return lax.dot_general(a, b, (((1,), (1,)), ((), ())), preferred_element_type=F32)


def _rms(x, g):
    return x * lax.rsqrt(jnp.mean(x * x, axis=-1, keepdims=True) + EPS) * g


def _store_tiles(ref, xt):
    for c in range(TM // TKV):
        ref[0, c] = xt[:, c * TKV:(c + 1) * TKV]


def _inproj_kernel(x_ref, g_ref, w_ref, wqt_ref, wvt_ref, qaug_ref, posf_ref,
                   qg_ref, kg_ref, mwqt_ref, mwk_ref, mwvt_ref, ct_ref, sa_ref, sb_ref, cost_ref, sint_ref,
                   gt_ref, kcr_ref, vcr_ref, qnt_ref, ksw_ref, vt_ref, mqt_ref, mk_ref, mvt_ref):
    n = _rms(x_ref[...], g_ref[0]).astype(BF16)
    c0, c1 = W_ZA, W_ZA + W_ZB
    cq, ckv, kr = _mla_norms(_dot(n, w_ref[0, :, 0:c0]), qg_ref, kg_ref, ct_ref, sa_ref, sb_ref)
    zb = _dot(n, w_ref[0, :, c0:c1])
    gt_ref[...] = zb[:, 0:LANES]
    kcr_ref[...] = zb[:, LANES:2 * LANES]
    vcr_ref[...] = zb[:, 2 * LANES:3 * LANES]
    ksw_ref[...] = (_dot(n, w_ref[0, :, c1:W_ROW]) + posf_ref[...]).astype(BF16)
    qt = _dot_nt(wqt_ref[0], n)
    slabs, pad = [], LANES - NSA_DH
    for h in range(NSA_HEADS):
        slabs += [qt[h * NSA_DH:(h + 1) * NSA_DH, :], jnp.broadcast_to(qaug_ref[h * pad:(h + 1) * pad, :], (pad, TM))]
    _store_tiles(qnt_ref, jnp.concatenate(slabs, axis=0).astype(BF16))
    _store_tiles(vt_ref, _dot_nt(wvt_ref[0], n).astype(BF16))
    _mla_project(cq, ckv, kr, mwqt_ref, mwk_ref, mwvt_ref, cost_ref, sint_ref, mqt_ref, mk_ref, mvt_ref)


def _inproj(h, g, w, wqt, wvt, qaug, posf, mla, rope, l, B, S):
    M = B * S
    spt = S // TM
    ct, sa, sb, cost, sint = rope
    row = lambda width: pl.BlockSpec((TM, width), lambda i: (i, 0))
    tiles = lambda rows: pl.BlockSpec((1, TM // TKV, rows, TKV), lambda i: (i // spt, i % spt, 0, 0))
    tab = pl.BlockSpec((TM, LANES), lambda i: (i % spt, 0))
    tabt = pl.BlockSpec((MLA_ROPE // 2, TM), lambda i: (0, i % spt))
    return pl.pallas_call(
        _inproj_kernel,
        grid=(M // TM,),
        in_specs=[row(D_MODEL), _layer_spec(g, l), _layer_spec(w, l), _layer_spec(wqt, l), _layer_spec(wvt, l),
                  _const_spec(qaug), pl.BlockSpec((TM, W_KS), lambda i: (i % spt, 0))]
                 + [_layer_spec(a, l) for a in mla] + [tab, tab, tab, tabt, tabt],
        out_specs=[row(LANES), row(LANES), row(LANES), tiles(W_QN), row(W_KS), tiles(4 * NSA_DH),
                   tiles(MLA_HEADS * LANES), row(MLA_HEADS * LANES), tiles(MLA_HEADS * MLA_V)],
        out_shape=[jax.ShapeDtypeStruct((M, LANES), F32),
                   jax.ShapeDtypeStruct((M, LANES), F32),
                   jax.ShapeDtypeStruct((M, LANES), F32),
                   jax.ShapeDtypeStruct((B, S // TKV, W_QN, TKV), BF16),
                   jax.ShapeDtypeStruct((M, W_KS), BF16),
                   jax.ShapeDtypeStruct((B, S // TKV, 4 * NSA_DH, TKV), BF16),
                   jax.ShapeDtypeStruct((B, S // TKV, MLA_HEADS * LANES, TKV), BF16),
                   jax.ShapeDtypeStruct((M, MLA_HEADS * LANES), BF16),
                   jax.ShapeDtypeStruct((B, S // TKV, MLA_HEADS * MLA_V, TKV), BF16)],
        compiler_params=_cparams(("parallel",)),
        name="attn_inproj",
    )(h, g, w, wqt, wvt, qaug, posf, *mla, ct, sa, sb, cost, sint)


def _rope_slab(x, ct, sa, sb):
    return x * ct + pltpu.roll(x, LANES - MLA_ROPE // 2, 1) * sa + pltpu.roll(x, MLA_ROPE // 2, 1) * sb


def _mla_norms(za, qg_ref, kg_ref, ct_ref, sa_ref, sb_ref):
    cq = _rms(za[:, 0:MLA_Q_RANK], qg_ref[0]).astype(BF16)
    ckv = _rms(za[:, MLA_Q_RANK:MLA_Q_RANK + MLA_KV_RANK], kg_ref[0]).astype(BF16)
    kr = _rope_slab(za[:, 5 * LANES:6 * LANES], ct_ref[...], sa_ref[...], sb_ref[...])
    return cq, ckv, kr


def _mla_project(cq, ckv, kr, wqt_ref, wk_ref, wvt_ref, cost_ref, sint_ref, qt_ref, k_ref, vt_ref):
    k = _dot(ckv, wk_ref[0])
    for h in range(MLA_HEADS):
        sl = slice(h * LANES, (h + 1) * LANES)
        k_ref[:, sl] = (k[:, sl] + kr).astype(BF16)
    qt = _dot_nt(wqt_ref[0], cq)
    cos, sin = cost_ref[...], sint_ref[...]
    half = MLA_ROPE // 2
    slabs = []
    for h in range(MLA_HEADS):
        base = h * LANES
        x1 = qt[base + MLA_NOPE:base + MLA_NOPE + half, :]
        x2 = qt[base + MLA_NOPE + half:base + MLA_QK, :]
        slabs += [qt[base:base + MLA_NOPE, :], x1 * cos - x2 * sin, x2 * cos + x1 * sin,
                  qt[base + MLA_QK:base + LANES, :]]
    _store_tiles(qt_ref, jnp.concatenate(slabs, axis=0).astype(BF16))
    _store_tiles(vt_ref, _dot_nt(wvt_ref[0], ckv).astype(BF16))


ONES_ROWS = 16


def _stage_scores(s, s_ref, mx_ref, b, h):
    mx_ref[b, h] = jnp.max(s, axis=0, keepdims=True)
    s_ref[b, h, 0:s.shape[0], :] = s


def _online_update(vt, s_ref, mx_ref, b, m_ref, acc_ref, h):
    m_old = m_ref[h]
    m_new = jnp.maximum(m_old, mx_ref[b, h])
    p = jnp.exp2(s_ref[b, h] - m_new).astype(BF16)
    vt1 = jnp.concatenate([vt, jnp.ones((ONES_ROWS, vt.shape[1]), BF16)], axis=0)
    acc_ref[h] = jnp.exp2(m_old - m_new) * acc_ref[h] + _dot(vt1, p)
    m_ref[h] = m_new


def _normalised(acc_ref, h, dh):
    return acc_ref[h, 0:dh, :] / acc_ref[h, dh:dh + 1, :]


def _pipelined_sweep(i, n_heads, score, update, score_diag, tile0_staged=False):
    every = range(n_heads)

    if not tile0_staged:
        @pl.when(i >= 1)
        def _():
            score(0, 0, every)

    def pair(j):
        for h in every:
            score(2 * j + 1, 1, [h])
            update(2 * j, 0, [h])
        for h in every:
            score(2 * j + 2, 0, [h])
            update(2 * j + 1, 1, [h])

    n_pairs = i // 2

    @pl.loop(0, n_pairs // 2)
    def _(jj):
        pair(2 * jj)
        pair(2 * jj + 1)

    @pl.when(n_pairs % 2 == 1)
    def _():
        pair(n_pairs - 1)

    @pl.when(i % 2 == 1)
    def _():
        for h in every:
            score_diag(1, [h])
            update(i - 1, 0, [h])
        update(i, 1, every)

    @pl.when(i % 2 == 0)
    def _():
        score_diag(0, every)
        update(i, 0, every)


def _mla_attn_kernel(qt_ref, k_ref, vt_ref, o_ref, m_ref, acc_ref, s_ref, mx_ref):
    i = pl.program_id(1)
    tq = TQ_MLA
    m_ref[...] = jnp.full(m_ref.shape, -jnp.inf, F32)
    acc_ref[...] = jnp.zeros(acc_ref.shape, F32)

    every = range(MLA_HEADS)

    def score(kt, b, heads, keep=None):
        for h in heads:
            sl = slice(h * LANES, (h + 1) * LANES)
            k = k_ref[0, pl.ds(pl.multiple_of(kt * TKV, TKV), TKV), sl]
            s = _dot(k, qt_ref[0, 0, sl, :])
            if keep is not None:
                s = jnp.where(keep, s, NEG_INF)
            _stage_scores(s, s_ref, mx_ref, b, h)

    def update(kt, b, heads):
        for h in heads:
            _online_update(vt_ref[0, kt, h * MLA_V:(h + 1) * MLA_V, :], s_ref, mx_ref, b, m_ref, acc_ref, h)

    causal = lax.broadcasted_iota(jnp.int32, (TKV, tq), 0) <= lax.broadcasted_iota(jnp.int32, (TKV, tq), 1)
    _pipelined_sweep(i, MLA_HEADS, score, update, lambda b, heads: score(i, b, heads, causal))
    out = jnp.concatenate([_normalised(acc_ref, h, MLA_V) for h in range(MLA_HEADS)], axis=0)
    o_ref[0] = out.T.astype(BF16)


def _mla_attn(q, k, vt, B, S):
    nkt = S // TKV
    hw = MLA_HEADS * LANES
    return pl.pallas_call(
        _mla_attn_kernel,
        grid=(B, S // TQ_MLA),
        in_specs=[pl.BlockSpec((1, 1, hw, TQ_MLA), lambda b, i: (b, i, 0, 0)),
                  pl.BlockSpec((1, S, hw), lambda b, i: (b, 0, 0)),
                  pl.BlockSpec((1, nkt, MLA_HEADS * MLA_V, TKV), lambda b, i: (b, 0, 0, 0))],
        out_specs=pl.BlockSpec((1, TQ_MLA, MLA_HEADS * MLA_V), lambda b, i: (b, i, 0)),
        out_shape=jax.ShapeDtypeStruct((B, S, MLA_HEADS * MLA_V), BF16),
        scratch_shapes=[pltpu.VMEM((MLA_HEADS, 1, TQ_MLA), F32),
                        pltpu.VMEM((MLA_HEADS, MLA_V + ONES_ROWS, TQ_MLA), F32),
                        pltpu.VMEM((2, MLA_HEADS, TKV, TQ_MLA), F32), pltpu.VMEM((2, MLA_HEADS, 1, TQ_MLA), F32)],
        compiler_params=_cparams(("parallel", "arbitrary")),
        name="mla_attn",
    )(q, k, vt)


def _gelu_tanh(x):
    return 0.5 * x * (1.0 + jnp.tanh(np.sqrt(2.0 / np.pi).astype(np.float32) * (x + 0.044715 * (x * x * x))))


def _compress_kernel(kx_ref, vx_ref, pa_ref, pb_ref, wa_ref, wb_ref, w2k_ref, w2vt_ref, feat_ref, kc_ref, vct_ref):
    half = L_CMP // 2
    nc = kx_ref.shape[1] // half

    def hidden(x_ref, kv):
        x = jnp.concatenate([x_ref[0, pl.ds(l, nc, stride=half), :] for l in range(half)], axis=1)
        ya = _dot((x + pa_ref[0, kv]).astype(BF16), wa_ref[0, kv])
        yb = _dot((x + pb_ref[0, kv]).astype(BF16), wb_ref[0, kv])
        return _gelu_tanh(ya + pltpu.roll(yb, nc - 1, 0)).astype(BF16)

    kc_ref[0] = (_dot(hidden(kx_ref, 0), w2k_ref[0]) + feat_ref[...]).astype(BF16)
    vct_ref[0] = _dot_nt(w2vt_ref[0], hidden(vx_ref, 1)).astype(BF16)


def _compress(kx, vx, pa, pb, wa, wb, w2k, w2vt, feat, l):
    B, S, width = kx.shape
    nc = S // (L_CMP // 2)
    xs = pl.BlockSpec((1, S, width), lambda b: (b, 0, 0))
    return pl.pallas_call(
        _compress_kernel,
        grid=(B,),
        in_specs=[xs, xs] + [_layer_spec(a, l) for a in (pa, pb, wa, wb, w2k, w2vt)] + [_const_spec(feat)],
        out_specs=[pl.BlockSpec((1, nc, 2 * LANES), lambda b: (b, 0, 0)),
                   pl.BlockSpec((1, 2 * NSA_DH, nc), lambda b: (b, 0, 0))],
        out_shape=[jax.ShapeDtypeStruct((B, nc, 2 * LANES), BF16),
                   jax.ShapeDtypeStruct((B, 2 * NSA_DH, nc), BF16)],
        compiler_params=_cparams(("parallel",)),
        name="nsa_compress",
    )(kx, vx, pa, pb, wa, wb, w2k, w2vt, feat)


def _nsa_attn_kernel(qnt_ref, kc_ref, vct_ref, ks_ref, kw_ref, vst_ref, vwt_ref, gt_ref, ovt_ref, oh_ref,
                     o_ref, ocmp_ref, m_ref, acc_ref, mw_ref, accw_ref, s_ref, mx_ref, qs_ref):
    i = pl.program_id(1)
    tq = TQ_NSA
    G, R, dh = NSA_KV_HEADS, NSA_REP, NSA_DH
    nc = kc_ref.shape[1]
    nblk = ovt_ref.shape[0]
    t0 = i * tq
    tpos = t0 + lax.broadcasted_iota(jnp.int32, (1, tq), 1)
    q_of = lambda h: qnt_ref[0, 0, h * LANES:(h + 1) * LANES, :]

    for ref in (m_ref, mw_ref):
        ref[...] = jnp.full(ref.shape, -jnp.inf, F32)
    for ref in (acc_ref, accw_ref):
        ref[...] = jnp.zeros(ref.shape, F32)

    every = range(G * R)
    rel = lax.broadcasted_iota(jnp.int32, (TKV, tq), 0) - lax.broadcasted_iota(jnp.int32, (TKV, tq), 1)
    causal = rel <= 0

    def score(kt, b, heads, sel, keep=None):
        rows = pl.ds(pl.multiple_of(kt * TKV, TKV), TKV)
        for h in heads:
            g = h // R
            if sel:
                k = jnp.concatenate([ks_ref[0, rows, g * LANES:(g + 1) * LANES], oh_ref[rows, :]], axis=1)
            else:
                k = kw_ref[0, rows, g * LANES:(g + 1) * LANES]
            s = _dot(k, qs_ref[h] if sel else q_of(h))
            if keep is not None:
                s = jnp.where(keep, s, NEG_INF)
            _stage_scores(s, s_ref, mx_ref, b, h)

    def update(kt, b, heads, sel):
        vt_ref, state = (vst_ref, (m_ref, acc_ref)) if sel else (vwt_ref, (mw_ref, accw_ref))
        for h in heads:
            g = h // R
            _online_update(vt_ref[0, kt, g * dh:(g + 1) * dh, :], s_ref, mx_ref, b, *state, h)

    assert W_WIN == 2 * TKV
    kt1, kt2 = jnp.maximum(i - 1, 0), jnp.maximum(i - 2, 0)
    keep1 = jnp.broadcast_to(i >= 1, (TKV, tq))
    keep2 = (rel > 0) & (i >= 2)

    nrow = lax.broadcasted_iota(jnp.int32, (nc, 1), 0)
    bias_c = jnp.where((tpos >= nrow * D_CMP + (L_CMP - 1)) & (nrow < nc - 1), 0.0, NEG_INF)
    has_cmp = tpos >= L_CMP - 1
    for h in every:
        g = h // R
        _stage_scores(_dot(kc_ref[0, :, g * LANES:(g + 1) * LANES], q_of(h)) + bias_c, s_ref, mx_ref, 0, h)
    score(i, 1, every, False, causal)
    for h in every:
        score(kt1, 2, [h], False, keep1)
        update(i, 1, [h], False)

    imps = [None] * G
    for h in every:
        g = h // R
        e = jnp.exp2(s_ref[0, h, 0:nc, :] - mx_ref[0, h])
        inv = jnp.where(has_cmp, 1.0 / jnp.sum(e, axis=0, keepdims=True), 0.0)
        both = _dot(jnp.concatenate([vct_ref[0, g * dh:(g + 1) * dh, :], ovt_ref[...]], axis=0), e.astype(BF16))
        ocmp_ref[h] = both[0:dh, :] * inv
        contrib = both[dh:dh + nblk, :] * inv
        imps[g] = contrib if imps[g] is None else imps[g] + contrib
        score(kt2, 3, [h], False, keep2)
        update(kt1, 2, [h], False)
    update(kt2, 3, every, False)

    cur = lax.shift_right_logical(tpos, 6)
    jrow = lax.broadcasted_iota(jnp.int32, (nblk, 1), 0)
    sub = lax.broadcasted_iota(jnp.int32, (8, 1), 0)
    n_slab = nblk // 8

    for g in range(G):
        imp = jnp.where(jrow > cur, -jnp.inf, imps[g])
        imp = jnp.where((jrow == 0) | (jrow == cur) | (jrow == cur - 1), jnp.inf, imp)
        slabs = [imp[8 * v:8 * v + 8, :] for v in range(n_slab)]
        ranks = [jnp.zeros((8, tq), F32) for _ in range(n_slab)]
        for jj in range(nblk):
            row = imp[jj:jj + 1, :]
            for v in range(n_slab):
                if v > jj // 8:
                    beats = row >= slabs[v]
                elif v < jj // 8:
                    beats = row > slabs[v]
                else:
                    beats = (row > slabs[v]) | ((row == slabs[v]) & (sub > jj % 8))
                ranks[v] = ranks[v] + jnp.where(beats, 1.0, 0.0)
        mb = jnp.where(jnp.concatenate(ranks, axis=0) < float(min(N_SEL, nblk)), 0.0, NEG_INF)
        mb = jnp.concatenate([mb, jnp.zeros((LANES - nblk, tq), F32)], axis=0).astype(BF16)
        for r in range(R):
            qs_ref[g * R + r] = jnp.concatenate([q_of(g * R + r), mb], axis=0)
        score(0, 0, range(g * R, (g + 1) * R), True)

    _pipelined_sweep(i, G * R, functools.partial(score, sel=True), functools.partial(update, sel=True),
                     lambda b, heads: score(i, b, heads, True, causal), tile0_staged=True)

    sig = 1.0 / (1.0 + jnp.exp(-gt_ref[0].T[0:G * 16, :]))
    outs = []
    for h in range(G * R):
        row = (h // R) * 16 + (h % R) * 4
        outs.append(sig[row:row + 1, :] * ocmp_ref[h] + sig[row + 1:row + 2, :] * _normalised(acc_ref, h, dh)
                    + sig[row + 2:row + 3, :] * _normalised(accw_ref, h, dh))
    o_ref[0] = jnp.concatenate(outs, axis=0).T.astype(BF16)


def _nsa_attn(qnt, kc, vct, ksw, vt, gt, ovt, oh, B, S):
    nkt = S // TKV
    nc = kc.shape[1]
    G = NSA_KV_HEADS
    ksw3 = ksw.reshape(B, S, W_KS)
    gt3 = gt.reshape(B, S, LANES)
    H, dh, tq = NSA_HEADS, NSA_DH, TQ_NSA
    state = [pltpu.VMEM((H, 1, tq), F32), pltpu.VMEM((H, dh + ONES_ROWS, tq), F32)]
    return pl.pallas_call(
        _nsa_attn_kernel,
        grid=(B, S // tq),
        in_specs=[pl.BlockSpec((1, 1, W_QN, tq), lambda b, i: (b, i, 0, 0)),
                  pl.BlockSpec((1, nc, G * LANES), lambda b, i: (b, 0, 0)),
                  pl.BlockSpec((1, G * dh, nc), lambda b, i: (b, 0, 0)),
                  pl.BlockSpec((1, S, G * LANES), lambda b, i: (b, 0, 0)),
                  pl.BlockSpec((1, S, G * LANES), lambda b, i: (b, 0, 1)),
                  pl.BlockSpec((1, nkt, G * dh, TKV), lambda b, i: (b, 0, 0, 0)),
                  pl.BlockSpec((1, nkt, G * dh, TKV), lambda b, i: (b, 0, 1, 0)),
                  pl.BlockSpec((1, tq, LANES), lambda b, i: (b, i, 0)),
                  _const_spec(ovt), _const_spec(oh)],
        out_specs=pl.BlockSpec((1, tq, H * dh), lambda b, i: (b, i, 0)),
        out_shape=jax.ShapeDtypeStruct((B, S, H * dh), BF16),
        scratch_shapes=([pltpu.VMEM((H, dh, tq), F32)] + state + state
                        + [pltpu.VMEM((4, H, TKV, tq), F32), pltpu.VMEM((4, H, 1, tq), F32),
                           pltpu.VMEM((H, 2 * LANES, tq), BF16)]),
        compiler_params=_cparams(("parallel", "arbitrary")),
        name="nsa_attn",
    )(qnt, kc, vct, ksw3, ksw3, vt, vt, gt3, ovt, oh)


FF_CHUNK = 256
HALO = 16
SUB = 8
QV = TM // SUB


def _ffn_kernel(x_ref, xp_ref, a_ref, ap_ref, b_ref, bp_ref, wo_ref, g_ref, wup_ref, cw_ref, cb_ref, wdn_ref, fg_ref,
                o_ref, n_ref, p_ref, act_ref, *, final, spt):
    assert CONV_W == 3
    i = pl.program_id(0)
    ka = a_ref.shape[1]
    n_slab = D_MODEL // LANES

    def attn_residual(h_rows, a_rows, b_rows):
        return h_rows[...] + _dot(a_rows[...], wo_ref[0, 0:ka, :]) + _dot(b_rows[...], wo_ref[0, ka:, :])

    o_ref[...] = attn_residual(x_ref, a_ref, b_ref)
    keep = jnp.where(i % spt == 0, 0.0, 1.0)
    n_ref[0:HALO, :] = (_rms(attn_residual(xp_ref, ap_ref, bp_ref), g_ref[0]) * keep).astype(BF16)
    nt = _rms(o_ref[...], g_ref[0])
    for c in range(n_slab):
        for s in range(SUB):
            p_ref[c, pl.ds(s, QV, stride=SUB), :] = nt[s * QV:(s + 1) * QV, c * LANES:(c + 1) * LANES]
    for c in range(n_slab):
        n_ref[HALO:HALO + TM, c * LANES:(c + 1) * LANES] = p_ref[c].astype(BF16)

    first_sublane = lax.broadcasted_iota(jnp.int32, (SUB, 1), 0) == 0

    def conv(col):
        hx = _dot(n_ref[...], wup_ref[0, :, col:col + FF_CHUNK])
        main = hx[HALO:HALO + TM, :]
        wrap1 = jnp.where(first_sublane, hx[HALO - 1:HALO, :], pltpu.roll(main[TM - SUB:TM, :], 1, 0))
        wrap2 = jnp.where(first_sublane, hx[HALO - 2:HALO - 1, :], pltpu.roll(main[TM - 2 * SUB:TM - SUB, :], 1, 0))
        tap1 = jnp.concatenate([wrap1, main[0:TM - SUB, :]], axis=0)
        tap2 = jnp.concatenate([wrap2, wrap1, main[0:TM - 2 * SUB, :]], axis=0)
        cw = cw_ref[0, :, col:col + FF_CHUNK]
        return cb_ref[0, :, col:col + FF_CHUNK] + tap2 * cw[0:1, :] + tap1 * cw[1:2, :] + main * cw[2:3, :]

    for c in range(D_FF // FF_CHUNK):
        col = c * FF_CHUNK
        gate = conv(col)
        up = conv(D_FF + col)
        act_ref[:, col:col + FF_CHUNK] = (gate * (1.0 / (1.0 + jnp.exp(-gate))) * up).astype(BF16)
    down = _dot(act_ref[...], wdn_ref[0])
    for c in range(n_slab):
        p_ref[c] = down[:, c * LANES:(c + 1) * LANES]
    for c in range(n_slab):
        for s in range(SUB):
            rows, lanes = slice(s * QV, (s + 1) * QV), slice(c * LANES, (c + 1) * LANES)
            o_ref[rows, lanes] = o_ref[rows, lanes] + p_ref[c, pl.ds(s, QV, stride=SUB), :]
    if final:
        o_ref[...] = _rms(o_ref[...], fg_ref[...])


def _outproj_ffn(h, oa, ob, wo, g, wup, cw, cb, wdn, fg, l, S, final):
    M = h.shape[0]
    spt = S // TM
    row = lambda width: pl.BlockSpec((TM, width), lambda i: (i, 0))
    prev = lambda width: pl.BlockSpec((HALO, width), lambda i: (jnp.maximum(i * (TM // HALO) - 1, 0), 0))
    return pl.pallas_call(
        functools.partial(_ffn_kernel, final=final, spt=spt),
        grid=(M // TM,),
        in_specs=[row(D_MODEL), prev(D_MODEL), row(oa.shape[1]), prev(oa.shape[1]), row(ob.shape[1]),
                  prev(ob.shape[1]), _layer_spec(wo, l), _layer_spec(g, l), _layer_spec(wup, l), _layer_spec(cw, l),
                  _layer_spec(cb, l), _layer_spec(wdn, l), _const_spec(fg)],
        out_specs=row(D_MODEL),
        out_shape=jax.ShapeDtypeStruct((M, D_MODEL), F32),
        scratch_shapes=[pltpu.VMEM((HALO + TM, D_MODEL), BF16), pltpu.VMEM((D_MODEL // LANES, TM, LANES), F32),
                        pltpu.VMEM((TM, D_FF), BF16)],
        compiler_params=_cparams(("parallel",)),
        name="outproj_ffn",
    )(h, h, oa, oa, ob, ob, wo, g, wup, cw, cb, wdn, fg)


def _slabs(w, n, width, pad_to=LANES):
    lead = w.shape[:-1]
    w = w.reshape(lead + (n, width))
    w = jnp.pad(w, [(0, 0)] * (w.ndim - 1) + [(0, pad_to - width)])
    return w.reshape(lead + (n * pad_to,))


def _inproj_weights(w_in):
    G, dh = NSA_KV_HEADS, NSA_DH
    cols = lambda off, n: w_in[..., off:off + n]
    zeros = lambda n: jnp.zeros(w_in.shape[:-1] + (n,), w_in.dtype)
    kr = jnp.concatenate([zeros(MLA_NOPE), cols(OFF_KROPE, MLA_ROPE), zeros(LANES - MLA_QK)], axis=-1)
    gates = _slabs(cols(OFF_GT, 3 * NSA_HEADS), NSA_HEADS, 3, 4)
    gates = jnp.concatenate([gates, zeros(LANES - 4 * NSA_HEADS)], axis=-1)
    w = jnp.concatenate([cols(OFF_CQ, MLA_Q_RANK), cols(OFF_CKV, MLA_KV_RANK), kr,
                         gates, cols(OFF_KC, G * dh), cols(OFF_VC, G * dh),
                         _slabs(cols(OFF_KS, G * dh), G, dh), _slabs(cols(OFF_KW, G * dh), G, dh)], axis=-1)
    assert w.shape[-1] == W_ROW
    qn = cols(OFF_QN, NSA_HEADS * dh) * (NSA_DH ** -0.5 * LOG2E)
    wvt = jnp.concatenate([cols(OFF_VS, G * dh), cols(OFF_VW, G * dh)], axis=-1)
    return w.astype(BF16), jnp.swapaxes(qn, 1, 2).astype(BF16), jnp.swapaxes(wvt, 1, 2).astype(BF16)


def _bf16_terms(x, n):
    terms = []
    for _ in range(n):
        bits = np.array(x, np.float32).view(np.uint32)
        bits = (bits + np.uint32(0x7FFF) + ((bits >> np.uint32(16)) & np.uint32(1))) & np.uint32(0xFFFF0000)
        terms.append(float(bits.view(np.float32)))
        x = x - terms[-1]
    return terms


N_POS_TERMS = 3


def _pos_features(pos, n_slabs):
    out = np.zeros((len(pos), n_slabs * LANES), np.float32)
    for s in range(n_slabs):
        base = s * LANES + NSA_DH
        out[:, base:base + N_POS_TERMS] = (pos // L_SEL)[:, None]
        out[:, base + N_POS_TERMS:base + 2 * N_POS_TERMS] = (pos % L_SEL)[:, None]
    return out


def _const_tables(S):
    slopes = 2.0 ** (-8.0 * np.arange(1, NSA_HEADS + 1) / NSA_HEADS)
    terms = np.array(_bf16_terms(LOG2E, N_POS_TERMS))
    pad = LANES - NSA_DH
    qaug = np.zeros((NSA_HEADS * pad, 1), np.float32)
    for h in range(NSA_HEADS):
        base = h * pad
        qaug[base:base + N_POS_TERMS, 0] = slopes[h] * L_SEL * terms
        qaug[base + N_POS_TERMS:base + 2 * N_POS_TERMS, 0] = slopes[h] * terms
    posf = _pos_features(np.arange(S), 4)
    nc = S // D_CMP
    feat = _pos_features(np.arange(nc) * D_CMP + L_CMP - 1, 2)
    n_cmp = (S - L_CMP) // D_CMP + 1
    nblk = S // L_SEL
    cs = np.arange(n_cmp) * D_CMP
    ss = np.arange(nblk) * L_SEL
    ov = ((cs[:, None] < ss[None, :] + L_SEL) & (cs[:, None] + L_CMP > ss[None, :])).astype(np.float32)
    ovt = np.zeros((nblk, nc), np.float32)
    ovt[:, :n_cmp] = ov.T
    assert nblk <= LANES
    onehot = (np.arange(S)[:, None] // L_SEL == np.arange(LANES)[None, :]).astype(np.float32)
    return (jnp.asarray(qaug), jnp.asarray(posf), jnp.asarray(feat), jnp.asarray(ovt, BF16),
            jnp.asarray(onehot, BF16))


def _rope_consts(S):
    half = MLA_ROPE // 2
    inv = (1.0 / (np.float32(ROPE_THETA) ** (np.arange(0, MLA_ROPE, 2, dtype=np.float32) / MLA_ROPE))).astype(np.float32)
    ang = np.arange(S, dtype=np.float32)[:, None] * inv[None, :]
    cos, sin = np.cos(ang), np.sin(ang)
    z = lambda w: np.zeros((S, w), np.float32)
    ct = np.concatenate([np.ones((S, MLA_NOPE), np.float32), cos, cos, z(LANES - MLA_QK)], axis=1)
    sa = np.concatenate([z(MLA_NOPE), -sin, z(LANES - MLA_NOPE - half)], axis=1)
    sb = np.concatenate([z(MLA_NOPE + half), sin, z(LANES - MLA_QK)], axis=1)
    return tuple(jnp.asarray(np.ascontiguousarray(t)) for t in (ct, sa, sb, cos.T, sin.T))


def _mla_weights(w_uq, w_ukv):
    wq = _slabs(w_uq, MLA_HEADS, MLA_QK) * (MLA_QK ** -0.5 * LOG2E)
    kv = w_ukv.reshape(w_ukv.shape[:-1] + (MLA_HEADS, MLA_NOPE + MLA_V))
    wk = _slabs(kv[..., :MLA_NOPE].reshape(w_ukv.shape[:-1] + (MLA_HEADS * MLA_NOPE,)), MLA_HEADS, MLA_NOPE)
    wvt = jnp.swapaxes(kv[..., MLA_NOPE:].reshape(w_ukv.shape[:-1] + (MLA_HEADS * MLA_V,)), 1, 2)
    return jnp.swapaxes(wq, 1, 2).astype(BF16), wk.astype(BF16), wvt.astype(BF16)


def _compress_weights(pos, w1, w2k, w2v):
    G, dh, half = NSA_KV_HEADS, NSA_DH, L_CMP // 2
    L = pos.shape[0]
    eye = jnp.eye(G, dtype=F32)

    def first_layer(w):
        return jnp.einsum('nclde,gk->nclgdke', w, eye).reshape(L, 2, half * G * dh, G * CMP_HID).astype(BF16)

    def pos_row(p):
        return jnp.broadcast_to(p[:, :, :, None, :], (L, 2, half, G, dh)).reshape(L, 2, 1, half * G * dh)

    w2kp = jnp.pad(w2k, ((0, 0), (0, 0), (0, LANES - dh)))
    w2k_bd = jnp.einsum('ned,gk->ngekd', w2kp, eye).reshape(L, G * CMP_HID, G * LANES).astype(BF16)
    w2vt_bd = jnp.einsum('ned,gk->ngdke', w2v, eye).reshape(L, G * dh, G * CMP_HID).astype(BF16)
    return (pos_row(pos[:, :, :half]), pos_row(pos[:, :, half:]), first_layer(w1[:, :, :half]),
            first_layer(w1[:, :, half:]), w2k_bd, w2vt_bd)


def kernel(x, attn_norm, w_in, q_norm, kv_norm, w_uq, w_ukv, cmp_pos_k, cmp_pos_v, cmp_k_w1, cmp_k_w2, cmp_v_w1,
           cmp_v_w2, w_o, ffn_norm, w_up, conv_w, conv_b, w_down, final_norm):
    B, S, D = x.shape
    M = B * S
    assert D == D_MODEL and S % TM == 0 and TQ_MLA == TKV and TQ_NSA == TKV and S % TKV == 0
    qaug, posf, feat, ovt, onehot = _const_tables(S)
    rope = _rope_consts(S)

    w_row, w_qt, w_vt = _inproj_weights(w_in)
    mqt, mk, mvt = _mla_weights(w_uq, w_ukv)
    pa, pb, cwa, cwb, w2k, w2vt = _compress_weights(jnp.stack([cmp_pos_k, cmp_pos_v], axis=1),
                                                    jnp.stack([cmp_k_w1, cmp_v_w1], axis=1), cmp_k_w2, cmp_v_w2)
    wo, wup, wdn = w_o.astype(BF16), w_up.astype(BF16), w_down.astype(BF16)
    g_attn, g_q, g_kv, g_ffn = (a[:, None, :] for a in (attn_norm, q_norm, kv_norm, ffn_norm))
    cb = conv_b[:, None, :]
    fg = final_norm[None, :]

    h = x.reshape(M, D)
    for l in range(DEPTH):
        gt, kcr, vcr, qnt, ksw, vt, qt, k, vmt = _inproj(h, g_attn, w_row, w_qt, w_vt, qaug, posf,
                                                          (g_q, g_kv, mqt, mk, mvt), rope, l, B, S)
        o_mla = _mla_attn(qt, k.reshape(B, S, -1), vmt, B, S)
        kc, vct = _compress(kcr.reshape(B, S, -1), vcr.reshape(B, S, -1), pa, pb, cwa, cwb, w2k, w2vt, feat, l)
        o_nsa = _nsa_attn(qnt, kc, vct, ksw, vt, gt, ovt, onehot, B, S)
        h = _outproj_ffn(h, o_mla.reshape(M, -1), o_nsa.reshape(M, -1), wo, g_ffn, wup, conv_w, cb, wdn, fg, l, S,
                         final=(l == DEPTH - 1))
    return h.reshape(B, S, D)
```

```python
import functools

import numpy as np
import jax
import jax.numpy as jnp
from jax import lax
from jax.experimental import pallas as pl
from jax.experimental.pallas import tpu as pltpu

F32 = jnp.float32
BF16 = jnp.bfloat16

D_MODEL = 1024
DEPTH = 2
EPS = 1e-6
NEG_INF = -1e30
LOG2E = float(np.log2(np.e))
LANES = 128

MLA_HEADS = 8
MLA_NOPE = 64
MLA_ROPE = 32
MLA_V = 64
MLA_Q_RANK = 384
MLA_KV_RANK = 256
MLA_QK = MLA_NOPE + MLA_ROPE
ROPE_THETA = 10000.0

NSA_HEADS = 8
NSA_KV_HEADS = 2
NSA_REP = NSA_HEADS // NSA_KV_HEADS
NSA_DH = 64
L_CMP = 32
D_CMP = 16
CMP_HID = 128
L_SEL = 64
N_SEL = 16
W_WIN = 512

D_FF = 2816
CONV_W = 3

_SIZES = [MLA_Q_RANK, MLA_KV_RANK, MLA_ROPE, NSA_HEADS * NSA_DH] + [NSA_KV_HEADS * NSA_DH] * 6 + [3 * NSA_HEADS]
_OFFS = [0] + [int(o) for o in np.cumsum(_SIZES)]
(OFF_CQ, OFF_CKV, OFF_KROPE, OFF_QN, OFF_KC, OFF_VC, OFF_KS, OFF_VS, OFF_KW, OFF_VW, OFF_GT, D_IN) = _OFFS

TM = 512
TKV = 256
TQ_MLA = 256
TQ_NSA = 256
VMEM_LIMIT = 56 * 1024 * 1024

W_ZA = 6 * LANES
W_ZB = 3 * LANES
W_KS = 4 * LANES
W_ROW = W_ZA + W_ZB + W_KS
W_QN = NSA_HEADS * LANES


def _cparams(sem):
    return pltpu.CompilerParams(dimension_semantics=sem, vmem_limit_bytes=VMEM_LIMIT)


def _layer_spec(a, l):
    return pl.BlockSpec((1,) + a.shape[1:], lambda *_: (l,) + (0,) * (a.ndim - 1))


def _const_spec(a):
    return pl.BlockSpec(a.shape, lambda *_: (0,) * a.ndim)


def _dot(a, b):
    return jnp.dot(a, b, preferred_element_type=F32)


def _dot_nt(a, b):
    return lax.dot_general(a, b, (((1,), (1,)), ((), ())), preferred_element_type=F32)


def _rms(x, g):
    return x * lax.rsqrt(jnp.mean(x * x, axis=-1, keepdims=True) + EPS) * g


def _store_tiles(ref, xt):
    for c in range(TM // TKV):
        ref[0, c] = xt[:, c * TKV:(c + 1) * TKV]


def _inproj_kernel(x_ref, g_ref, w_ref, wqt_ref, wvt_ref, qaug_ref, posf_ref,
                   qg_ref, kg_ref, mwqt_ref, mwk_ref, mwvt_ref, ct_ref, sa_ref, sb_ref, cost_ref, sint_ref,
                   gt_ref, kcr_ref, vcr_ref, qnt_ref, ksw_ref, vt_ref, mqt_ref, mk_ref, mvt_ref):
    n = _rms(x_ref[...], g_ref[0]).astype(BF16)
    c0, c1 = W_ZA, W_ZA + W_ZB
    cq, ckv, kr = _mla_norms(_dot(n, w_ref[0, :, 0:c0]), qg_ref, kg_ref, ct_ref, sa_ref, sb_ref)
    zb = _dot(n, w_ref[0, :, c0:c1])
    gt_ref[...] = zb[:, 0:LANES]
    kcr_ref[...] = zb[:, LANES:2 * LANES]
    vcr_ref[...] = zb[:, 2 * LANES:3 * LANES]
    ksw_ref[...] = (_dot(n, w_ref[0, :, c1:W_ROW]) + posf_ref[...]).astype(BF16)
    qt = _dot_nt(wqt_ref[0], n)
    slabs, pad = [], LANES - NSA_DH
    for h in range(NSA_HEADS):
        slabs += [qt[h * NSA_DH:(h + 1) * NSA_DH, :], jnp.broadcast_to(qaug_ref[h * pad:(h + 1) * pad, :], (pad, TM))]
    _store_tiles(qnt_ref, jnp.concatenate(slabs, axis=0).astype(BF16))
    _store_tiles(vt_ref, _dot_nt(wvt_ref[0], n).astype(BF16))
    _mla_project(cq, ckv, kr, mwqt_ref, mwk_ref, mwvt_ref, cost_ref, sint_ref, mqt_ref, mk_ref, mvt_ref)


def _inproj(h, g, w, wqt, wvt, qaug, posf, mla, rope, l, B, S):
    M = B * S
    spt = S // TM
    ct, sa, sb, cost, sint = rope
    row = lambda width: pl.BlockSpec((TM, width), lambda i: (i, 0))
    tiles = lambda rows: pl.BlockSpec((1, TM // TKV, rows, TKV), lambda i: (i // spt, i % spt, 0, 0))
    tab = pl.BlockSpec((TM, LANES), lambda i: (i % spt, 0))
    tabt = pl.BlockSpec((MLA_ROPE // 2, TM), lambda i: (0, i % spt))
    return pl.pallas_call(
        _inproj_kernel,
        grid=(M // TM,),
        in_specs=[row(D_MODEL), _layer_spec(g, l), _layer_spec(w, l), _layer_spec(wqt, l), _layer_spec(wvt, l),
                  _const_spec(qaug), pl.BlockSpec((TM, W_KS), lambda i: (i % spt, 0))]
                 + [_layer_spec(a, l) for a in mla] + [tab, tab, tab, tabt, tabt],
        out_specs=[row(LANES), row(LANES), row(LANES), tiles(W_QN), row(W_KS), tiles(4 * NSA_DH),
                   tiles(MLA_HEADS * LANES), row(MLA_HEADS * LANES), tiles(MLA_HEADS * MLA_V)],
        out_shape=[jax.ShapeDtypeStruct((M, LANES), F32),
                   jax.ShapeDtypeStruct((M, LANES), F32),
                   jax.ShapeDtypeStruct((M, LANES), F32),
                   jax.ShapeDtypeStruct((B, S // TKV, W_QN, TKV), BF16),
                   jax.ShapeDtypeStruct((M, W_KS), BF16),
                   jax.ShapeDtypeStruct((B, S // TKV, 4 * NSA_DH, TKV), BF16),
                   jax.ShapeDtypeStruct((B, S // TKV, MLA_HEADS * LANES, TKV), BF16),
                   jax.ShapeDtypeStruct((M, MLA_HEADS * LANES), BF16),
                   jax.ShapeDtypeStruct((B, S // TKV, MLA_HEADS * MLA_V, TKV), BF16)],
        compiler_params=_cparams(("parallel",)),
        name="attn_inproj",
    )(h, g, w, wqt, wvt, qaug, posf, *mla, ct, sa, sb, cost, sint)


def _rope_slab(x, ct, sa, sb):
    return x * ct + pltpu.roll(x, LANES - MLA_ROPE // 2, 1) * sa + pltpu.roll(x, MLA_ROPE // 2, 1) * sb


def _mla_norms(za, qg_ref, kg_ref, ct_ref, sa_ref, sb_ref):
    cq = _rms(za[:, 0:MLA_Q_RANK], qg_ref[0]).astype(BF16)
    ckv = _rms(za[:, MLA_Q_RANK:MLA_Q_RANK + MLA_KV_RANK], kg_ref[0]).astype(BF16)
    kr = _rope_slab(za[:, 5 * LANES:6 * LANES], ct_ref[...], sa_ref[...], sb_ref[...])
    return cq, ckv, kr


def _mla_project(cq, ckv, kr, wqt_ref, wk_ref, wvt_ref, cost_ref, sint_ref, qt_ref, k_ref, vt_ref):
    k = _dot(ckv, wk_ref[0])
    for h in range(MLA_HEADS):
        sl = slice(h * LANES, (h + 1) * LANES)
        k_ref[:, sl] = (k[:, sl] + kr).astype(BF16)
    qt = _dot_nt(wqt_ref[0], cq)
    cos, sin = cost_ref[...], sint_ref[...]
    half = MLA_ROPE // 2
    slabs = []
    for h in range(MLA_HEADS):
        base = h * LANES
        x1 = qt[base + MLA_NOPE:base + MLA_NOPE + half, :]
        x2 = qt[base + MLA_NOPE + half:base + MLA_QK, :]
        slabs += [qt[base:base + MLA_NOPE, :], x1 * cos - x2 * sin, x2 * cos + x1 * sin,
                  qt[base + MLA_QK:base + LANES, :]]
    _store_tiles(qt_ref, jnp.concatenate(slabs, axis=0).astype(BF16))
    _store_tiles(vt_ref, _dot_nt(wvt_ref[0], ckv).astype(BF16))


ONES_ROWS = 16


def _stage_scores(s, s_ref, mx_ref, b, h):
    mx_ref[b, h] = jnp.max(s, axis=0, keepdims=True)
    s_ref[b, h, 0:s.shape[0], :] = s


def _online_update(vt, s_ref, mx_ref, b, m_ref, acc_ref, h, valid=None):
    m_old = m_ref[h]
    mx = mx_ref[b, h] if valid is None else jnp.where(valid, mx_ref[b, h], -jnp.inf)
    m_new = jnp.maximum(m_old, mx)
    p = jnp.exp2(s_ref[b, h] - (m_new if valid is None else jnp.where(valid, m_new, jnp.inf))).astype(BF16)
    vt1 = jnp.concatenate([vt, jnp.ones((ONES_ROWS, vt.shape[1]), BF16)], axis=0)
    acc_ref[h] = jnp.exp2(m_old - m_new) * acc_ref[h] + _dot(vt1, p)
    m_ref[h] = m_new


def _normalised(acc_ref, h, dh):
    return acc_ref[h, 0:dh, :] / acc_ref[h, dh:dh + 1, :]


def _pipelined_sweep(i, n_heads, score, update, score_diag, tile0_staged=False):
    every = range(n_heads)

    if not tile0_staged:
        @pl.when(i >= 1)
        def _():
            score(0, 0, every)

    def pair(j):
        for h in every:
            score(2 * j + 1, 1, [h])
            update(2 * j, 0, [h])
        for h in every:
            score(2 * j + 2, 0, [h])
            update(2 * j + 1, 1, [h])

    n_pairs = i // 2

    @pl.loop(0, n_pairs // 2)
    def _(jj):
        pair(2 * jj)
        pair(2 * jj + 1)

    def tail(odd_pair, odd_tile):
        if odd_pair:
            pair(n_pairs - 1)
        if odd_tile:
            for h in every:
                score_diag(1, [h])
                update(i - 1, 0, [h])
            update(i, 1, every)
        else:
            score_diag(0, every)
            update(i, 0, every)

    for odd_pair in (False, True):
        for odd_tile in (False, True):
            pl.when((n_pairs % 2 == int(odd_pair)) & (i % 2 == int(odd_tile)))(
                functools.partial(tail, odd_pair, odd_tile))


def _mla_attn_kernel(qt_ref, k_ref, vt_ref, o_ref, m_ref, acc_ref, s_ref, mx_ref):
    i = pl.program_id(1)
    tq = TQ_MLA
    m_ref[...] = jnp.full(m_ref.shape, -jnp.inf, F32)
    acc_ref[...] = jnp.zeros(acc_ref.shape, F32)

    every = range(MLA_HEADS)

    def score(kt, b, heads, keep=None):
        for h in heads:
            sl = slice(h * LANES, (h + 1) * LANES)
            k = k_ref[0, pl.ds(pl.multiple_of(kt * TKV, TKV), TKV), sl]
            s = _dot(k, qt_ref[0, 0, sl, :])
            if keep is not None:
                s = jnp.where(keep, s, NEG_INF)
            _stage_scores(s, s_ref, mx_ref, b, h)

    def update(kt, b, heads):
        for h in heads:
            _online_update(vt_ref[0, kt, h * MLA_V:(h + 1) * MLA_V, :], s_ref, mx_ref, b, m_ref, acc_ref, h)

    causal = lax.broadcasted_iota(jnp.int32, (TKV, tq), 0) <= lax.broadcasted_iota(jnp.int32, (TKV, tq), 1)
    _pipelined_sweep(i, MLA_HEADS, score, update, lambda b, heads: score(i, b, heads, causal))
    out = jnp.concatenate([_normalised(acc_ref, h, MLA_V) for h in range(MLA_HEADS)], axis=0)
    o_ref[0] = out.T.astype(BF16)


def _mla_attn(q, k, vt, B, S):
    nkt = S // TKV
    hw = MLA_HEADS * LANES
    return pl.pallas_call(
        _mla_attn_kernel,
        grid=(B, S // TQ_MLA),
        in_specs=[pl.BlockSpec((1, 1, hw, TQ_MLA), lambda b, i: (b, i, 0, 0)),
                  pl.BlockSpec((1, S, hw), lambda b, i: (b, 0, 0)),
                  pl.BlockSpec((1, nkt, MLA_HEADS * MLA_V, TKV), lambda b, i: (b, 0, 0, 0))],
        out_specs=pl.BlockSpec((1, TQ_MLA, MLA_HEADS * MLA_V), lambda b, i: (b, i, 0)),
        out_shape=jax.ShapeDtypeStruct((B, S, MLA_HEADS * MLA_V), BF16),
        scratch_shapes=[pltpu.VMEM((MLA_HEADS, 1, TQ_MLA), F32),
                        pltpu.VMEM((MLA_HEADS, MLA_V + ONES_ROWS, TQ_MLA), F32),
                        pltpu.VMEM((2, MLA_HEADS, TKV, TQ_MLA), F32), pltpu.VMEM((2, MLA_HEADS, 1, TQ_MLA), F32)],
        compiler_params=_cparams(("parallel", "arbitrary")),
        name="mla_attn",
    )(q, k, vt)


def _gelu_tanh(x):
    return 0.5 * x * (1.0 + jnp.tanh(np.sqrt(2.0 / np.pi).astype(np.float32) * (x + 0.044715 * (x * x * x))))


def _compress_kernel(kx_ref, vx_ref, pa_ref, pb_ref, wa_ref, wb_ref, w2k_ref, w2vt_ref, feat_ref, kc_ref, vct_ref):
    half = L_CMP // 2
    nc = kx_ref.shape[1] // half

    def hidden(x_ref, kv):
        x = jnp.concatenate([x_ref[0, pl.ds(l, nc, stride=half), :] for l in range(half)], axis=1)
        ya = _dot((x + pa_ref[0, kv]).astype(BF16), wa_ref[0, kv])
        yb = _dot((x + pb_ref[0, kv]).astype(BF16), wb_ref[0, kv])
        return _gelu_tanh(ya + pltpu.roll(yb, nc - 1, 0)).astype(BF16)

    kc_ref[0] = (_dot(hidden(kx_ref, 0), w2k_ref[0]) + feat_ref[...]).astype(BF16)
    vct_ref[0] = _dot_nt(w2vt_ref[0], hidden(vx_ref, 1)).astype(BF16)


def _compress(kx, vx, pa, pb, wa, wb, w2k, w2vt, feat, l):
    B, S, width = kx.shape
    nc = S // (L_CMP // 2)
    xs = pl.BlockSpec((1, S, width), lambda b: (b, 0, 0))
    return pl.pallas_call(
        _compress_kernel,
        grid=(B,),
        in_specs=[xs, xs] + [_layer_spec(a, l) for a in (pa, pb, wa, wb, w2k, w2vt)] + [_const_spec(feat)],
        out_specs=[pl.BlockSpec((1, nc, 2 * LANES), lambda b: (b, 0, 0)),
                   pl.BlockSpec((1, 2 * NSA_DH, nc), lambda b: (b, 0, 0))],
        out_shape=[jax.ShapeDtypeStruct((B, nc, 2 * LANES), BF16),
                   jax.ShapeDtypeStruct((B, 2 * NSA_DH, nc), BF16)],
        compiler_params=_cparams(("parallel",)),
        name="nsa_compress",
    )(kx, vx, pa, pb, wa, wb, w2k, w2vt, feat)


def _nsa_attn_kernel(qnt_ref, kc_ref, vct_ref, ks_ref, kw_ref, vst_ref, vwt_ref, gt_ref, ovt_ref, oh_ref,
                     o_ref, ocmp_ref, m_ref, acc_ref, mw_ref, accw_ref, s_ref, mx_ref, qs_ref):
    i = pl.program_id(1)
    tq = TQ_NSA
    G, R, dh = NSA_KV_HEADS, NSA_REP, NSA_DH
    nc = kc_ref.shape[1]
    nblk = ovt_ref.shape[0]
    t0 = i * tq
    tpos = t0 + lax.broadcasted_iota(jnp.int32, (1, tq), 1)
    q_of = lambda h: qnt_ref[0, 0, h * LANES:(h + 1) * LANES, :]

    for ref in (m_ref, mw_ref):
        ref[...] = jnp.full(ref.shape, -jnp.inf, F32)
    for ref in (acc_ref, accw_ref):
        ref[...] = jnp.zeros(ref.shape, F32)

    every = range(G * R)
    rel = lax.broadcasted_iota(jnp.int32, (TKV, tq), 0) - lax.broadcasted_iota(jnp.int32, (TKV, tq), 1)
    causal = rel <= 0

    def score(kt, b, heads, sel, keep=None):
        rows = pl.ds(pl.multiple_of(kt * TKV, TKV), TKV)
        for h in heads:
            g = h // R
            if sel:
                k = jnp.concatenate([ks_ref[0, rows, g * LANES:(g + 1) * LANES], oh_ref[rows, :]], axis=1)
            else:
                k = kw_ref[0, rows, g * LANES:(g + 1) * LANES]
            s = _dot(k, qs_ref[h] if sel else q_of(h))
            if keep is not None:
                s = jnp.where(keep, s, NEG_INF)
            _stage_scores(s, s_ref, mx_ref, b, h)

    def update(kt, b, heads, sel, valid=None):
        vt_ref, state = (vst_ref, (m_ref, acc_ref)) if sel else (vwt_ref, (mw_ref, accw_ref))
        for h in heads:
            g = h // R
            _online_update(vt_ref[0, kt, g * dh:(g + 1) * dh, :], s_ref, mx_ref, b, *state, h, valid)

    assert W_WIN == 2 * TKV
    kt1, kt2 = jnp.maximum(i - 1, 0), jnp.maximum(i - 2, 0)
    near = rel > 0

    nrow = lax.broadcasted_iota(jnp.int32, (nc, 1), 0)
    bias_c = jnp.where((tpos >= nrow * D_CMP + (L_CMP - 1)) & (nrow < nc - 1), 0.0, NEG_INF)
    has_cmp = tpos >= L_CMP - 1
    for h in every:
        g = h // R
        _stage_scores(_dot(kc_ref[0, :, g * LANES:(g + 1) * LANES], q_of(h)) + bias_c, s_ref, mx_ref, 0, h)
    score(i, 1, every, False, causal)
    for h in every:
        score(kt1, 2, [h], False)
        update(i, 1, [h], False)

    imps = [None] * G
    for h in every:
        g = h // R
        e = jnp.exp2(s_ref[0, h, 0:nc, :] - mx_ref[0, h])
        inv = jnp.where(has_cmp, 1.0 / jnp.sum(e, axis=0, keepdims=True), 0.0)
        both = _dot(jnp.concatenate([vct_ref[0, g * dh:(g + 1) * dh, :], ovt_ref[...]], axis=0), e.astype(BF16))
        ocmp_ref[h] = both[0:dh, :] * inv
        contrib = both[dh:dh + nblk, :] * inv
        imps[g] = contrib if imps[g] is None else imps[g] + contrib
        score(kt2, 3, [h], False, near)
        update(kt1, 2, [h], False, i >= 1)
    update(kt2, 3, every, False, i >= 2)

    cur = lax.shift_right_logical(tpos, 6)
    jrow = lax.broadcasted_iota(jnp.int32, (nblk, 1), 0)
    sub = lax.broadcasted_iota(jnp.int32, (8, 1), 0)
    n_slab = nblk // 8

    for g in range(G):
        imp = jnp.where(jrow > cur, -jnp.inf, imps[g])
        imp = jnp.where((jrow == 0) | (jrow == cur) | (jrow == cur - 1), jnp.inf, imp)
        slabs = [imp[8 * v:8 * v + 8, :] for v in range(n_slab)]
        ranks = [jnp.zeros((8, tq), F32) for _ in range(n_slab)]
        for jj in range(nblk):
            row = imp[jj:jj + 1, :]
            for v in range(n_slab):
                if v > jj // 8:
                    beats = row >= slabs[v]
                elif v < jj // 8:
                    beats = row > slabs[v]
                else:
                    beats = (row > slabs[v]) | ((row == slabs[v]) & (sub > jj % 8))
                ranks[v] = ranks[v] + jnp.where(beats, 1.0, 0.0)
        mb = jnp.where(jnp.concatenate(ranks, axis=0) < float(min(N_SEL, nblk)), 0.0, NEG_INF)
        mb = jnp.concatenate([mb, jnp.zeros((LANES - nblk, tq), F32)], axis=0).astype(BF16)
        for r in range(R):
            qs_ref[g * R + r] = jnp.concatenate([q_of(g * R + r), mb], axis=0)
        score(0, 0, range(g * R, (g + 1) * R), True)

    _pipelined_sweep(i, G * R, functools.partial(score, sel=True), functools.partial(update, sel=True),
                     lambda b, heads: score(i, b, heads, True, causal), tile0_staged=True)

    sig = 1.0 / (1.0 + jnp.exp(-gt_ref[0].T[0:G * 16, :]))
    outs = []
    for h in range(G * R):
        row = (h // R) * 16 + (h % R) * 4
        outs.append(sig[row:row + 1, :] * ocmp_ref[h] + sig[row + 1:row + 2, :] * _normalised(acc_ref, h, dh)
                    + sig[row + 2:row + 3, :] * _normalised(accw_ref, h, dh))
    o_ref[0] = jnp.concatenate(outs, axis=0).T.astype(BF16)


def _nsa_attn(qnt, kc, vct, ksw, vt, gt, ovt, oh, B, S):
    nkt = S // TKV
    nc = kc.shape[1]
    G = NSA_KV_HEADS
    ksw3 = ksw.reshape(B, S, W_KS)
    gt3 = gt.reshape(B, S, LANES)
    H, dh, tq = NSA_HEADS, NSA_DH, TQ_NSA
    state = [pltpu.VMEM((H, 1, tq), F32), pltpu.VMEM((H, dh + ONES_ROWS, tq), F32)]
    return pl.pallas_call(
        _nsa_attn_kernel,
        grid=(B, S // tq),
        in_specs=[pl.BlockSpec((1, 1, W_QN, tq), lambda b, i: (b, i, 0, 0)),
                  pl.BlockSpec((1, nc, G * LANES), lambda b, i: (b, 0, 0)),
                  pl.BlockSpec((1, G * dh, nc), lambda b, i: (b, 0, 0)),
                  pl.BlockSpec((1, S, G * LANES), lambda b, i: (b, 0, 0)),
                  pl.BlockSpec((1, S, G * LANES), lambda b, i: (b, 0, 1)),
                  pl.BlockSpec((1, nkt, G * dh, TKV), lambda b, i: (b, 0, 0, 0)),
                  pl.BlockSpec((1, nkt, G * dh, TKV), lambda b, i: (b, 0, 1, 0)),
                  pl.BlockSpec((1, tq, LANES), lambda b, i: (b, i, 0)),
                  _const_spec(ovt), _const_spec(oh)],
        out_specs=pl.BlockSpec((1, tq, H * dh), lambda b, i: (b, i, 0)),
        out_shape=jax.ShapeDtypeStruct((B, S, H * dh), BF16),
        scratch_shapes=([pltpu.VMEM((H, dh, tq), F32)] + state + state
                        + [pltpu.VMEM((4, H, TKV, tq), F32), pltpu.VMEM((4, H, 1, tq), F32),
                           pltpu.VMEM((H, 2 * LANES, tq), BF16)]),
        compiler_params=_cparams(("parallel", "arbitrary")),
        name="nsa_attn",
    )(qnt, kc, vct, ksw3, ksw3, vt, vt, gt3, ovt, oh)


FF_CHUNK = 256
HALO = 16
SUB = 8
QV = TM // SUB


def _ffn_kernel(x_ref, xp_ref, a_ref, ap_ref, b_ref, bp_ref, wo_ref, g_ref, wup_ref, cw_ref, cb_ref, wdn_ref, fg_ref,
                o_ref, n_ref, p_ref, act_ref, *, final, spt):
    assert CONV_W == 3
    i = pl.program_id(0)
    ka = a_ref.shape[1]
    n_slab = D_MODEL // LANES

    def attn_residual(h_rows, a_rows, b_rows):
        return h_rows[...] + _dot(a_rows[...], wo_ref[0, 0:ka, :]) + _dot(b_rows[...], wo_ref[0, ka:, :])

    o_ref[...] = attn_residual(x_ref, a_ref, b_ref)
    keep = jnp.where(i % spt == 0, 0.0, 1.0)
    n_ref[0:HALO, :] = (_rms(attn_residual(xp_ref, ap_ref, bp_ref), g_ref[0]) * keep).astype(BF16)
    nt = _rms(o_ref[...], g_ref[0])
    for c in range(n_slab):
        for s in range(SUB):
            p_ref[c, pl.ds(s, QV, stride=SUB), :] = nt[s * QV:(s + 1) * QV, c * LANES:(c + 1) * LANES]
    for c in range(n_slab):
        n_ref[HALO:HALO + TM, c * LANES:(c + 1) * LANES] = p_ref[c].astype(BF16)

    first_sublane = lax.broadcasted_iota(jnp.int32, (SUB, 1), 0) == 0

    def conv(col):
        hx = _dot(n_ref[...], wup_ref[0, :, col:col + FF_CHUNK])
        main = hx[HALO:HALO + TM, :]
        wrap1 = jnp.where(first_sublane, hx[HALO - 1:HALO, :], pltpu.roll(main[TM - SUB:TM, :], 1, 0))
        wrap2 = jnp.where(first_sublane, hx[HALO - 2:HALO - 1, :], pltpu.roll(main[TM - 2 * SUB:TM - SUB, :], 1, 0))
        tap1 = jnp.concatenate([wrap1, main[0:TM - SUB, :]], axis=0)
        tap2 = jnp.concatenate([wrap2, wrap1, main[0:TM - 2 * SUB, :]], axis=0)
        cw = cw_ref[0, :, col:col + FF_CHUNK]
        return cb_ref[0, :, col:col + FF_CHUNK] + tap2 * cw[0:1, :] + tap1 * cw[1:2, :] + main * cw[2:3, :]

    for c in range(D_FF // FF_CHUNK):
        col = c * FF_CHUNK
        gate = conv(col)
        up = conv(D_FF + col)
        act_ref[:, col:col + FF_CHUNK] = (gate * (1.0 / (1.0 + jnp.exp(-gate))) * up).astype(BF16)
    down = _dot(act_ref[...], wdn_ref[0])
    for c in range(n_slab):
        p_ref[c] = down[:, c * LANES:(c + 1) * LANES]
    for c in range(n_slab):
        for s in range(SUB):
            rows, lanes = slice(s * QV, (s + 1) * QV), slice(c * LANES, (c + 1) * LANES)
            o_ref[rows, lanes] = o_ref[rows, lanes] + p_ref[c, pl.ds(s, QV, stride=SUB), :]
    if final:
        o_ref[...] = _rms(o_ref[...], fg_ref[...])


def _outproj_ffn(h, oa, ob, wo, g, wup, cw, cb, wdn, fg, l, S, final):
    M = h.shape[0]
    spt = S // TM
    row = lambda width: pl.BlockSpec((TM, width), lambda i: (i, 0))
    prev = lambda width: pl.BlockSpec((HALO, width), lambda i: (jnp.maximum(i * (TM // HALO) - 1, 0), 0))
    return pl.pallas_call(
        functools.partial(_ffn_kernel, final=final, spt=spt),
        grid=(M // TM,),
        in_specs=[row(D_MODEL), prev(D_MODEL), row(oa.shape[1]), prev(oa.shape[1]), row(ob.shape[1]),
                  prev(ob.shape[1]), _layer_spec(wo, l), _layer_spec(g, l), _layer_spec(wup, l), _layer_spec(cw, l),
                  _layer_spec(cb, l), _layer_spec(wdn, l), _const_spec(fg)],
        out_specs=row(D_MODEL),
        out_shape=jax.ShapeDtypeStruct((M, D_MODEL), F32),
        scratch_shapes=[pltpu.VMEM((HALO + TM, D_MODEL), BF16), pltpu.VMEM((D_MODEL // LANES, TM, LANES), F32),
                        pltpu.VMEM((TM, D_FF), BF16)],
        compiler_params=_cparams(("parallel",)),
        name="outproj_ffn",
    )(h, h, oa, oa, ob, ob, wo, g, wup, cw, cb, wdn, fg)


def _slabs(w, n, width, pad_to=LANES):
    lead = w.shape[:-1]
    w = w.reshape(lead + (n, width))
    w = jnp.pad(w, [(0, 0)] * (w.ndim - 1) + [(0, pad_to - width)])
    return w.reshape(lead + (n * pad_to,))


def _inproj_weights(w_in):
    G, dh = NSA_KV_HEADS, NSA_DH
    cols = lambda off, n: w_in[..., off:off + n]
    zeros = lambda n: jnp.zeros(w_in.shape[:-1] + (n,), w_in.dtype)
    kr = jnp.concatenate([zeros(MLA_NOPE), cols(OFF_KROPE, MLA_ROPE), zeros(LANES - MLA_QK)], axis=-1)
    gates = _slabs(cols(OFF_GT, 3 * NSA_HEADS), NSA_HEADS, 3, 4)
    gates = jnp.concatenate([gates, zeros(LANES - 4 * NSA_HEADS)], axis=-1)
    w = jnp.concatenate([cols(OFF_CQ, MLA_Q_RANK), cols(OFF_CKV, MLA_KV_RANK), kr,
                         gates, cols(OFF_KC, G * dh), cols(OFF_VC, G * dh),
                         _slabs(cols(OFF_KS, G * dh), G, dh), _slabs(cols(OFF_KW, G * dh), G, dh)], axis=-1)
    assert w.shape[-1] == W_ROW
    qn = cols(OFF_QN, NSA_HEADS * dh) * (NSA_DH ** -0.5 * LOG2E)
    wvt = jnp.concatenate([cols(OFF_VS, G * dh), cols(OFF_VW, G * dh)], axis=-1)
    return w.astype(BF16), jnp.swapaxes(qn, 1, 2).astype(BF16), jnp.swapaxes(wvt, 1, 2).astype(BF16)


def _bf16_terms(x, n):
    terms = []
    for _ in range(n):
        bits = np.array(x, np.float32).view(np.uint32)
        bits = (bits + np.uint32(0x7FFF) + ((bits >> np.uint32(16)) & np.uint32(1))) & np.uint32(0xFFFF0000)
        terms.append(float(bits.view(np.float32)))
        x = x - terms[-1]
    return terms


N_POS_TERMS = 3


def _pos_features(pos, n_slabs):
    out = np.zeros((len(pos), n_slabs * LANES), np.float32)
    for s in range(n_slabs):
        base = s * LANES + NSA_DH
        out[:, base:base + N_POS_TERMS] = (pos // L_SEL)[:, None]
        out[:, base + N_POS_TERMS:base + 2 * N_POS_TERMS] = (pos % L_SEL)[:, None]
    return out


def _const_tables(S):
    slopes = 2.0 ** (-8.0 * np.arange(1, NSA_HEADS + 1) / NSA_HEADS)
    terms = np.array(_bf16_terms(LOG2E, N_POS_TERMS))
    pad = LANES - NSA_DH
    qaug = np.zeros((NSA_HEADS * pad, 1), np.float32)
    for h in range(NSA_HEADS):
        base = h * pad
        qaug[base:base + N_POS_TERMS, 0] = slopes[h] * L_SEL * terms
        qaug[base + N_POS_TERMS:base + 2 * N_POS_TERMS, 0] = slopes[h] * terms
    posf = _pos_features(np.arange(S), 4)
    nc = S // D_CMP
    feat = _pos_features(np.arange(nc) * D_CMP + L_CMP - 1, 2)
    n_cmp = (S - L_CMP) // D_CMP + 1
    nblk = S // L_SEL
    cs = np.arange(n_cmp) * D_CMP
    ss = np.arange(nblk) * L_SEL
    ov = ((cs[:, None] < ss[None, :] + L_SEL) & (cs[:, None] + L_CMP > ss[None, :])).astype(np.float32)
    ovt = np.zeros((nblk, nc), np.float32)
    ovt[:, :n_cmp] = ov.T
    assert nblk <= LANES
    onehot = (np.arange(S)[:, None] // L_SEL == np.arange(LANES)[None, :]).astype(np.float32)
    return (jnp.asarray(qaug), jnp.asarray(posf), jnp.asarray(feat), jnp.asarray(ovt, BF16),
            jnp.asarray(onehot, BF16))


def _rope_consts(S):
    half = MLA_ROPE // 2
    inv = (1.0 / (np.float32(ROPE_THETA) ** (np.arange(0, MLA_ROPE, 2, dtype=np.float32) / MLA_ROPE))).astype(np.float32)
    ang = np.arange(S, dtype=np.float32)[:, None] * inv[None, :]
    cos, sin = np.cos(ang), np.sin(ang)
    z = lambda w: np.zeros((S, w), np.float32)
    ct = np.concatenate([np.ones((S, MLA_NOPE), np.float32), cos, cos, z(LANES - MLA_QK)], axis=1)
    sa = np.concatenate([z(MLA_NOPE), -sin, z(LANES - MLA_NOPE - half)], axis=1)
    sb = np.concatenate([z(MLA_NOPE + half), sin, z(LANES - MLA_QK)], axis=1)
    return tuple(jnp.asarray(np.ascontiguousarray(t)) for t in (ct, sa, sb, cos.T, sin.T))


def _mla_weights(w_uq, w_ukv):
    wq = _slabs(w_uq, MLA_HEADS, MLA_QK) * (MLA_QK ** -0.5 * LOG2E)
    kv = w_ukv.reshape(w_ukv.shape[:-1] + (MLA_HEADS, MLA_NOPE + MLA_V))
    wk = _slabs(kv[..., :MLA_NOPE].reshape(w_ukv.shape[:-1] + (MLA_HEADS * MLA_NOPE,)), MLA_HEADS, MLA_NOPE)
    wvt = jnp.swapaxes(kv[..., MLA_NOPE:].reshape(w_ukv.shape[:-1] + (MLA_HEADS * MLA_V,)), 1, 2)
    return jnp.swapaxes(wq, 1, 2).astype(BF16), wk.astype(BF16), wvt.astype(BF16)


def _compress_weights(pos, w1, w2k, w2v):
    G, dh, half = NSA_KV_HEADS, NSA_DH, L_CMP // 2
    L = pos.shape[0]
    eye = jnp.eye(G, dtype=F32)

    def first_layer(w):
        return jnp.einsum('nclde,gk->nclgdke', w, eye).reshape(L, 2, half * G * dh, G * CMP_HID).astype(BF16)

    def pos_row(p):
        return jnp.broadcast_to(p[:, :, :, None, :], (L, 2, half, G, dh)).reshape(L, 2, 1, half * G * dh)

    w2kp = jnp.pad(w2k, ((0, 0), (0, 0), (0, LANES - dh)))
    w2k_bd = jnp.einsum('ned,gk->ngekd', w2kp, eye).reshape(L, G * CMP_HID, G * LANES).astype(BF16)
    w2vt_bd = jnp.einsum('ned,gk->ngdke', w2v, eye).reshape(L, G * dh, G * CMP_HID).astype(BF16)
    return (pos_row(pos[:, :, :half]), pos_row(pos[:, :, half:]), first_layer(w1[:, :, :half]),
            first_layer(w1[:, :, half:]), w2k_bd, w2vt_bd)


def kernel(x, attn_norm, w_in, q_norm, kv_norm, w_uq, w_ukv, cmp_pos_k, cmp_pos_v, cmp_k_w1, cmp_k_w2, cmp_v_w1,
           cmp_v_w2, w_o, ffn_norm, w_up, conv_w, conv_b, w_down, final_norm):
    B, S, D = x.shape
    M = B * S
    assert D == D_MODEL and S % TM == 0 and TQ_MLA == TKV and TQ_NSA == TKV and S % TKV == 0
    qaug, posf, feat, ovt, onehot = _const_tables(S)
    rope = _rope_consts(S)

    w_row, w_qt, w_vt = _inproj_weights(w_in)
    mqt, mk, mvt = _mla_weights(w_uq, w_ukv)
    pa, pb, cwa, cwb, w2k, w2vt = _compress_weights(jnp.stack([cmp_pos_k, cmp_pos_v], axis=1),
                                                    jnp.stack([cmp_k_w1, cmp_v_w1], axis=1), cmp_k_w2, cmp_v_w2)
    wo, wup, wdn = w_o.astype(BF16), w_up.astype(BF16), w_down.astype(BF16)
    g_attn, g_q, g_kv, g_ffn = (a[:, None, :] for a in (attn_norm, q_norm, kv_norm, ffn_norm))
    cb = conv_b[:, None, :]
    fg = final_norm[None, :]

    h = x.reshape(M, D)
    for l in range(DEPTH):
        gt, kcr, vcr, qnt, ksw, vt, qt, k, vmt = _inproj(h, g_attn, w_row, w_qt, w_vt, qaug, posf,
                                                          (g_q, g_kv, mqt, mk, mvt), rope, l, B, S)
        o_mla = _mla_attn(qt, k.reshape(B, S, -1), vmt, B, S)
        kc, vct = _compress(kcr.reshape(B, S, -1), vcr.reshape(B, S, -1), pa, pb, cwa, cwb, w2k, w2vt, feat, l)
        o_nsa = _nsa_attn(qnt, kc, vct, ksw, vt, gt, ovt, onehot, B, S)
        h = _outproj_ffn(h, o_mla.reshape(M, -1), o_nsa.reshape(M, -1), wo, g_ffn, wup, conv_w, cb, wdn, fg, l, S,
                         final=(l == DEPTH - 1))
    return h.reshape(B, S, D)
```

```python
import functools

import numpy as np
import jax
import jax.numpy as jnp
from jax import lax
from jax.experimental import pallas as pl
from jax.experimental.pallas import tpu as pltpu

F32 = jnp.float32
BF16 = jnp.bfloat16

D_MODEL = 1024
DEPTH = 2
EPS = 1e-6
NEG_INF = -1e30
LOG2E = float(np.log2(np.e))
LANES = 128

MLA_HEADS = 8
MLA_NOPE = 64
MLA_ROPE = 32
MLA_V = 64
MLA_Q_RANK = 384
MLA_KV_RANK = 256
MLA_QK = MLA_NOPE + MLA_ROPE
ROPE_THETA = 10000.0

NSA_HEADS = 8
NSA_KV_HEADS = 2
NSA_REP = NSA_HEADS // NSA_KV_HEADS
NSA_DH = 64
L_CMP = 32
D_CMP = 16
CMP_HID = 128
L_SEL = 64
N_SEL = 16
W_WIN = 512

D_FF = 2816
CONV_W = 3

_SIZES = [MLA_Q_RANK, MLA_KV_RANK, MLA_ROPE, NSA_HEADS * NSA_DH] + [NSA_KV_HEADS * NSA_DH] * 6 + [3 * NSA_HEADS]
_OFFS = [0] + [int(o) for o in np.cumsum(_SIZES)]
(OFF_CQ, OFF_CKV, OFF_KROPE, OFF_QN, OFF_KC, OFF_VC, OFF_KS, OFF_VS, OFF_KW, OFF_VW, OFF_GT, D_IN) = _OFFS

TM = 512
TKV = 256
TQ_MLA = 256
TQ_NSA = 256
VMEM_LIMIT = 56 * 1024 * 1024

W_ZA = 6 * LANES
W_ZB = 3 * LANES
W_KS = 4 * LANES
W_ROW = W_ZA + W_ZB + W_KS
W_QN = NSA_HEADS * LANES


def _cparams(sem):
    return pltpu.CompilerParams(dimension_semantics=sem, vmem_limit_bytes=VMEM_LIMIT)


def _layer_spec(a, l):
    return pl.BlockSpec((1,) + a.shape[1:], lambda *_: (l,) + (0,) * (a.ndim - 1))


def _const_spec(a):
    return pl.BlockSpec(a.shape, lambda *_: (0,) * a.ndim)


def _dot(a, b):
    return jnp.dot(a, b, preferred_element_type=F32)


def _dot_nt(a, b):
    return lax.dot_general(a, b, (((1,), (1,)), ((), ())), preferred_element_type=F32)


def _rms(x, g):
    return x * lax.rsqrt(jnp.mean(x * x, axis=-1, keepdims=True) + EPS) * g


def _store_tiles(ref, xt):
    for c in range(TM // TKV):
        ref[0, c] = xt[:, c * TKV:(c + 1) * TKV]


def _inproj_kernel(x_ref, g_ref, w_ref, wqt_ref, wvt_ref, qaug_ref, posf_ref,
                   qg_ref, kg_ref, mwqt_ref, mwk_ref, mwvt_ref, ct_ref, sa_ref, sb_ref, cost_ref, sint_ref,
                   gt_ref, kcr_ref, vcr_ref, qnt_ref, ksw_ref, vt_ref, mqt_ref, mk_ref, mvt_ref):
    n = _rms(x_ref[...], g_ref[0]).astype(BF16)
    c0, c1 = W_ZA, W_ZA + W_ZB
    cq, ckv, kr = _mla_norms(_dot(n, w_ref[0, :, 0:c0]), qg_ref, kg_ref, ct_ref, sa_ref, sb_ref)
    zb = _dot(n, w_ref[0, :, c0:c1])
    gt_ref[...] = zb[:, 0:LANES]
    kcr_ref[...] = zb[:, LANES:2 * LANES]
    vcr_ref[...] = zb[:, 2 * LANES:3 * LANES]
    ksw_ref[...] = (_dot(n, w_ref[0, :, c1:W_ROW]) + posf_ref[...]).astype(BF16)
    qt = _dot_nt(wqt_ref[0], n)
    slabs, pad = [], LANES - NSA_DH
    for h in range(NSA_HEADS):
        slabs += [qt[h * NSA_DH:(h + 1) * NSA_DH, :], jnp.broadcast_to(qaug_ref[h * pad:(h + 1) * pad, :], (pad, TM))]
    _store_tiles(qnt_ref, jnp.concatenate(slabs, axis=0).astype(BF16))
    _store_tiles(vt_ref, _dot_nt(wvt_ref[0], n).astype(BF16))
    _mla_project(cq, ckv, kr, mwqt_ref, mwk_ref, mwvt_ref, cost_ref, sint_ref, mqt_ref, mk_ref, mvt_ref)


def _inproj(h, g, w, wqt, wvt, qaug, posf, mla, rope, l, B, S):
    M = B * S
    spt = S // TM
    ct, sa, sb, cost, sint = rope
    row = lambda width: pl.BlockSpec((TM, width), lambda i: (i, 0))
    tiles = lambda rows: pl.BlockSpec((1, TM // TKV, rows, TKV), lambda i: (i // spt, i % spt, 0, 0))
    tab = pl.BlockSpec((TM, LANES), lambda i: (i % spt, 0))
    tabt = pl.BlockSpec((MLA_ROPE // 2, TM), lambda i: (0, i % spt))
    return pl.pallas_call(
        _inproj_kernel,
        grid=(M // TM,),
        in_specs=[row(D_MODEL), _layer_spec(g, l), _layer_spec(w, l), _layer_spec(wqt, l), _layer_spec(wvt, l),
                  _const_spec(qaug), pl.BlockSpec((TM, W_KS), lambda i: (i % spt, 0))]
                 + [_layer_spec(a, l) for a in mla] + [tab, tab, tab, tabt, tabt],
        out_specs=[row(LANES), row(LANES), row(LANES), tiles(W_QN), row(W_KS), tiles(4 * NSA_DH),
                   tiles(MLA_HEADS * LANES), row(MLA_HEADS * LANES), tiles(MLA_HEADS * MLA_V)],
        out_shape=[jax.ShapeDtypeStruct((M, LANES), F32),
                   jax.ShapeDtypeStruct((M, LANES), F32),
                   jax.ShapeDtypeStruct((M, LANES), F32),
                   jax.ShapeDtypeStruct((B, S // TKV, W_QN, TKV), BF16),
                   jax.ShapeDtypeStruct((M, W_KS), BF16),
                   jax.ShapeDtypeStruct((B, S // TKV, 4 * NSA_DH, TKV), BF16),
                   jax.ShapeDtypeStruct((B, S // TKV, MLA_HEADS * LANES, TKV), BF16),
                   jax.ShapeDtypeStruct((M, MLA_HEADS * LANES), BF16),
                   jax.ShapeDtypeStruct((B, S // TKV, MLA_HEADS * MLA_V, TKV), BF16)],
        compiler_params=_cparams(("parallel",)),
        name="attn_inproj",
    )(h, g, w, wqt, wvt, qaug, posf, *mla, ct, sa, sb, cost, sint)


def _rope_slab(x, ct, sa, sb):
    return x * ct + pltpu.roll(x, LANES - MLA_ROPE // 2, 1) * sa + pltpu.roll(x, MLA_ROPE // 2, 1) * sb


def _mla_norms(za, qg_ref, kg_ref, ct_ref, sa_ref, sb_ref):
    cq = _rms(za[:, 0:MLA_Q_RANK], qg_ref[0]).astype(BF16)
    ckv = _rms(za[:, MLA_Q_RANK:MLA_Q_RANK + MLA_KV_RANK], kg_ref[0]).astype(BF16)
    kr = _rope_slab(za[:, 5 * LANES:6 * LANES], ct_ref[...], sa_ref[...], sb_ref[...])
    return cq, ckv, kr


def _mla_project(cq, ckv, kr, wqt_ref, wk_ref, wvt_ref, cost_ref, sint_ref, qt_ref, k_ref, vt_ref):
    k = _dot(ckv, wk_ref[0])
    for h in range(MLA_HEADS):
        sl = slice(h * LANES, (h + 1) * LANES)
        k_ref[:, sl] = (k[:, sl] + kr).astype(BF16)
    qt = _dot_nt(wqt_ref[0], cq)
    cos, sin = cost_ref[...], sint_ref[...]
    half = MLA_ROPE // 2
    slabs = []
    for h in range(MLA_HEADS):
        base = h * LANES
        x1 = qt[base + MLA_NOPE:base + MLA_NOPE + half, :]
        x2 = qt[base + MLA_NOPE + half:base + MLA_QK, :]
        slabs += [qt[base:base + MLA_NOPE, :], x1 * cos - x2 * sin, x2 * cos + x1 * sin,
                  qt[base + MLA_QK:base + LANES, :]]
    _store_tiles(qt_ref, jnp.concatenate(slabs, axis=0).astype(BF16))
    _store_tiles(vt_ref, _dot_nt(wvt_ref[0], ckv).astype(BF16))


ONES_ROWS = 16


def _stage_scores(s, s_ref, mx_ref, b, h):
    mx_ref[b, h] = jnp.max(s, axis=0, keepdims=True)
    s_ref[b, h, 0:s.shape[0], :] = s


def _online_update(vt, s_ref, mx_ref, b, m_ref, acc_ref, h, valid=None):
    m_old = m_ref[h]
    mx = mx_ref[b, h] if valid is None else jnp.where(valid, mx_ref[b, h], -jnp.inf)
    m_new = jnp.maximum(m_old, mx)
    p = jnp.exp2(s_ref[b, h] - (m_new if valid is None else jnp.where(valid, m_new, jnp.inf))).astype(BF16)
    vt1 = jnp.concatenate([vt, jnp.ones((ONES_ROWS, vt.shape[1]), BF16)], axis=0)
    acc_ref[h] = jnp.exp2(m_old - m_new) * acc_ref[h] + _dot(vt1, p)
    m_ref[h] = m_new


def _normalised(acc_ref, h, dh):
    return acc_ref[h, 0:dh, :] / acc_ref[h, dh:dh + 1, :]


def _pipelined_sweep(i, n_heads, score, update, score_diag, tile0_staged=False):
    every = range(n_heads)

    if not tile0_staged:
        @pl.when(i >= 1)
        def _():
            score(0, 0, every)

    def pair(j):
        for h in every:
            score(2 * j + 1, 1, [h])
            update(2 * j, 0, [h])
        for h in every:
            score(2 * j + 2, 0, [h])
            update(2 * j + 1, 1, [h])

    n_pairs = i // 2

    @pl.loop(0, n_pairs // 4)
    def _(jj):
        for c in range(4):
            pair(4 * jj + c)

    @pl.when((n_pairs // 2) % 2 == 1)
    def _():
        pair(n_pairs // 4 * 4)
        pair(n_pairs // 4 * 4 + 1)

    def tail(odd_pair, odd_tile):
        if odd_pair:
            pair(n_pairs - 1)
        if odd_tile:
            for h in every:
                score_diag(1, [h])
                update(i - 1, 0, [h])
            update(i, 1, every)
        else:
            score_diag(0, every)
            update(i, 0, every)

    for odd_pair in (False, True):
        for odd_tile in (False, True):
            pl.when((n_pairs % 2 == int(odd_pair)) & (i % 2 == int(odd_tile)))(
                functools.partial(tail, odd_pair, odd_tile))


def _mla_attn_kernel(qt_ref, k_ref, vt_ref, o_ref, m_ref, acc_ref, s_ref, mx_ref):
    i = pl.program_id(1)
    tq = TQ_MLA
    m_ref[...] = jnp.full(m_ref.shape, -jnp.inf, F32)
    acc_ref[...] = jnp.zeros(acc_ref.shape, F32)

    every = range(MLA_HEADS)

    def score(kt, b, heads, keep=None):
        for h in heads:
            sl = slice(h * LANES, (h + 1) * LANES)
            k = k_ref[0, pl.ds(pl.multiple_of(kt * TKV, TKV), TKV), sl]
            s = _dot(k, qt_ref[0, 0, sl, :])
            if keep is not None:
                s = jnp.where(keep, s, NEG_INF)
            _stage_scores(s, s_ref, mx_ref, b, h)

    def update(kt, b, heads):
        for h in heads:
            _online_update(vt_ref[0, kt, h * MLA_V:(h + 1) * MLA_V, :], s_ref, mx_ref, b, m_ref, acc_ref, h)

    causal = lax.broadcasted_iota(jnp.int32, (TKV, tq), 0) <= lax.broadcasted_iota(jnp.int32, (TKV, tq), 1)
    _pipelined_sweep(i, MLA_HEADS, score, update, lambda b, heads: score(i, b, heads, causal))
    out = jnp.concatenate([_normalised(acc_ref, h, MLA_V) for h in range(MLA_HEADS)], axis=0)
    o_ref[0] = out.T.astype(BF16)


def _mla_attn(q, k, vt, B, S):
    nkt = S // TKV
    hw = MLA_HEADS * LANES
    return pl.pallas_call(
        _mla_attn_kernel,
        grid=(B, S // TQ_MLA),
        in_specs=[pl.BlockSpec((1, 1, hw, TQ_MLA), lambda b, i: (b, i, 0, 0)),
                  pl.BlockSpec((1, S, hw), lambda b, i: (b, 0, 0)),
                  pl.BlockSpec((1, nkt, MLA_HEADS * MLA_V, TKV), lambda b, i: (b, 0, 0, 0))],
        out_specs=pl.BlockSpec((1, TQ_MLA, MLA_HEADS * MLA_V), lambda b, i: (b, i, 0)),
        out_shape=jax.ShapeDtypeStruct((B, S, MLA_HEADS * MLA_V), BF16),
        scratch_shapes=[pltpu.VMEM((MLA_HEADS, 1, TQ_MLA), F32),
                        pltpu.VMEM((MLA_HEADS, MLA_V + ONES_ROWS, TQ_MLA), F32),
                        pltpu.VMEM((2, MLA_HEADS, TKV, TQ_MLA), F32), pltpu.VMEM((2, MLA_HEADS, 1, TQ_MLA), F32)],
        compiler_params=_cparams(("parallel", "arbitrary")),
        name="mla_attn",
    )(q, k, vt)


def _gelu_tanh(x):
    return 0.5 * x * (1.0 + jnp.tanh(np.sqrt(2.0 / np.pi).astype(np.float32) * (x + 0.044715 * (x * x * x))))


def _compress_kernel(kx_ref, vx_ref, pa_ref, pb_ref, wa_ref, wb_ref, w2k_ref, w2vt_ref, feat_ref, kc_ref, vct_ref):
    half = L_CMP // 2
    nc = kx_ref.shape[1] // half

    def hidden(x_ref, kv):
        x = jnp.concatenate([x_ref[0, pl.ds(l, nc, stride=half), :] for l in range(half)], axis=1)
        ya = _dot((x + pa_ref[0, kv]).astype(BF16), wa_ref[0, kv])
        yb = _dot((x + pb_ref[0, kv]).astype(BF16), wb_ref[0, kv])
        return _gelu_tanh(ya + pltpu.roll(yb, nc - 1, 0)).astype(BF16)

    kc_ref[0] = (_dot(hidden(kx_ref, 0), w2k_ref[0]) + feat_ref[...]).astype(BF16)
    vct_ref[0] = _dot_nt(w2vt_ref[0], hidden(vx_ref, 1)).astype(BF16)


def _compress(kx, vx, pa, pb, wa, wb, w2k, w2vt, feat, l):
    B, S, width = kx.shape
    nc = S // (L_CMP // 2)
    xs = pl.BlockSpec((1, S, width), lambda b: (b, 0, 0))
    return pl.pallas_call(
        _compress_kernel,
        grid=(B,),
        in_specs=[xs, xs] + [_layer_spec(a, l) for a in (pa, pb, wa, wb, w2k, w2vt)] + [_const_spec(feat)],
        out_specs=[pl.BlockSpec((1, nc, 2 * LANES), lambda b: (b, 0, 0)),
                   pl.BlockSpec((1, 2 * NSA_DH, nc), lambda b: (b, 0, 0))],
        out_shape=[jax.ShapeDtypeStruct((B, nc, 2 * LANES), BF16),
                   jax.ShapeDtypeStruct((B, 2 * NSA_DH, nc), BF16)],
        compiler_params=_cparams(("parallel",)),
        name="nsa_compress",
    )(kx, vx, pa, pb, wa, wb, w2k, w2vt, feat)


def _nsa_attn_kernel(qnt_ref, kc_ref, vct_ref, ks_ref, kw_ref, vst_ref, vwt_ref, gt_ref, ovt_ref, oh_ref,
                     o_ref, ocmp_ref, m_ref, acc_ref, mw_ref, accw_ref, s_ref, mx_ref, qs_ref):
    i = pl.program_id(1)
    tq = TQ_NSA
    G, R, dh = NSA_KV_HEADS, NSA_REP, NSA_DH
    nc = kc_ref.shape[1]
    nblk = ovt_ref.shape[0]
    t0 = i * tq
    tpos = t0 + lax.broadcasted_iota(jnp.int32, (1, tq), 1)
    q_of = lambda h: qnt_ref[0, 0, h * LANES:(h + 1) * LANES, :]

    for ref in (m_ref, mw_ref):
        ref[...] = jnp.full(ref.shape, -jnp.inf, F32)
    for ref in (acc_ref, accw_ref):
        ref[...] = jnp.zeros(ref.shape, F32)

    every = range(G * R)
    rel = lax.broadcasted_iota(jnp.int32, (TKV, tq), 0) - lax.broadcasted_iota(jnp.int32, (TKV, tq), 1)
    causal = rel <= 0

    def score(kt, b, heads, sel, keep=None):
        rows = pl.ds(pl.multiple_of(kt * TKV, TKV), TKV)
        for h in heads:
            g = h // R
            if sel:
                k = jnp.concatenate([ks_ref[0, rows, g * LANES:(g + 1) * LANES], oh_ref[rows, :]], axis=1)
            else:
                k = kw_ref[0, rows, g * LANES:(g + 1) * LANES]
            s = _dot(k, qs_ref[h] if sel else q_of(h))
            if keep is not None:
                s = jnp.where(keep, s, NEG_INF)
            _stage_scores(s, s_ref, mx_ref, b, h)

    def update(kt, b, heads, sel, valid=None):
        vt_ref, state = (vst_ref, (m_ref, acc_ref)) if sel else (vwt_ref, (mw_ref, accw_ref))
        for h in heads:
            g = h // R
            _online_update(vt_ref[0, kt, g * dh:(g + 1) * dh, :], s_ref, mx_ref, b, *state, h, valid)

    assert W_WIN == 2 * TKV
    kt1, kt2 = jnp.maximum(i - 1, 0), jnp.maximum(i - 2, 0)
    near = rel > 0

    nrow = lax.broadcasted_iota(jnp.int32, (nc, 1), 0)
    bias_c = jnp.where((tpos >= nrow * D_CMP + (L_CMP - 1)) & (nrow < nc - 1), 0.0, NEG_INF)
    has_cmp = tpos >= L_CMP - 1
    for h in every:
        g = h // R
        _stage_scores(_dot(kc_ref[0, :, g * LANES:(g + 1) * LANES], q_of(h)) + bias_c, s_ref, mx_ref, 0, h)
    score(i, 1, every, False, causal)
    for h in every:
        score(kt1, 2, [h], False)
        update(i, 1, [h], False)

    imps = [None] * G
    for h in every:
        g = h // R
        e = jnp.exp2(s_ref[0, h, 0:nc, :] - mx_ref[0, h])
        inv = jnp.where(has_cmp, 1.0 / jnp.sum(e, axis=0, keepdims=True), 0.0)
        both = _dot(jnp.concatenate([vct_ref[0, g * dh:(g + 1) * dh, :], ovt_ref[...]], axis=0), e.astype(BF16))
        ocmp_ref[h] = both[0:dh, :] * inv
        contrib = both[dh:dh + nblk, :] * inv
        imps[g] = contrib if imps[g] is None else imps[g] + contrib
        score(kt2, 3, [h], False, near)
        update(kt1, 2, [h], False, i >= 1)
    update(kt2, 3, every, False, i >= 2)

    cur = lax.shift_right_logical(tpos, 6)
    jrow = lax.broadcasted_iota(jnp.int32, (nblk, 1), 0)
    sub = lax.broadcasted_iota(jnp.int32, (8, 1), 0)
    n_slab = nblk // 8

    for g in range(G):
        imp = jnp.where(jrow > cur, -jnp.inf, imps[g])
        imp = jnp.where((jrow == 0) | (jrow == cur) | (jrow == cur - 1), jnp.inf, imp)
        slabs = [imp[8 * v:8 * v + 8, :] for v in range(n_slab)]
        ranks = [jnp.zeros((8, tq), F32) for _ in range(n_slab)]
        for jj in range(nblk):
            row = imp[jj:jj + 1, :]
            for v in range(n_slab):
                if v > jj // 8:
                    beats = row >= slabs[v]
                elif v < jj // 8:
                    beats = row > slabs[v]
                else:
                    beats = (row > slabs[v]) | ((row == slabs[v]) & (sub > jj % 8))
                ranks[v] = ranks[v] + jnp.where(beats, 1.0, 0.0)
        mb = jnp.where(jnp.concatenate(ranks, axis=0) < float(min(N_SEL, nblk)), 0.0, NEG_INF)
        mb = jnp.concatenate([mb, jnp.zeros((LANES - nblk, tq), F32)], axis=0).astype(BF16)
        for r in range(R):
            qs_ref[g * R + r] = jnp.concatenate([q_of(g * R + r), mb], axis=0)
        score(0, 0, range(g * R, (g + 1) * R), True)

    _pipelined_sweep(i, G * R, functools.partial(score, sel=True), functools.partial(update, sel=True),
                     lambda b, heads: score(i, b, heads, True, causal), tile0_staged=True)

    sig = 1.0 / (1.0 + jnp.exp(-gt_ref[0].T[0:G * 16, :]))
    outs = []
    for h in range(G * R):
        row = (h // R) * 16 + (h % R) * 4
        outs.append(sig[row:row + 1, :] * ocmp_ref[h] + sig[row + 1:row + 2, :] * _normalised(acc_ref, h, dh)
                    + sig[row + 2:row + 3, :] * _normalised(accw_ref, h, dh))
    o_ref[0] = jnp.concatenate(outs, axis=0).T.astype(BF16)


def _nsa_attn(qnt, kc, vct, ksw, vt, gt, ovt, oh, B, S):
    nkt = S // TKV
    nc = kc.shape[1]
    G = NSA_KV_HEADS
    ksw3 = ksw.reshape(B, S, W_KS)
    gt3 = gt.reshape(B, S, LANES)
    H, dh, tq = NSA_HEADS, NSA_DH, TQ_NSA
    state = [pltpu.VMEM((H, 1, tq), F32), pltpu.VMEM((H, dh + ONES_ROWS, tq), F32)]
    return pl.pallas_call(
        _nsa_attn_kernel,
        grid=(B, S // tq),
        in_specs=[pl.BlockSpec((1, 1, W_QN, tq), lambda b, i: (b, i, 0, 0)),
                  pl.BlockSpec((1, nc, G * LANES), lambda b, i: (b, 0, 0)),
                  pl.BlockSpec((1, G * dh, nc), lambda b, i: (b, 0, 0)),
                  pl.BlockSpec((1, S, G * LANES), lambda b, i: (b, 0, 0)),
                  pl.BlockSpec((1, S, G * LANES), lambda b, i: (b, 0, 1)),
                  pl.BlockSpec((1, nkt, G * dh, TKV), lambda b, i: (b, 0, 0, 0)),
                  pl.BlockSpec((1, nkt, G * dh, TKV), lambda b, i: (b, 0, 1, 0)),
                  pl.BlockSpec((1, tq, LANES), lambda b, i: (b, i, 0)),
                  _const_spec(ovt), _const_spec(oh)],
        out_specs=pl.BlockSpec((1, tq, H * dh), lambda b, i: (b, i, 0)),
        out_shape=jax.ShapeDtypeStruct((B, S, H * dh), BF16),
        scratch_shapes=([pltpu.VMEM((H, dh, tq), F32)] + state + state
                        + [pltpu.VMEM((4, H, TKV, tq), F32), pltpu.VMEM((4, H, 1, tq), F32),
                           pltpu.VMEM((H, 2 * LANES, tq), BF16)]),
        compiler_params=_cparams(("parallel", "arbitrary")),
        name="nsa_attn",
    )(qnt, kc, vct, ksw3, ksw3, vt, vt, gt3, ovt, oh)


FF_CHUNK = 256
HALO = 16
SUB = 8
QV = TM // SUB


def _ffn_kernel(x_ref, xp_ref, a_ref, ap_ref, b_ref, bp_ref, wo_ref, g_ref, wup_ref, cw_ref, cb_ref, wdn_ref, fg_ref,
                o_ref, n_ref, p_ref, act_ref, *, final, spt):
    assert CONV_W == 3
    i = pl.program_id(0)
    ka = a_ref.shape[1]
    n_slab = D_MODEL // LANES

    def attn_residual(h_rows, a_rows, b_rows):
        return h_rows[...] + _dot(a_rows[...], wo_ref[0, 0:ka, :]) + _dot(b_rows[...], wo_ref[0, ka:, :])

    o_ref[...] = attn_residual(x_ref, a_ref, b_ref)
    keep = jnp.where(i % spt == 0, 0.0, 1.0)
    n_ref[0:HALO, :] = (_rms(attn_residual(xp_ref, ap_ref, bp_ref), g_ref[0]) * keep).astype(BF16)
    nt = _rms(o_ref[...], g_ref[0])
    for c in range(n_slab):
        for s in range(SUB):
            p_ref[c, pl.ds(s, QV, stride=SUB), :] = nt[s * QV:(s + 1) * QV, c * LANES:(c + 1) * LANES]
    for c in range(n_slab):
        n_ref[HALO:HALO + TM, c * LANES:(c + 1) * LANES] = p_ref[c].astype(BF16)

    first_sublane = lax.broadcasted_iota(jnp.int32, (SUB, 1), 0) == 0

    def conv(col):
        hx = _dot(n_ref[...], wup_ref[0, :, col:col + FF_CHUNK])
        main = hx[HALO:HALO + TM, :]
        wrap1 = jnp.where(first_sublane, hx[HALO - 1:HALO, :], pltpu.roll(main[TM - SUB:TM, :], 1, 0))
        wrap2 = jnp.where(first_sublane, hx[HALO - 2:HALO - 1, :], pltpu.roll(main[TM - 2 * SUB:TM - SUB, :], 1, 0))
        tap1 = jnp.concatenate([wrap1, main[0:TM - SUB, :]], axis=0)
        tap2 = jnp.concatenate([wrap2, wrap1, main[0:TM - 2 * SUB, :]], axis=0)
        cw = cw_ref[0, :, col:col + FF_CHUNK]
        return cb_ref[0, :, col:col + FF_CHUNK] + tap2 * cw[0:1, :] + tap1 * cw[1:2, :] + main * cw[2:3, :]

    for c in range(D_FF // FF_CHUNK):
        col = c * FF_CHUNK
        gate = conv(col)
        up = conv(D_FF + col)
        act_ref[:, col:col + FF_CHUNK] = (gate * (1.0 / (1.0 + jnp.exp(-gate))) * up).astype(BF16)
    down = _dot(act_ref[...], wdn_ref[0])
    for c in range(n_slab):
        p_ref[c] = down[:, c * LANES:(c + 1) * LANES]
    for c in range(n_slab):
        for s in range(SUB):
            rows, lanes = slice(s * QV, (s + 1) * QV), slice(c * LANES, (c + 1) * LANES)
            o_ref[rows, lanes] = o_ref[rows, lanes] + p_ref[c, pl.ds(s, QV, stride=SUB), :]
    if final:
        o_ref[...] = _rms(o_ref[...], fg_ref[...])


def _outproj_ffn(h, oa, ob, wo, g, wup, cw, cb, wdn, fg, l, S, final):
    M = h.shape[0]
    spt = S // TM
    row = lambda width: pl.BlockSpec((TM, width), lambda i: (i, 0))
    prev = lambda width: pl.BlockSpec((HALO, width), lambda i: (jnp.maximum(i * (TM // HALO) - 1, 0), 0))
    return pl.pallas_call(
        functools.partial(_ffn_kernel, final=final, spt=spt),
        grid=(M // TM,),
        in_specs=[row(D_MODEL), prev(D_MODEL), row(oa.shape[1]), prev(oa.shape[1]), row(ob.shape[1]),
                  prev(ob.shape[1]), _layer_spec(wo, l), _layer_spec(g, l), _layer_spec(wup, l), _layer_spec(cw, l),
                  _layer_spec(cb, l), _layer_spec(wdn, l), _const_spec(fg)],
        out_specs=row(D_MODEL),
        out_shape=jax.ShapeDtypeStruct((M, D_MODEL), F32),
        scratch_shapes=[pltpu.VMEM((HALO + TM, D_MODEL), BF16), pltpu.VMEM((D_MODEL // LANES, TM, LANES), F32),
                        pltpu.VMEM((TM, D_FF), BF16)],
        compiler_params=_cparams(("parallel",)),
        name="outproj_ffn",
    )(h, h, oa, oa, ob, ob, wo, g, wup, cw, cb, wdn, fg)


def _slabs(w, n, width, pad_to=LANES):
    lead = w.shape[:-1]
    w = w.reshape(lead + (n, width))
    w = jnp.pad(w, [(0, 0)] * (w.ndim - 1) + [(0, pad_to - width)])
    return w.reshape(lead + (n * pad_to,))


def _inproj_weights(w_in):
    G, dh = NSA_KV_HEADS, NSA_DH
    cols = lambda off, n: w_in[..., off:off + n]
    zeros = lambda n: jnp.zeros(w_in.shape[:-1] + (n,), w_in.dtype)
    kr = jnp.concatenate([zeros(MLA_NOPE), cols(OFF_KROPE, MLA_ROPE), zeros(LANES - MLA_QK)], axis=-1)
    gates = _slabs(cols(OFF_GT, 3 * NSA_HEADS), NSA_HEADS, 3, 4)
    gates = jnp.concatenate([gates, zeros(LANES - 4 * NSA_HEADS)], axis=-1)
    w = jnp.concatenate([cols(OFF_CQ, MLA_Q_RANK), cols(OFF_CKV, MLA_KV_RANK), kr,
                         gates, cols(OFF_KC, G * dh), cols(OFF_VC, G * dh),
                         _slabs(cols(OFF_KS, G * dh), G, dh), _slabs(cols(OFF_KW, G * dh), G, dh)], axis=-1)
    assert w.shape[-1] == W_ROW
    qn = cols(OFF_QN, NSA_HEADS * dh) * (NSA_DH ** -0.5 * LOG2E)
    wvt = jnp.concatenate([cols(OFF_VS, G * dh), cols(OFF_VW, G * dh)], axis=-1)
    return w.astype(BF16), jnp.swapaxes(qn, 1, 2).astype(BF16), jnp.swapaxes(wvt, 1, 2).astype(BF16)


def _bf16_terms(x, n):
    terms = []
    for _ in range(n):
        bits = np.array(x, np.float32).view(np.uint32)
        bits = (bits + np.uint32(0x7FFF) + ((bits >> np.uint32(16)) & np.uint32(1))) & np.uint32(0xFFFF0000)
        terms.append(float(bits.view(np.float32)))
        x = x - terms[-1]
    return terms


N_POS_TERMS = 3


def _pos_features(pos, n_slabs):
    out = np.zeros((len(pos), n_slabs * LANES), np.float32)
    for s in range(n_slabs):
        base = s * LANES + NSA_DH
        out[:, base:base + N_POS_TERMS] = (pos // L_SEL)[:, None]
        out[:, base + N_POS_TERMS:base + 2 * N_POS_TERMS] = (pos % L_SEL)[:, None]
    return out


def _const_tables(S):
    slopes = 2.0 ** (-8.0 * np.arange(1, NSA_HEADS + 1) / NSA_HEADS)
    terms = np.array(_bf16_terms(LOG2E, N_POS_TERMS))
    pad = LANES - NSA_DH
    qaug = np.zeros((NSA_HEADS * pad, 1), np.float32)
    for h in range(NSA_HEADS):
        base = h * pad
        qaug[base:base + N_POS_TERMS, 0] = slopes[h] * L_SEL * terms
        qaug[base + N_POS_TERMS:base + 2 * N_POS_TERMS, 0] = slopes[h] * terms
    posf = _pos_features(np.arange(S), 4)
    nc = S // D_CMP
    feat = _pos_features(np.arange(nc) * D_CMP + L_CMP - 1, 2)
    n_cmp = (S - L_CMP) // D_CMP + 1
    nblk = S // L_SEL
    cs = np.arange(n_cmp) * D_CMP
    ss = np.arange(nblk) * L_SEL
    ov = ((cs[:, None] < ss[None, :] + L_SEL) & (cs[:, None] + L_CMP > ss[None, :])).astype(np.float32)
    ovt = np.zeros((nblk, nc), np.float32)
    ovt[:, :n_cmp] = ov.T
    assert nblk <= LANES
    onehot = (np.arange(S)[:, None] // L_SEL == np.arange(LANES)[None, :]).astype(np.float32)
    return (jnp.asarray(qaug), jnp.asarray(posf), jnp.asarray(feat), jnp.asarray(ovt, BF16),
            jnp.asarray(onehot, BF16))


def _rope_consts(S):
    half = MLA_ROPE // 2
    inv = (1.0 / (np.float32(ROPE_THETA) ** (np.arange(0, MLA_ROPE, 2, dtype=np.float32) / MLA_ROPE))).astype(np.float32)
    ang = np.arange(S, dtype=np.float32)[:, None] * inv[None, :]
    cos, sin = np.cos(ang), np.sin(ang)
    z = lambda w: np.zeros((S, w), np.float32)
    ct = np.concatenate([np.ones((S, MLA_NOPE), np.float32), cos, cos, z(LANES - MLA_QK)], axis=1)
    sa = np.concatenate([z(MLA_NOPE), -sin, z(LANES - MLA_NOPE - half)], axis=1)
    sb = np.concatenate([z(MLA_NOPE + half), sin, z(LANES - MLA_QK)], axis=1)
    return tuple(jnp.asarray(np.ascontiguousarray(t)) for t in (ct, sa, sb, cos.T, sin.T))


def _mla_weights(w_uq, w_ukv):
    wq = _slabs(w_uq, MLA_HEADS, MLA_QK) * (MLA_QK ** -0.5 * LOG2E)
    kv = w_ukv.reshape(w_ukv.shape[:-1] + (MLA_HEADS, MLA_NOPE + MLA_V))
    wk = _slabs(kv[..., :MLA_NOPE].reshape(w_ukv.shape[:-1] + (MLA_HEADS * MLA_NOPE,)), MLA_HEADS, MLA_NOPE)
    wvt = jnp.swapaxes(kv[..., MLA_NOPE:].reshape(w_ukv.shape[:-1] + (MLA_HEADS * MLA_V,)), 1, 2)
    return jnp.swapaxes(wq, 1, 2).astype(BF16), wk.astype(BF16), wvt.astype(BF16)


def _compress_weights(pos, w1, w2k, w2v):
    G, dh, half = NSA_KV_HEADS, NSA_DH, L_CMP // 2
    L = pos.shape[0]
    eye = jnp.eye(G, dtype=F32)

    def first_layer(w):
        return jnp.einsum('nclde,gk->nclgdke', w, eye).reshape(L, 2, half * G * dh, G * CMP_HID).astype(BF16)

    def pos_row(p):
        return jnp.broadcast_to(p[:, :, :, None, :], (L, 2, half, G, dh)).reshape(L, 2, 1, half * G * dh)

    w2kp = jnp.pad(w2k, ((0, 0), (0, 0), (0, LANES - dh)))
    w2k_bd = jnp.einsum('ned,gk->ngekd', w2kp, eye).reshape(L, G * CMP_HID, G * LANES).astype(BF16)
    w2vt_bd = jnp.einsum('ned,gk->ngdke', w2v, eye).reshape(L, G * dh, G * CMP_HID).astype(BF16)
    return (pos_row(pos[:, :, :half]), pos_row(pos[:, :, half:]), first_layer(w1[:, :, :half]),
            first_layer(w1[:, :, half:]), w2k_bd, w2vt_bd)


def kernel(x, attn_norm, w_in, q_norm, kv_norm, w_uq, w_ukv, cmp_pos_k, cmp_pos_v, cmp_k_w1, cmp_k_w2, cmp_v_w1,
           cmp_v_w2, w_o, ffn_norm, w_up, conv_w, conv_b, w_down, final_norm):
    B, S, D = x.shape
    M = B * S
    assert D == D_MODEL and S % TM == 0 and TQ_MLA == TKV and TQ_NSA == TKV and S % TKV == 0
    qaug, posf, feat, ovt, onehot = _const_tables(S)
    rope = _rope_consts(S)

    w_row, w_qt, w_vt = _inproj_weights(w_in)
    mqt, mk, mvt = _mla_weights(w_uq, w_ukv)
    pa, pb, cwa, cwb, w2k, w2vt = _compress_weights(jnp.stack([cmp_pos_k, cmp_pos_v], axis=1),
                                                    jnp.stack([cmp_k_w1, cmp_v_w1], axis=1), cmp_k_w2, cmp_v_w2)
    wo, wup, wdn = w_o.astype(BF16), w_up.astype(BF16), w_down.astype(BF16)
    g_attn, g_q, g_kv, g_ffn = (a[:, None, :] for a in (attn_norm, q_norm, kv_norm, ffn_norm))
    cb = conv_b[:, None, :]
    fg = final_norm[None, :]

    h = x.reshape(M, D)
    for l in range(DEPTH):
        gt, kcr, vcr, qnt, ksw, vt, qt, k, vmt = _inproj(h, g_attn, w_row, w_qt, w_vt, qaug, posf,
                                                          (g_q, g_kv, mqt, mk, mvt), rope, l, B, S)
        o_mla = _mla_attn(qt, k.reshape(B, S, -1), vmt, B, S)
        kc, vct = _compress(kcr.reshape(B, S, -1), vcr.reshape(B, S, -1), pa, pb, cwa, cwb, w2k, w2vt, feat, l)
        o_nsa = _nsa_attn(qnt, kc, vct, ksw, vt, gt, ovt, onehot, B, S)
        h = _outproj_ffn(h, o_mla.reshape(M, -1), o_nsa.reshape(M, -1), wo, g_ffn, wup, conv_w, cb, wdn, fg, l, S,
                         final=(l == DEPTH - 1))
    return h.reshape(B, S, D)
```
